```python
import jax, jax.numpy as jnp
from jax import lax
import numpy as np

D_MODEL = 1024
BATCH = 16
SEQ = 4096
DEPTH = 4
DEC_BATCH = 16
DEC_SEQ = 2048
PAST_LEN = 128

GRID_W = 64
HEAD_DIM = 64
ROPE_THETA = 10000.0
NORM_EPS = 1e-6
Q_BLOCK = 128
A_Q_HEADS = 4
A_KV_HEADS = 2
CONV_CH = 256
CONV_WIDTH = 31
C_Q_HEADS = 4
C_KV_HEADS = 2
WINDOW = 128
GLA_HEADS = 4
GLA_DK = 64
GLA_DV = 64
GLA_RANK = 16
GLA_TAU = 16.0
GLA_CHUNK = 64
N_BRANCH = 4
BRANCH_W = 256
N_EXPERTS = 16
EXPERT_FF = 1024
EC_CAPACITY = 2
PLE_DIM = 256

A_COLS = (A_Q_HEADS + 2 * A_KV_HEADS) * HEAD_DIM
B_COLS = 2 * CONV_CH
C_COLS = (C_Q_HEADS + 2 * C_KV_HEADS) * HEAD_DIM
D_COLS = GLA_HEADS * (2 * GLA_DK + 2 * GLA_DV) + 2 * GLA_RANK
G_COLS = N_BRANCH * D_MODEL
IN_COLS = A_COLS + B_COLS + C_COLS + D_COLS + G_COLS

kernel_name = 'hybrid_parallel_encoder_ec_moe'

F32 = jnp.float32


def rmsnorm(x, g):
    xf = x.astype(F32)
    y = xf * lax.rsqrt(jnp.mean(xf * xf, axis=-1, keepdims=True) + NORM_EPS) * g.astype(F32)
    return y.astype(x.dtype)


def rope_cos_sin(pos, dim):
    inv = ROPE_THETA ** (-jnp.arange(0, dim, 2, dtype=F32) / dim)
    ang = pos.astype(F32)[:, None] * inv[None, :]
    return jnp.cos(ang), jnp.sin(ang)


def apply_rope(x, cos, sin):
    xf = x.astype(F32)
    half = xf.shape[-1] // 2
    x1, x2 = xf[..., :half], xf[..., half:]
    c = cos[None, :, None, :]
    s = sin[None, :, None, :]
    return jnp.concatenate([x1 * c - x2 * s, x1 * s + x2 * c], axis=-1).astype(x.dtype)


def axial_rope(x, row_cs, col_cs):
    h = HEAD_DIM // 2
    return jnp.concatenate([apply_rope(x[..., :h], *row_cs), apply_rope(x[..., h:], *col_cs)], axis=-1)


def position_tables(S):
    rows = S // GRID_W
    row = jnp.repeat(jnp.arange(rows), GRID_W)
    col = jnp.tile(jnp.arange(GRID_W), rows)
    t = jnp.arange(S)
    return (rope_cos_sin(row, HEAD_DIM // 2), rope_cos_sin(col, HEAD_DIM // 2),
            rope_cos_sin(t, HEAD_DIM))


def global_attention(q, k, v):
    B, S, Hq, d = q.shape
    Hkv = k.shape[2]
    G = Hq // Hkv
    nb = S // Q_BLOCK
    scale = d ** -0.5
    qb = jnp.moveaxis(q.reshape(B, nb, Q_BLOCK, Hkv, G, d), 1, 0)

    def block(qblk):
        s = jnp.einsum('bqkgd,bskd->bkgqs', qblk, k).astype(F32) * scale
        p = jax.nn.softmax(s, axis=-1).astype(v.dtype)
        return jnp.einsum('bkgqs,bskd->bqkgd', p, v)

    o = lax.map(block, qb)
    return jnp.moveaxis(o, 0, 1).reshape(B, S, Hq * d)


def window_attention(q, k, v, sink):
    B, S, Hq, d = q.shape
    Hkv = k.shape[2]
    G = Hq // Hkv
    nb = S // Q_BLOCK
    scale = d ** -0.5
    qb = q.reshape(B, nb, Q_BLOCK, Hkv, G, d)

    def band(t):
        tp = jnp.pad(t, ((0, 0), (Q_BLOCK, Q_BLOCK), (0, 0), (0, 0))).reshape(B, nb + 2, Q_BLOCK, Hkv, d)
        return jnp.concatenate([tp[:, :-2], tp[:, 1:-1], tp[:, 2:]], axis=2)

    kb, vb = band(k), band(v)
    s = jnp.einsum('bnqkgd,bnskd->bnkgqs', qb, kb).astype(F32) * scale
    blk = jnp.arange(nb)[:, None] * Q_BLOCK
    qpos = blk + jnp.arange(Q_BLOCK)[None, :]
    kpos = blk - Q_BLOCK + jnp.arange(3 * Q_BLOCK)[None, :]
    kp = kpos[:, None, :]
    valid = (jnp.abs(kp - qpos[:, :, None]) <= WINDOW) & (kp >= 0) & (kp < S)
    s = jnp.where(valid[None, :, None, None], s, -jnp.inf)
    sk = sink.astype(F32).reshape(Hkv, G)[None, None, :, :, None, None]
    m = jnp.maximum(jnp.max(s, axis=-1, keepdims=True), sk)
    e = jnp.exp(s - m)
    p = e / (jnp.sum(e, axis=-1, keepdims=True) + jnp.exp(sk - m))
    o = jnp.einsum('bnkgqs,bnskd->bnqkgd', p.astype(v.dtype), vb)
    return o.reshape(B, S, Hq * d)


def conv_module(u, w_dw, b_dw, ln_g, ln_b):
    a, g = jnp.split(u, 2, axis=-1)
    z = a * jax.nn.sigmoid(g)
    z = lax.conv_general_dilated(z, w_dw.astype(z.dtype)[:, None, :], window_strides=(1,),
                                 padding=[(CONV_WIDTH // 2, CONV_WIDTH // 2)],
                                 dimension_numbers=('NWC', 'WIO', 'NWC'),
                                 feature_group_count=CONV_CH)
    zf = z.astype(F32) + b_dw.astype(F32)
    mu = jnp.mean(zf, axis=-1, keepdims=True)
    var = jnp.mean(jnp.square(zf - mu), axis=-1, keepdims=True)
    zf = (zf - mu) * lax.rsqrt(var + NORM_EPS) * ln_g.astype(F32) + ln_b.astype(F32)
    return jax.nn.silu(zf).astype(u.dtype)


def gla_direction(q, k, v, log_a):
    B, S, H, dk = q.shape
    dv = v.shape[-1]
    L = GLA_CHUNK
    nc = S // L
    q, k, v, log_a = (t.reshape(B, nc, L, H, t.shape[-1]) for t in (q, k, v, log_a))
    b = jnp.cumsum(log_a, axis=2)
    b_mid = b[:, :, L // 2 - 1:L // 2]
    b_last = b[:, :, -1]
    att = jnp.einsum('bcihd,bcjhd->bchij', q * jnp.exp(b - b_mid), k * jnp.exp(b_mid - b))
    tri = jnp.tril(jnp.ones((L, L), dtype=bool))
    att = jnp.where(tri, att, 0.0)
    o_intra = jnp.einsum('bchij,bcjhe->bcihe', att, v)
    kv = jnp.einsum('bcjhd,bcjhe->bchde', k * jnp.exp(b_last[:, :, None] - b), v)

    def step(state, inp):
        dec, kv_c = inp
        return jnp.exp(dec)[..., None] * state + kv_c, state

    s0 = jnp.zeros((B, H, dk, dv), q.dtype)
    _, prev = lax.scan(step, s0, (jnp.moveaxis(b_last, 1, 0), jnp.moveaxis(kv, 1, 0)))
    prev = jnp.moveaxis(prev, 0, 1)
    o_inter = jnp.einsum('bcihd,bchde->bcihe', q * jnp.exp(b), prev)
    return (o_intra + o_inter).reshape(B, S, H, dv)


def gla_mixer(qd, kd, vd, rd, zlr, w2, b2, gn):
    B, S, _ = qd.shape
    q = qd.astype(F32).reshape(B, S, GLA_HEADS, GLA_DK) * (GLA_DK ** -0.5)
    k = kd.astype(F32).reshape(B, S, GLA_HEADS, GLA_DK)
    v = vd.astype(F32).reshape(B, S, GLA_HEADS, GLA_DV)
    z = zlr.astype(F32).reshape(B, S, 2, GLA_RANK)
    la = jax.nn.log_sigmoid(jnp.einsum('bsnr,nrc->bsnc', z, w2.astype(F32)) + b2.astype(F32)) / GLA_TAU
    la_f = la[:, :, 0].reshape(B, S, GLA_HEADS, GLA_DK)
    la_b = la[:, :, 1].reshape(B, S, GLA_HEADS, GLA_DK)
    flip = lambda t: jnp.flip(t, axis=1)
    o = gla_direction(q, k, v, la_f) + flip(gla_direction(flip(q), flip(k), flip(v), flip(la_b)))
    o = o * lax.rsqrt(jnp.mean(o * o, axis=-1, keepdims=True) + NORM_EPS) * gn.astype(F32)
    o = o.reshape(B, S, GLA_HEADS * GLA_DV) * jax.nn.silu(rd.astype(F32))
    return o.astype(qd.dtype)


def mixer_block(h, w_in, qk_g, sink, dw, dwb, lng, lnb, g2, gb2, gn, w_branch, w_out, tables):
    B, S, _ = h.shape
    row_cs, col_cs, t_cs = tables
    u = h @ w_in
    sizes = [A_Q_HEADS * HEAD_DIM, A_KV_HEADS * HEAD_DIM, A_KV_HEADS * HEAD_DIM,
             B_COLS,
             C_Q_HEADS * HEAD_DIM, C_KV_HEADS * HEAD_DIM, C_KV_HEADS * HEAD_DIM,
             GLA_HEADS * GLA_DK, GLA_HEADS * GLA_DK, GLA_HEADS * GLA_DV, GLA_HEADS * GLA_DV,
             2 * GLA_RANK, G_COLS]
    (qa, ka, va, ub, qc, kc, vc, qd, kd, vd, rd, zlr, gates) = jnp.split(u, np.cumsum(sizes)[:-1], axis=-1)
    qa = axial_rope(rmsnorm(qa.reshape(B, S, A_Q_HEADS, HEAD_DIM), qk_g[0]), row_cs, col_cs)
    ka = axial_rope(rmsnorm(ka.reshape(B, S, A_KV_HEADS, HEAD_DIM), qk_g[1]), row_cs, col_cs)
    oa = global_attention(qa, ka, va.reshape(B, S, A_KV_HEADS, HEAD_DIM))
    ob = conv_module(ub, dw, dwb, lng, lnb)
    qc = apply_rope(qc.reshape(B, S, C_Q_HEADS, HEAD_DIM), *t_cs)
    kc = apply_rope(kc.reshape(B, S, C_KV_HEADS, HEAD_DIM), *t_cs)
    oc = window_attention(qc, kc, vc.reshape(B, S, C_KV_HEADS, HEAD_DIM), sink)
    od = gla_mixer(qd, kd, vd, rd, zlr, g2, gb2, gn)
    gates = jax.nn.sigmoid(gates.reshape(B, S, N_BRANCH, D_MODEL))
    merged = 0.0
    for i, o in enumerate((oa, ob, oc, od)):
        merged = merged + gates[:, :, i] * (o @ w_branch[i])
    return merged @ w_out


def expert_choice_ffn(h, w_router, wg, wu, wd):
    B, S, D = h.shape
    N = B * S
    hf = h.reshape(N, D)
    aff = jax.nn.softmax((hf @ w_router).astype(F32), axis=-1)
    cap = max(1, EC_CAPACITY * N // N_EXPERTS)
    gate, idx = lax.top_k(aff.T, cap)
    xe = hf[idx]
    hid = jax.nn.silu(jnp.einsum('ecd,edf->ecf', xe, wg)) * jnp.einsum('ecd,edf->ecf', xe, wu)
    ye = jnp.einsum('ecf,efd->ecd', hid, wd) * gate[..., None].astype(h.dtype)
    out = jnp.zeros_like(hf).at[idx.reshape(-1)].add(ye.reshape(-1, D))
    return out.reshape(B, S, D)


def trunk(x, p, weights):
    (norm_mix, w_in, qk_norm, sink_logit, conv_dw, conv_dw_b, conv_ln_g, conv_ln_b,
     gla_w2, gla_b2, gla_norm, w_branch, w_out, norm_ffn, w_router, w_gate_e, w_up_e,
     w_down_e, norm_ple, w_ple_gate, w_ple_proj, norm_final) = weights
    tables = position_tables(x.shape[1])
    for i in range(DEPTH):
        h = rmsnorm(x, norm_mix[i])
        x = x + mixer_block(h, w_in[i], qk_norm[i], sink_logit[i], conv_dw[i], conv_dw_b[i],
                            conv_ln_g[i], conv_ln_b[i], gla_w2[i], gla_b2[i], gla_norm[i],
                            w_branch[i], w_out[i], tables)
        x = x + expert_choice_ffn(rmsnorm(x, norm_ffn[i]), w_router[i], w_gate_e[i], w_up_e[i], w_down_e[i])
        x = x + jax.nn.sigmoid(rmsnorm(x, norm_ple[i]) @ w_ple_gate[i]) * (p[i] @ w_ple_proj[i])
    return rmsnorm(x, norm_final)


def setup_inputs(seed: int = 0) -> dict:
    key = jax.random.key(seed)
    ks = jax.random.split(key, 32)
    nrm = lambda k, shape, scale: jax.random.normal(k, shape, F32) * scale
    return {
        'x_prompt': nrm(ks[0], (BATCH, SEQ, D_MODEL), 1.0),
        'x_sample': nrm(ks[1], (DEC_BATCH, DEC_SEQ, D_MODEL), 1.0),
        'p_prompt': nrm(ks[2], (DEPTH, BATCH, SEQ, PLE_DIM), 1.0),
        'p_sample': nrm(ks[3], (DEPTH, DEC_BATCH, DEC_SEQ, PLE_DIM), 1.0),
        'norm_mix': 1.0 + nrm(ks[4], (DEPTH, D_MODEL), 0.02),
        'w_in': nrm(ks[5], (DEPTH, D_MODEL, IN_COLS), D_MODEL ** -0.5),
        'qk_norm': 1.0 + nrm(ks[6], (DEPTH, 2, HEAD_DIM), 0.02),
        'sink_logit': nrm(ks[7], (DEPTH, C_Q_HEADS), 0.5),
        'conv_dw': nrm(ks[8], (DEPTH, CONV_WIDTH, CONV_CH), CONV_WIDTH ** -0.5),
        'conv_dw_b': nrm(ks[9], (DEPTH, CONV_CH), 0.02),
        'conv_ln_g': 1.0 + nrm(ks[10], (DEPTH, CONV_CH), 0.02),
        'conv_ln_b': nrm(ks[11], (DEPTH, CONV_CH), 0.02),
        'gla_w2': nrm(ks[12], (DEPTH, 2, GLA_RANK, GLA_HEADS * GLA_DK), GLA_RANK ** -0.5),
        'gla_b2': nrm(ks[13], (DEPTH, 2, GLA_HEADS * GLA_DK), 0.02),
        'gla_norm': 1.0 + nrm(ks[14], (DEPTH, GLA_DV), 0.02),
        'w_branch': nrm(ks[15], (DEPTH, N_BRANCH, BRANCH_W, D_MODEL), BRANCH_W ** -0.5),
        'w_out': nrm(ks[16], (DEPTH, D_MODEL, D_MODEL), D_MODEL ** -0.5),
        'norm_ffn': 1.0 + nrm(ks[17], (DEPTH, D_MODEL), 0.02),
        'w_router': nrm(ks[18], (DEPTH, D_MODEL, N_EXPERTS), D_MODEL ** -0.5),
        'w_gate_e': nrm(ks[19], (DEPTH, N_EXPERTS, D_MODEL, EXPERT_FF), D_MODEL ** -0.5),
        'w_up_e': nrm(ks[20], (DEPTH, N_EXPERTS, D_MODEL, EXPERT_FF), D_MODEL ** -0.5),
        'w_down_e': nrm(ks[21], (DEPTH, N_EXPERTS, EXPERT_FF, D_MODEL), EXPERT_FF ** -0.5),
        'norm_ple': 1.0 + nrm(ks[22], (DEPTH, D_MODEL), 0.02),
        'w_ple_gate': nrm(ks[23], (DEPTH, D_MODEL, D_MODEL), D_MODEL ** -0.5),
        'w_ple_proj': nrm(ks[24], (DEPTH, PLE_DIM, D_MODEL), PLE_DIM ** -0.5),
        'norm_final': 1.0 + nrm(ks[25], (D_MODEL,), 0.02),
    }


def reference(x_prompt, x_sample, p_prompt, p_sample, norm_mix, w_in, qk_norm, sink_logit,
              conv_dw, conv_dw_b, conv_ln_g, conv_ln_b, gla_w2, gla_b2, gla_norm, w_branch,
              w_out, norm_ffn, w_router, w_gate_e, w_up_e, w_down_e, norm_ple, w_ple_gate,
              w_ple_proj, norm_final):
    weights = (norm_mix, w_in, qk_norm, sink_logit, conv_dw, conv_dw_b, conv_ln_g, conv_ln_b,
               gla_w2, gla_b2, gla_norm, w_branch, w_out, norm_ffn, w_router, w_gate_e, w_up_e,
               w_down_e, norm_ple, w_ple_gate, w_ple_proj, norm_final)
    y_prompt = trunk(x_prompt, p_prompt, weights)
    y_sample = trunk(x_sample, p_sample, weights)
    return (y_prompt, y_sample)
```

```python
import functools

import jax
import jax.numpy as jnp
import numpy as np
from jax import lax
from jax.experimental import pallas as pl
from jax.experimental.pallas import tpu as pltpu

F32 = jnp.float32
BF16 = jnp.bfloat16

D_MODEL = 1024
DEPTH = 4
GRID_W = 64
HEAD_DIM = 64
ROPE_THETA = 10000.0
NORM_EPS = 1e-6
CONV_CH = 256
CONV_WIDTH = 31
CONV_HALO = 16
WINDOW = 128
GLA_TAU = 16.0
GLA_CHUNK = 64
GLA_BLOCK = 256
N_EXPERTS = 16
EC_CAPACITY = 2
PLE_DIM = 256
LANES = 128
VMEM_LIMIT = 56 * 1024 * 1024

_QA, _KA, _VA = (0, 256), (256, 384), (384, 512)
_UB = (512, 1024)
_QC, _KC, _VC = (1024, 1280), (1280, 1408), (1408, 1536)
_DD = (1536, 2560)
_ZL = (2560, 2592)
_GATES = 2592
_QPERM = np.concatenate([np.arange(0, 64), np.arange(128, 192), np.arange(64, 128), np.arange(192, 256)])
_HEAD_PERM = np.array([0, 2, 1, 3])


def _cparams(*sem):
    return pltpu.CompilerParams(dimension_semantics=sem, vmem_limit_bytes=VMEM_LIMIT)


def _dot(a, b):
    return jnp.dot(a, b, preferred_element_type=F32)


def _dot_nt(a, b):
    return lax.dot_general(a, b, (((1,), (1,)), ((), ())), preferred_element_type=F32)


def _rms(x, g):
    return x * lax.rsqrt(jnp.mean(x * x, axis=-1, keepdims=True) + NORM_EPS) * g


def _split2(x):
    hi = x.astype(BF16)
    lo = (x - hi.astype(F32)).astype(BF16)
    return hi, lo


def _group_ones(width, group):
    r = lax.broadcasted_iota(jnp.int32, (width, width), 0) // group
    c = lax.broadcasted_iota(jnp.int32, (width, width), 1) // group
    return jnp.where(r == c, 1.0, 0.0).astype(BF16)


def _group_meansq(x, group):
    hi, lo = _split2(x * x)
    ones = _group_ones(x.shape[1], group)
    return (_dot(hi, ones) + _dot(lo, ones)) * (1.0 / group)


def _rope(x, cos, sin_signed, half):
    width = x.shape[1]
    lane = lax.broadcasted_iota(jnp.int32, x.shape, 1)
    from_lo = pltpu.roll(x, half, 1)
    from_hi = pltpu.roll(x, width - half, 1)
    partner = jnp.where((lane & half) != 0, from_lo, from_hi)
    return x * cos + partner * sin_signed


def _proj_kernel(x_ref, g_ref, wa_ref, wb_ref, wc_ref, wd_ref, wz_ref, gq_ref, gk_ref,
                 ca_ref, sa_ref, cc_ref, sc_ref,
                 qa_ref, ka_ref, va_ref, zb_ref, qc_ref, kc_ref, vc_ref, d_ref, zl_ref):
    h = _rms(x_ref[...], g_ref[...]).astype(BF16)
    scale = HEAD_DIM ** -0.5

    ua = _dot(h, wa_ref[...])
    ca, sa = ca_ref[...], sa_ref[...]
    q = ua[:, :256]
    q = q * lax.rsqrt(_group_meansq(q, HEAD_DIM) + NORM_EPS) * gq_ref[...]
    q = _rope(q, jnp.concatenate([ca, ca], axis=1), jnp.concatenate([sa, sa], axis=1), 16)
    qa_ref[...] = (q * scale).astype(BF16)
    k = ua[:, 256:384]
    k = k * lax.rsqrt(_group_meansq(k, HEAD_DIM) + NORM_EPS) * gk_ref[...]
    ka_ref[...] = _rope(k, ca, sa, 16).astype(BF16)
    va_ref[...] = ua[:, 384:].astype(BF16)

    ub = _dot(h, wb_ref[...])
    zb_ref[...] = (ub[:, :CONV_CH] * jax.nn.sigmoid(ub[:, CONV_CH:])).astype(BF16)

    uc = _dot(h, wc_ref[...])
    cc, sc = cc_ref[...], sc_ref[...]
    qc = _rope(uc[:, :256], jnp.concatenate([cc, cc], axis=1), jnp.concatenate([sc, sc], axis=1), 32)
    qc_ref[...] = (qc * scale).astype(BF16)
    kc_ref[...] = _rope(uc[:, 256:384], cc, sc, 32).astype(BF16)
    vc_ref[...] = uc[:, 384:].astype(BF16)

    ud = _dot(h, wd_ref[...])
    d_ref[:, :256] = (ud[:, :256] * scale).astype(BF16)
    d_ref[:, 256:] = ud[:, 256:].astype(BF16)
    zl_ref[...] = _dot(h, wz_ref[...]).astype(BF16)


def _proj(x, lw, tabs, S):
    N = x.shape[0]
    T = min(512, S)
    per_row = S // T
    tok = lambda w: pl.BlockSpec((T, w), lambda i: (i, 0))
    full = lambda a: pl.BlockSpec(a.shape, lambda i: (0,) * a.ndim)
    tab = pl.BlockSpec((T, LANES), lambda i: (i % per_row, 0))
    widths = (256, 128, 128, 256, 256, 128, 128, 1024, 128)
    consts = (lw['g_mix'], lw['wa'], lw['wb'], lw['wc'], lw['wd'], lw['wz'], lw['gq'], lw['gk'])
    return pl.pallas_call(
        _proj_kernel,
        grid=(N // T,),
        in_specs=[tok(D_MODEL)] + [full(a) for a in consts] + [tab] * 4,
        out_specs=[tok(w) for w in widths],
        out_shape=[jax.ShapeDtypeStruct((N, w), BF16) for w in widths],
        compiler_params=_cparams("parallel"),
        name="proj",
    )(x, *consts, *tabs)


def _stack_heads(qb):
    lane = lax.broadcasted_iota(jnp.int32, qb.shape, 1)
    zero = jnp.zeros_like(qb)
    return jnp.concatenate([jnp.where(lane < HEAD_DIM, qb, zero), jnp.where(lane < HEAD_DIM, zero, qb)], axis=0)


def _unstack_heads(o, T):
    lane = lax.broadcasted_iota(jnp.int32, (T, LANES), 1)
    return jnp.where(lane < HEAD_DIM, o[:T], o[T:])


def _gattn_kernel(q_ref, k_ref, v_ref, o_ref, m_ref, l_ref, acc_ref, *, tk):
    T = q_ref.shape[0]
    S = k_ref.shape[0]
    for j in range(2):
        qst = _stack_heads(q_ref[:, j * LANES:(j + 1) * LANES])
        m_ref[...] = jnp.full(m_ref.shape, -jnp.inf, F32)
        l_ref[...] = jnp.zeros(l_ref.shape, F32)
        acc_ref[...] = jnp.zeros(acc_ref.shape, F32)

        def body(i, carry):
            start = pl.multiple_of(i * tk, tk)
            s = _dot_nt(qst, k_ref[pl.ds(start, tk), :])
            m_prev = m_ref[...]
            m_new = jnp.maximum(m_prev, jnp.max(s, axis=1, keepdims=True))
            alpha = jnp.exp(m_prev - m_new)
            p = jnp.exp(s - m_new)
            l_ref[...] = alpha * l_ref[...] + jnp.sum(p, axis=1, keepdims=True)
            acc_ref[...] = alpha * acc_ref[...] + _dot(p.astype(BF16), v_ref[pl.ds(start, tk), :])
            m_ref[...] = m_new
            return carry

        lax.fori_loop(0, S // tk, body, 0)
        o = acc_ref[...] / l_ref[...]
        o_ref[:, j * LANES:(j + 1) * LANES] = _unstack_heads(o, T).astype(BF16)


def _gattn(q, k, v):
    B, S, _ = q.shape
    T = min(256, S)
    tk = min(512, S)
    return pl.pallas_call(
        functools.partial(_gattn_kernel, tk=tk),
        grid=(B, S // T),
        in_specs=[pl.BlockSpec((None, T, 256), lambda b, i: (b, i, 0)),
                  pl.BlockSpec((None, S, LANES), lambda b, i: (b, 0, 0)),
                  pl.BlockSpec((None, S, LANES), lambda b, i: (b, 0, 0))],
        out_specs=pl.BlockSpec((None, T, 256), lambda b, i: (b, i, 0)),
        out_shape=jax.ShapeDtypeStruct((B, S, 256), BF16),
        scratch_shapes=[pltpu.VMEM((2 * T, 1), F32), pltpu.VMEM((2 * T, 1), F32),
                        pltpu.VMEM((2 * T, LANES), F32)],
        compiler_params=_cparams("parallel", "parallel"),
        name="gattn",
    )(q, k, v)


def _wattn_kernel(sink_ref, q_ref, k_ref, v_ref, o_ref, *, kw):
    T = q_ref.shape[0]
    S = k_ref.shape[0]
    i = pl.program_id(1)
    start = pl.multiple_of(jnp.clip(i * T - WINDOW, 0, S - kw), WINDOW)
    kwin = k_ref[pl.ds(start, kw), :]
    vwin = v_ref[pl.ds(start, kw), :]
    row = lax.broadcasted_iota(jnp.int32, (2 * T, kw), 0)
    col = lax.broadcasted_iota(jnp.int32, (2 * T, kw), 1)
    qpos = i * T + jnp.where(row < T, row, row - T)
    valid = jnp.abs(start + col - qpos) <= WINDOW
    first = lax.broadcasted_iota(jnp.int32, (2 * T, 1), 0) < T
    for j in range(2):
        qst = _stack_heads(q_ref[:, j * LANES:(j + 1) * LANES])
        s = jnp.where(valid, _dot_nt(qst, kwin), -jnp.inf)
        sk = jnp.where(first, sink_ref[2 * j], sink_ref[2 * j + 1])
        m = jnp.maximum(jnp.max(s, axis=1, keepdims=True), sk)
        e = jnp.exp(s - m)
        p = e / (jnp.sum(e, axis=1, keepdims=True) + jnp.exp(sk - m))
        o = _dot(p.astype(BF16), vwin)
        o_ref[:, j * LANES:(j + 1) * LANES] = _unstack_heads(o, T).astype(BF16)


def _wattn(q, k, v, sink):
    B, S, _ = q.shape
    T = min(256, S - 2 * WINDOW) if S > 2 * WINDOW else S
    kw = min(T + 2 * WINDOW, S)
    return pl.pallas_call(
        functools.partial(_wattn_kernel, kw=kw),
        grid=(B, S // T),
        in_specs=[pl.BlockSpec(memory_space=pltpu.SMEM),
                  pl.BlockSpec((None, T, 256), lambda b, i: (b, i, 0)),
                  pl.BlockSpec((None, S, LANES), lambda b, i: (b, 0, 0)),
                  pl.BlockSpec((None, S, LANES), lambda b, i: (b, 0, 0))],
        out_specs=pl.BlockSpec((None, T, 256), lambda b, i: (b, i, 0)),
        out_shape=jax.ShapeDtypeStruct((B, S, 256), BF16),
        compiler_params=_cparams("parallel", "parallel"),
        name="wattn",
    )(sink, q, k, v)


def _conv_kernel(zp_ref, zc_ref, zn_ref, w_ref, b_ref, g_ref, beta_ref, o_ref, buf_ref):
    T = zc_ref.shape[0]
    i = pl.program_id(1)
    H = CONV_HALO
    keep_prev = jnp.where(i > 0, 1.0, 0.0)
    keep_next = jnp.where(i < pl.num_programs(1) - 1, 1.0, 0.0)
    buf_ref[0:H, :] = zp_ref[...].astype(F32) * keep_prev
    buf_ref[H:H + T, :] = zc_ref[...].astype(F32)
    buf_ref[H + T:, :] = zn_ref[...].astype(F32) * keep_next
    acc = jnp.zeros((T, CONV_CH), F32)
    off = H - CONV_WIDTH // 2
    for tap in range(CONV_WIDTH):
        acc = acc + buf_ref[off + tap:off + tap + T, :] * w_ref[tap:tap + 1, :]
    z = acc + b_ref[...]
    mu = jnp.mean(z, axis=-1, keepdims=True)
    zc = z - mu
    var = jnp.mean(zc * zc, axis=-1, keepdims=True)
    y = zc * lax.rsqrt(var + NORM_EPS) * g_ref[...] + beta_ref[...]
    o_ref[...] = (y * jax.nn.sigmoid(y)).astype(BF16)


def _conv(z, lw):
    B, S, _ = z.shape
    T = min(512, S)
    H = CONV_HALO
    per = T // H
    last = S // H - 1
    full = lambda a: pl.BlockSpec(a.shape, lambda b, i: (0,) * a.ndim)
    consts = (lw['conv_w'], lw['conv_b'], lw['ln_g'], lw['ln_b'])
    return pl.pallas_call(
        _conv_kernel,
        grid=(B, S // T),
        in_specs=[pl.BlockSpec((None, H, CONV_CH), lambda b, i: (b, jnp.maximum(i * per - 1, 0), 0)),
                  pl.BlockSpec((None, T, CONV_CH), lambda b, i: (b, i, 0)),
                  pl.BlockSpec((None, H, CONV_CH), lambda b, i: (b, jnp.minimum((i + 1) * per, last), 0))]
                 + [full(a) for a in consts],
        out_specs=pl.BlockSpec((None, T, CONV_CH), lambda b, i: (b, i, 0)),
        out_shape=jax.ShapeDtypeStruct((B, S, CONV_CH), BF16),
        scratch_shapes=[pltpu.VMEM((T + 2 * H, CONV_CH), F32)],
        compiler_params=_cparams("parallel", "parallel"),
        name="conv",
    )(z, z, z, *consts)


def _gla_kernel(*refs, reverse, accumulate):
    if accumulate:
        d_ref, zl_ref, w2_ref, b2_ref, prev_ref, o_ref, st_ref = refs
    else:
        d_ref, zl_ref, w2_ref, b2_ref, o_ref, st_ref = refs
    TB = d_ref.shape[0]
    L = GLA_CHUNK
    nch = TB // L

    @pl.when(pl.program_id(1) == 0)
    def _():
        st_ref[...] = jnp.zeros(st_ref.shape, F32)

    pre = _dot(zl_ref[...], w2_ref[...]) + b2_ref[...]
    la = (jnp.minimum(pre, 0.0) - jnp.log(1.0 + jnp.exp(-jnp.abs(pre)))) * (1.0 / GLA_TAU)

    r = lax.broadcasted_iota(jnp.int32, (TB, TB), 0)
    c = lax.broadcasted_iota(jnp.int32, (TB, TB), 1)
    same = (r // L) == (c // L)
    base = (r // L) * L
    if reverse:
        tri = same & (c >= r)
        trimid = same & (c >= base + L // 2)
    else:
        tri = same & (c <= r)
        trimid = same & (c <= base + L // 2 - 1)
    sel = jnp.concatenate([jnp.where(m, 1.0, 0.0).astype(BF16) for m in (tri, trimid, same)], axis=0)
    hi = la.astype(BF16)
    r1 = la - hi.astype(F32)
    mid = r1.astype(BF16)
    lo = (r1 - mid.astype(F32)).astype(BF16)
    cums = _dot(sel, jnp.concatenate([hi, mid, lo], axis=1))
    cums = cums[:, :256] + cums[:, 256:512] + cums[:, 512:]
    b, bmid, blast = cums[:TB], cums[TB:2 * TB], cums[2 * TB:]

    q = d_ref[:, 0:256].astype(F32)
    k = d_ref[:, 256:512].astype(F32)
    v = d_ref[:, 512:768]
    qt = (q * jnp.exp(b - bmid)).astype(BF16)
    kt = (k * jnp.exp(bmid - b)).astype(BF16)
    qe = (q * jnp.exp(b)).astype(BF16)
    kl = (k * jnp.exp(blast - b)).astype(BF16)
    dec = jnp.exp(blast)

    r2 = lax.broadcasted_iota(jnp.int32, (2 * TB, TB), 0) % TB
    c2 = lax.broadcasted_iota(jnp.int32, (2 * TB, TB), 1)
    tri2 = ((r2 // L) == (c2 // L)) & ((c2 >= r2) if reverse else (c2 <= r2))
    rr =lax.broadcasted_iota(jnp.int32, (LANES, LANES), 0) // HEAD_DIM
    cc = lax.broadcasted_iota(jnp.int32, (LANES, LANES), 1) // HEAD_DIM
    head_diag = rr == cc
    rowid = lax.broadcasted_iota(jnp.int32, (TB, LANES), 0) // L
    order = range(nch - 1, -1, -1) if reverse else range(nch)

    for p in range(2):
        ls = slice(p * LANES, (p + 1) * LANES)
        vb = v[:, ls]
        att = jnp.where(tri2, _dot_nt(_stack_heads(qt[:, ls]), kt[:, ls]), 0.0)
        o_intra = _unstack_heads(_dot(att.astype(BF16), vb), TB)
        vt = vb.astype(F32).T.astype(BF16)
        klb = kl[:, ls]
        st = st_ref[p]
        o_inter = [None] * nch
        for ci in order:
            rows = slice(ci * L, (ci + 1) * L)
            o_inter[ci] = _dot_nt(qe[rows, ls], st.astype(BF16))
            kv_t = _dot(vt, jnp.where(rowid == ci, klb, jnp.zeros_like(klb)))
            st = st * dec[ci * L:ci * L + 1, ls] + jnp.where(head_diag, kv_t, 0.0)
        st_ref[p] = st
        o = o_intra + jnp.concatenate(o_inter, axis=0)
        if accumulate:
            o = o + prev_ref[:, ls]
        o_ref[:, ls] = o


def _gla_dir(d, zl, w2, b2, prev, reverse):
    B, S, _ = d.shape
    TB = min(GLA_BLOCK, S)
    nb = S // TB
    blk = (lambda b, i: (b, nb - 1 - i, 0)) if reverse else (lambda b, i: (b, i, 0))
    full = lambda a: pl.BlockSpec(a.shape, lambda b, i: (0,) * a.ndim)
    in_specs = [pl.BlockSpec((None, TB, 1024), blk), pl.BlockSpec((None, TB, LANES), blk), full(w2), full(b2)]
    args = [d, zl, w2, b2]
    if prev is not None:
        in_specs.append(pl.BlockSpec((None, TB, 256), blk))
        args.append(prev)
    return pl.pallas_call(
        functools.partial(_gla_kernel, reverse=reverse, accumulate=prev is not None),
        grid=(B, nb),
        in_specs=in_specs,
        out_specs=pl.BlockSpec((None, TB, 256), blk),
        out_shape=jax.ShapeDtypeStruct((B, S, 256), F32),
        scratch_shapes=[pltpu.VMEM((2, LANES, LANES), F32)],
        compiler_params=_cparams("parallel", "arbitrary"),
        name="gla_bwd" if reverse else "gla_fwd",
    )(*args)


def _merge_kernel(x_ref, oa_ref, ob_ref, oc_ref, og_ref, rd_ref, gmix_ref, wg_ref, wbr_ref, wout_ref,
                  gn_ref, gffn_ref, wrh_ref, wrl_ref, x1_ref, h2_ref, aff_ref):
    x = x_ref[...]
    h = _rms(x, gmix_ref[...]).astype(BF16)
    og = og_ref[...]
    od = og * lax.rsqrt(_group_meansq(og, HEAD_DIM) + NORM_EPS) * gn_ref[...]
    rd = rd_ref[...].astype(F32)
    od = (od * (rd * jax.nn.sigmoid(rd))).astype(BF16)
    branches = (oa_ref[...], ob_ref[...], oc_ref[...], od)
    merged = jnp.zeros(x.shape, F32)
    for i, o in enumerate(branches):
        gate = jax.nn.sigmoid(_dot(h, wg_ref[:, i * D_MODEL:(i + 1) * D_MODEL]))
        merged = merged + gate * _dot(o, wbr_ref[i])
    x1 = x + _dot(merged.astype(BF16), wout_ref[...])
    x1_ref[...] = x1
    h2 = _rms(x1, gffn_ref[...])
    h2_ref[...] = h2.astype(BF16)
    hh, hl = _split2(h2)
    logits = _dot(hh, wrh_ref[...]) + _dot(hh, wrl_ref[...]) + _dot(hl, wrh_ref[...])
    lt = logits.T[:N_EXPERTS]
    e = jnp.exp(lt - jnp.max(lt, axis=0, keepdims=True))
    aff_ref[...] = e / jnp.sum(e, axis=0, keepdims=True)


def _merge(x, oa, ob, oc, og, d, lw):
    N = x.shape[0]
    T = min(256, N)
    tok = lambda w: pl.BlockSpec((T, w), lambda i: (i, 0))
    full = lambda a: pl.BlockSpec(a.shape, lambda i: (0,) * a.ndim)
    consts = (lw['g_mix'], lw['wg'], lw['wbr'], lw['wout'], lw['gn'], lw['g_ffn'], lw['wr_hi'], lw['wr_lo'])
    return pl.pallas_call(
        _merge_kernel,
        grid=(N // T,),
        in_specs=[tok(D_MODEL), tok(256), tok(256), tok(256), tok(256),
                  pl.BlockSpec((T, 256), lambda i: (i, 3))] + [full(a) for a in consts],
        out_specs=[tok(D_MODEL), tok(D_MODEL), pl.BlockSpec((N_EXPERTS, T), lambda i: (0, i))],
        out_shape=[jax.ShapeDtypeStruct((N, D_MODEL), F32), jax.ShapeDtypeStruct((N, D_MODEL), BF16),
                   jax.ShapeDtypeStruct((N_EXPERTS, N), F32)],
        compiler_params=_cparams("parallel"),
        name="merge",
    )(x, oa, ob, oc, og, d, *consts)


def _ffn_kernel(x_ref, wg_ref, wu_ref, wd_ref, gate_ref, y_ref):
    x = x_ref[...]
    g = _dot(x, wg_ref[...])
    hid = (g * jax.nn.sigmoid(g)) * _dot(x, wu_ref[...])
    y_ref[...] = _dot(hid.astype(BF16), wd_ref[...]) * gate_ref[...]


def _ffn(xe, gate, lw):
    E, C, _ = xe.shape
    M = min(512, C)
    wspec = pl.BlockSpec((None, D_MODEL, D_MODEL), lambda e, s: (e, 0, 0))
    return pl.pallas_call(
        _ffn_kernel,
        grid=(E, C // M),
        in_specs=[pl.BlockSpec((None, M, D_MODEL), lambda e, s: (e, s, 0)), wspec, wspec, wspec,
                  pl.BlockSpec((None, M, 1), lambda e, s: (e, s, 0))],
        out_specs=pl.BlockSpec((None, M, D_MODEL), lambda e, s: (e, s, 0)),
        out_shape=jax.ShapeDtypeStruct((E, C, D_MODEL), F32),
        compiler_params=_cparams("parallel", "parallel"),
        name="ffn",
    )(xe, lw['we_g'], lw['we_u'], lw['we_d'], gate)


def _ple_kernel(x_ref, p_ref, g_ref, wgate_ref, wproj_ref, gfin_ref, o_ref, *, final):
    x = x_ref[...]
    h = _rms(x, g_ref[...]).astype(BF16)
    gate = jax.nn.sigmoid(_dot(h, wgate_ref[...]))
    y = x + gate * _dot(p_ref[...].astype(BF16), wproj_ref[...])
    if final:
        y = _rms(y, gfin_ref[...])
    o_ref[...] = y


def _ple(x, p, lw, gfin, final):
    N = x.shape[0]
    T = min(512, N)
    tok = lambda w: pl.BlockSpec((T, w), lambda i: (i, 0))
    full = lambda a: pl.BlockSpec(a.shape, lambda i: (0,) * a.ndim)
    consts = (lw['g_ple'], lw['w_pg'], lw['w_pp'], gfin)
    return pl.pallas_call(
        functools.partial(_ple_kernel, final=final),
        grid=(N // T,),
        in_specs=[tok(D_MODEL), tok(PLE_DIM)] + [full(a) for a in consts],
        out_specs=tok(D_MODEL),
        out_shape=jax.ShapeDtypeStruct((N, D_MODEL), F32),
        compiler_params=_cparams("parallel"),
        name="ple",
    )(x, p, *consts)


def _rope_tables(S):
    lane = np.arange(LANES)
    d = lane % HEAD_DIM
    t = jnp.arange(S)
    inv_a = ROPE_THETA ** (-jnp.arange(0, HEAD_DIM // 2, 2, dtype=F32) / (HEAD_DIM // 2))
    pos_a = jnp.where((d // 32 == 0)[None, :], (t // GRID_W)[:, None], (t % GRID_W)[:, None]).astype(F32)
    ang_a = pos_a * inv_a[d % 16][None, :]
    sign_a = jnp.where(d % 32 < 16, -1.0, 1.0)[None, :]
    inv_c = ROPE_THETA ** (-jnp.arange(0, HEAD_DIM, 2, dtype=F32) / HEAD_DIM)
    ang_c = t.astype(F32)[:, None] * inv_c[d % 32][None, :]
    sign_c = jnp.where(d < 32, -1.0, 1.0)[None, :]
    return (jnp.cos(ang_a), jnp.sin(ang_a) * sign_a, jnp.cos(ang_c), jnp.sin(ang_c) * sign_c)


def _layer_weights(i, w):
    w_in = w['w_in'][i]
    cols = lambda r: w_in[:, r[0]:r[1]]
    bf = lambda a: a.astype(BF16)
    row = lambda a: a.reshape(1, -1).astype(F32)
    wz = jnp.pad(cols(_ZL), ((0, 0), (0, LANES - (_ZL[1] - _ZL[0]))))
    w2 = w['gla_w2'][i]
    w2p = jnp.zeros((2, LANES, 256), F32).at[0, 0:16].set(w2[0]).at[1, 16:32].set(w2[1])
    wr = jnp.pad(w['w_router'][i], ((0, 0), (0, LANES - N_EXPERTS)))
    wr_hi = wr.astype(BF16)
    wbr = w['w_branch'][i]
    wbr = jnp.stack([wbr[0][_QPERM], wbr[1], wbr[2][_QPERM], wbr[3]])
    return dict(
        g_mix=row(w['norm_mix'][i]),
        wa=bf(jnp.concatenate([cols(_QA)[:, _QPERM], cols(_KA), cols(_VA)], axis=1)),
        wb=bf(cols(_UB)),
        wc=bf(jnp.concatenate([cols(_QC)[:, _QPERM], cols(_KC), cols(_VC)], axis=1)),
        wd=bf(cols(_DD)),
        wz=bf(wz),
        gq=row(jnp.tile(w['qk_norm'][i, 0], 4)),
        gk=row(jnp.tile(w['qk_norm'][i, 1], 2)),
        sink=w['sink_logit'][i][_HEAD_PERM].astype(F32),
        conv_w=w['conv_dw'][i].astype(F32),
        conv_b=row(w['conv_dw_b'][i]), ln_g=row(w['conv_ln_g'][i]), ln_b=row(w['conv_ln_b'][i]),
        w2=bf(w2p), b2=w['gla_b2'][i].reshape(2, 1, 256).astype(F32),
        gn=row(jnp.tile(w['gla_norm'][i], 4)),
        wg=bf(w_in[:, _GATES:]), wbr=bf(wbr), wout=bf(w['w_out'][i]),
        g_ffn=row(w['norm_ffn'][i]),
        wr_hi=wr_hi, wr_lo=(wr - wr_hi.astype(F32)).astype(BF16),
        we_g=bf(w['w_gate_e'][i]), we_u=bf(w['w_up_e'][i]), we_d=bf(w['w_down_e'][i]),
        g_ple=row(w['norm_ple'][i]), w_pg=bf(w['w_ple_gate'][i]), w_pp=bf(w['w_ple_proj'][i]),
    )


def _trunk(x3, p4, layers, gfin):
    B, S, _ = x3.shape
    N = B * S
    tabs = _rope_tables(S)
    x = x3.reshape(N, D_MODEL)
    cap = max(1, EC_CAPACITY * N // N_EXPERTS)
    for i, lw in enumerate(layers):
        qa, ka, va, zb, qc, kc, vc, d, zl = _proj(x, lw, tabs, S)
        b3 = lambda a: a.reshape(B, S, a.shape[-1])
        oa = _gattn(b3(qa), b3(ka), b3(va))
        ob = _conv(b3(zb), lw)
        oc = _wattn(b3(qc), b3(kc), b3(vc), lw['sink'])
        og = _gla_dir(b3(d), b3(zl), lw['w2'][0], lw['b2'][0], None, False)
        og = _gla_dir(b3(d), b3(zl), lw['w2'][1], lw['b2'][1], og, True)
        flat = lambda a: a.reshape(N, a.shape[-1])
        x1, h2, aff_t = _merge(x, flat(oa), flat(ob), flat(oc), flat(og), d, lw)
        gate, idx = lax.top_k(aff_t, cap)
        ye = _ffn(h2[idx], gate[..., None], lw)
        x2 = x1.at[idx.reshape(-1)].add(ye.reshape(-1, D_MODEL))
        x = _ple(x2, p4[i].reshape(N, PLE_DIM), lw, gfin, i == len(layers) - 1)
    return x.reshape(B, S, D_MODEL)


def kernel(x_prompt, x_sample, p_prompt, p_sample, norm_mix, w_in, qk_norm, sink_logit, conv_dw, conv_dw_b,
           conv_ln_g, conv_ln_b, gla_w2, gla_b2, gla_norm, w_branch, w_out, norm_ffn, w_router, w_gate_e,
           w_up_e, w_down_e, norm_ple, w_ple_gate, w_ple_proj, norm_final):
    w = dict(norm_mix=norm_mix, w_in=w_in, qk_norm=qk_norm, sink_logit=sink_logit, conv_dw=conv_dw,
             conv_dw_b=conv_dw_b, conv_ln_g=conv_ln_g, conv_ln_b=conv_ln_b, gla_w2=gla_w2, gla_b2=gla_b2,
             gla_norm=gla_norm, w_branch=w_branch, w_out=w_out, norm_ffn=norm_ffn, w_router=w_router,
             w_gate_e=w_gate_e, w_up_e=w_up_e, w_down_e=w_down_e, norm_ple=norm_ple,
             w_ple_gate=w_ple_gate, w_ple_proj=w_ple_proj)
    layers = [_layer_weights(i, w) for i in range(norm_mix.shape[0])]
    gfin = norm_final.reshape(1, -1).astype(F32)
    return (_trunk(x_prompt, p_prompt, layers, gfin), _trunk(x_sample, p_sample, layers, gfin))
```

```python
import functools

import jax
import jax.numpy as jnp
import numpy as np
from jax import lax
from jax.experimental import pallas as pl
from jax.experimental.pallas import tpu as pltpu

F32 = jnp.float32
BF16 = jnp.bfloat16

D_MODEL = 1024
DEPTH = 4
GRID_W = 64
HEAD_DIM = 64
ROPE_THETA = 10000.0
NORM_EPS = 1e-6
CONV_CH = 256
CONV_WIDTH = 31
CONV_HALO = 16
WINDOW = 128
GLA_TAU = 16.0
GLA_CHUNK = 64
GLA_BLOCK = 256
N_EXPERTS = 16
EC_CAPACITY = 2
PLE_DIM = 256
LANES = 128
VMEM_LIMIT = 56 * 1024 * 1024
LOG2E = 1.4426950408889634

_QA, _KA, _VA = (0, 256), (256, 384), (384, 512)
_UB = (512, 1024)
_QC, _KC, _VC = (1024, 1280), (1280, 1408), (1408, 1536)
_DD = (1536, 2560)
_ZL = (2560, 2592)
_GATES = 2592
_QPERM = np.concatenate([np.arange(0, 64), np.arange(128, 192), np.arange(64, 128), np.arange(192, 256)])
_HEAD_PERM = np.array([0, 2, 1, 3])


def _cparams(*sem):
    return pltpu.CompilerParams(dimension_semantics=sem, vmem_limit_bytes=VMEM_LIMIT)


def _dot(a, b):
    return jnp.dot(a, b, preferred_element_type=F32)


def _dot_nt(a, b):
    return lax.dot_general(a, b, (((1,), (1,)), ((), ())), preferred_element_type=F32)


def _rms(x, g):
    return x * lax.rsqrt(jnp.mean(x * x, axis=-1, keepdims=True) + NORM_EPS) * g


def _split2(x):
    hi = x.astype(BF16)
    lo = (x - hi.astype(F32)).astype(BF16)
    return hi, lo


def _group_ones(width, group):
    r = lax.broadcasted_iota(jnp.int32, (width, width), 0) // group
    c = lax.broadcasted_iota(jnp.int32, (width, width), 1) // group
    return jnp.where(r == c, 1.0, 0.0).astype(BF16)


def _group_meansq(x, group):
    hi, lo = _split2(x * x)
    ones = _group_ones(x.shape[1], group)
    return (_dot(hi, ones) + _dot(lo, ones)) * (1.0 / group)


def _rope(x, cos, sin_signed, half):
    width = x.shape[1]
    lane = lax.broadcasted_iota(jnp.int32, x.shape, 1)
    from_lo = pltpu.roll(x, half, 1)
    from_hi = pltpu.roll(x, width - half, 1)
    partner = jnp.where((lane & half) != 0, from_lo, from_hi)
    return x * cos + partner * sin_signed


def _proj_kernel(x_ref, g_ref, wa_ref, wb_ref, wc_ref, wd_ref, wz_ref, gq_ref, gk_ref,
                 ca_ref, sa_ref, cc_ref, sc_ref,
                 qa_ref, kat_ref, va_ref, zb_ref, qc_ref, kc_ref, vc_ref, d_ref, zl_ref):
    h = _rms(x_ref[...], g_ref[...]).astype(BF16)
    scale = HEAD_DIM ** -0.5

    ua = _dot(h, wa_ref[...])
    ca, sa = ca_ref[...], sa_ref[...]
    q = ua[:, :256]
    q = q * lax.rsqrt(_group_meansq(q, HEAD_DIM) + NORM_EPS) * gq_ref[...]
    q = _rope(q, jnp.concatenate([ca, ca], axis=1), jnp.concatenate([sa, sa], axis=1), 16)
    qa_ref[...] = (q * (scale * LOG2E)).astype(BF16)
    k = ua[:, 256:384]
    k = k * lax.rsqrt(_group_meansq(k, HEAD_DIM) + NORM_EPS) * gk_ref[...]
    kat_ref[...] = _rope(k, ca, sa, 16).T.astype(BF16)
    va_ref[...] = ua[:, 384:].astype(BF16)

    ub = _dot(h, wb_ref[...])
    zb_ref[...] = (ub[:, :CONV_CH] * jax.nn.sigmoid(ub[:, CONV_CH:])).astype(BF16)

    uc = _dot(h, wc_ref[...])
    cc, sc = cc_ref[...], sc_ref[...]
    qc = _rope(uc[:, :256], jnp.concatenate([cc, cc], axis=1), jnp.concatenate([sc, sc], axis=1), 32)
    qc_ref[...] = (qc * scale).astype(BF16)
    kc_ref[...] = _rope(uc[:, 256:384], cc, sc, 32).astype(BF16)
    vc_ref[...] = uc[:, 384:].astype(BF16)

    ud = _dot(h, wd_ref[...])
    d_ref[:, :256] = (ud[:, :256] * scale).astype(BF16)
    d_ref[:, 256:] = ud[:, 256:].astype(BF16)
    zl_ref[...] = _dot(h, wz_ref[...]).astype(BF16)


def _proj(x, lw, tabs, S):
    N = x.shape[0]
    T = min(512, S)
    per_row = S // T
    tok = lambda w: pl.BlockSpec((T, w), lambda i: (i, 0))
    full = lambda a: pl.BlockSpec(a.shape, lambda i: (0,) * a.ndim)
    tab = pl.BlockSpec((T, LANES), lambda i: (i % per_row, 0))
    widths = (256, 128, 128, 256, 256, 128, 128, 1024, 128)
    consts = (lw['g_mix'], lw['wa'], lw['wb'], lw['wc'], lw['wd'], lw['wz'], lw['gq'], lw['gk'])
    return pl.pallas_call(
        _proj_kernel,
        grid=(N // T,),
        in_specs=[tok(D_MODEL)] + [full(a) for a in consts] + [tab] * 4,
        out_specs=[tok(256), pl.BlockSpec((LANES, T), lambda i: (0, i))] + [tok(w) for w in widths[2:]],
        out_shape=[jax.ShapeDtypeStruct((N, 256), BF16), jax.ShapeDtypeStruct((LANES, N), BF16)]
                  + [jax.ShapeDtypeStruct((N, w), BF16) for w in widths[2:]],
        compiler_params=_cparams("parallel"),
        name="proj",
    )(x, *consts, *tabs)


def _stack_heads(qb):
    lane = lax.broadcasted_iota(jnp.int32, qb.shape, 1)
    zero = jnp.zeros_like(qb)
    return jnp.concatenate([jnp.where(lane < HEAD_DIM, qb, zero), jnp.where(lane < HEAD_DIM, zero, qb)], axis=0)


def _unstack_heads(o, T):
    lane = lax.broadcasted_iota(jnp.int32, (T, LANES), 1)
    return jnp.where(lane < HEAD_DIM, o[:T], o[T:])


def _gattn_kernel(q_ref, kt_ref, v_ref, o_ref, *, tk):
    T = q_ref.shape[0]
    S = kt_ref.shape[1]
    lane_v = lax.broadcasted_iota(jnp.int32, (tk, LANES), 1)
    one = jnp.ones((tk, LANES), BF16)
    qs = [_stack_heads(q_ref[:, j * LANES:(j + 1) * LANES]) for j in range(2)]
    m = [jnp.full((2 * T, 1), -jnp.inf, F32) for _ in range(2)]
    acc = [jnp.zeros((2 * T, LANES), F32) for _ in range(2)]
    for c in range(S // tk):
        kt = kt_ref[:, c * tk:(c + 1) * tk]
        v = v_ref[c * tk:(c + 1) * tk, :]
        va = jnp.where(lane_v < HEAD_DIM, v, one)
        vb = jnp.where(lane_v < HEAD_DIM, one, v)
        for j in range(2):
            s = _dot(qs[j], kt)
            m_new = jnp.maximum(m[j], jnp.max(s, axis=1, keepdims=True))
            alpha = jnp.exp2(m[j] - m_new)
            p = jnp.exp2(s - m_new).astype(BF16)
            pv = jnp.concatenate([_dot(p[:T], va), _dot(p[T:], vb)], axis=0)
            acc[j] = alpha * acc[j] + pv
            m[j] = m_new
    lane_o = lax.broadcasted_iota(jnp.int32, (T, LANES), 1)
    for j in range(2):
        a, b = acc[j][:T], acc[j][T:]
        o = jnp.where(lane_o < HEAD_DIM, a / pltpu.roll(a, HEAD_DIM, 1), b / pltpu.roll(b, HEAD_DIM, 1))
        o_ref[:, j * LANES:(j + 1) * LANES] = o.astype(BF16)


def _gattn(q, kt, v):
    B, S, _ = q.shape
    T = min(256, S)
    tk = min(1024, S)
    return pl.pallas_call(
        functools.partial(_gattn_kernel, tk=tk),
        grid=(B, S // T),
        in_specs=[pl.BlockSpec((None, T, 256), lambda b, i: (b, i, 0)),
                  pl.BlockSpec((LANES, S), lambda b, i: (0, b)),
                  pl.BlockSpec((None, S, LANES), lambda b, i: (b, 0, 0))],
        out_specs=pl.BlockSpec((None, T, 256), lambda b, i: (b, i, 0)),
        out_shape=jax.ShapeDtypeStruct((B, S, 256), BF16),
        compiler_params=_cparams("parallel", "parallel"),
        name="gattn",
    )(q, kt, v)


def _wattn_kernel(sink_ref, q_ref, k_ref, v_ref, o_ref, *, kw):
    T = q_ref.shape[0]
    S = k_ref.shape[0]
    i = pl.program_id(1)
    start = pl.multiple_of(jnp.clip(i * T - WINDOW, 0, S - kw), WINDOW)
    kwin = k_ref[pl.ds(start, kw), :]
    vwin = v_ref[pl.ds(start, kw), :]
    row = lax.broadcasted_iota(jnp.int32, (2 * T, kw), 0)
    col = lax.broadcasted_iota(jnp.int32, (2 * T, kw), 1)
    qpos = i * T + jnp.where(row < T, row, row - T)
    valid = jnp.abs(start + col - qpos) <= WINDOW
    first = lax.broadcasted_iota(jnp.int32, (2 * T, 1), 0) < T
    for j in range(2):
        qst = _stack_heads(q_ref[:, j * LANES:(j + 1) * LANES])
        s = jnp.where(valid, _dot_nt(qst, kwin), -jnp.inf)
        sk = jnp.where(first, sink_ref[2 * j], sink_ref[2 * j + 1])
        m = jnp.maximum(jnp.max(s, axis=1, keepdims=True), sk)
        e = jnp.exp(s - m)
        p = e / (jnp.sum(e, axis=1, keepdims=True) + jnp.exp(sk - m))
        o = _dot(p.astype(BF16), vwin)
        o_ref[:, j * LANES:(j + 1) * LANES] = _unstack_heads(o, T).astype(BF16)


def _wattn(q, k, v, sink):
    B, S, _ = q.shape
    T = min(256, S - 2 * WINDOW) if S > 2 * WINDOW else S
    kw = min(T + 2 * WINDOW, S)
    return pl.pallas_call(
        functools.partial(_wattn_kernel, kw=kw),
        grid=(B, S // T),
        in_specs=[pl.BlockSpec(memory_space=pltpu.SMEM),
                  pl.BlockSpec((None, T, 256), lambda b, i: (b, i, 0)),
                  pl.BlockSpec((None, S, LANES), lambda b, i: (b, 0, 0)),
                  pl.BlockSpec((None, S, LANES), lambda b, i: (b, 0, 0))],
        out_specs=pl.BlockSpec((None, T, 256), lambda b, i: (b, i, 0)),
        out_shape=jax.ShapeDtypeStruct((B, S, 256), BF16),
        compiler_params=_cparams("parallel", "parallel"),
        name="wattn",
    )(sink, q, k, v)


def _conv_kernel(zp_ref, zc_ref, zn_ref, w_ref, b_ref, g_ref, beta_ref, o_ref, buf_ref):
    T = zc_ref.shape[0]
    i = pl.program_id(1)
    H = CONV_HALO
    keep_prev = jnp.where(i > 0, 1.0, 0.0)
    keep_next = jnp.where(i < pl.num_programs(1) - 1, 1.0, 0.0)
    buf_ref[0:H, :] = zp_ref[...].astype(F32) * keep_prev
    buf_ref[H:H + T, :] = zc_ref[...].astype(F32)
    buf_ref[H + T:, :] = zn_ref[...].astype(F32) * keep_next
    acc = jnp.zeros((T, CONV_CH), F32)
    off = H - CONV_WIDTH // 2
    for tap in range(CONV_WIDTH):
        acc = acc + buf_ref[off + tap:off + tap + T, :] * w_ref[tap:tap + 1, :]
    z = acc + b_ref[...]
    mu = jnp.mean(z, axis=-1, keepdims=True)
    zc = z - mu
    var = jnp.mean(zc * zc, axis=-1, keepdims=True)
    y = zc * lax.rsqrt(var + NORM_EPS) * g_ref[...] + beta_ref[...]
    o_ref[...] = (y * jax.nn.sigmoid(y)).astype(BF16)


def _conv(z, lw):
    B, S, _ = z.shape
    T = min(512, S)
    H = CONV_HALO
    per = T // H
    last = S // H - 1
    full = lambda a: pl.BlockSpec(a.shape, lambda b, i: (0,) * a.ndim)
    consts = (lw['conv_w'], lw['conv_b'], lw['ln_g'], lw['ln_b'])
    return pl.pallas_call(
        _conv_kernel,
        grid=(B, S // T),
        in_specs=[pl.BlockSpec((None, H, CONV_CH), lambda b, i: (b, jnp.maximum(i * per - 1, 0), 0)),
                  pl.BlockSpec((None, T, CONV_CH), lambda b, i: (b, i, 0)),
                  pl.BlockSpec((None, H, CONV_CH), lambda b, i: (b, jnp.minimum((i + 1) * per, last), 0))]
                 + [full(a) for a in consts],
        out_specs=pl.BlockSpec((None, T, CONV_CH), lambda b, i: (b, i, 0)),
        out_shape=jax.ShapeDtypeStruct((B, S, CONV_CH), BF16),
        scratch_shapes=[pltpu.VMEM((T + 2 * H, CONV_CH), F32)],
        compiler_params=_cparams("parallel", "parallel"),
        name="conv",
    )(z, z, z, *consts)


def _gla_kernel(*refs, reverse, accumulate):
    if accumulate:
        d_ref, zl_ref, w2_ref, b2_ref, prev_ref, o_ref, st_ref = refs
    else:
        d_ref, zl_ref, w2_ref, b2_ref, o_ref, st_ref = refs
    TB = d_ref.shape[0]
    L = GLA_CHUNK
    nch = TB // L

    @pl.when(pl.program_id(1) == 0)
    def _():
        st_ref[...] = jnp.zeros(st_ref.shape, F32)

    pre = _dot(zl_ref[...], w2_ref[...]) + b2_ref[...]
    la = (jnp.minimum(pre, 0.0) - jnp.log(1.0 + jnp.exp(-jnp.abs(pre)))) * (1.0 / GLA_TAU)

    r = lax.broadcasted_iota(jnp.int32, (TB, TB), 0)
    c = lax.broadcasted_iota(jnp.int32, (TB, TB), 1)
    same = (r // L) == (c // L)
    base = (r // L) * L
    if reverse:
        tri = same & (c >= r)
        trimid = same & (c >= base + L // 2)
    else:
        tri = same & (c <= r)
        trimid = same & (c <= base + L // 2 - 1)
    sel = jnp.concatenate([jnp.where(m, 1.0, 0.0).astype(BF16) for m in (tri, trimid, same)], axis=0)
    hi = la.astype(BF16)
    r1 = la - hi.astype(F32)
    mid = r1.astype(BF16)
    lo = (r1 - mid.astype(F32)).astype(BF16)
    cums = _dot(sel, jnp.concatenate([hi, mid, lo], axis=1))
    cums = cums[:, :256] + cums[:, 256:512] + cums[:, 512:]
    b, bmid, blast = cums[:TB], cums[TB:2 * TB], cums[2 * TB:]

    q = d_ref[:, 0:256].astype(F32)
    k = d_ref[:, 256:512].astype(F32)
    v = d_ref[:, 512:768]
    qt = (q * jnp.exp(b - bmid)).astype(BF16)
    kt = (k * jnp.exp(bmid - b)).astype(BF16)
    qe = (q * jnp.exp(b)).astype(BF16)
    kl = (k * jnp.exp(blast - b)).astype(BF16)
    dec = jnp.exp(blast)

    r2 = lax.broadcasted_iota(jnp.int32, (2 * TB, TB), 0) % TB
    c2 = lax.broadcasted_iota(jnp.int32, (2 * TB, TB), 1)
    tri2 = ((r2 // L) == (c2 // L)) & ((c2 >= r2) if reverse else (c2 <= r2))
    rr =lax.broadcasted_iota(jnp.int32, (LANES, LANES), 0) // HEAD_DIM
    cc = lax.broadcasted_iota(jnp.int32, (LANES, LANES), 1) // HEAD_DIM
    head_diag = rr == cc
    rowid = lax.broadcasted_iota(jnp.int32, (TB, LANES), 0) // L
    order = range(nch - 1, -1, -1) if reverse else range(nch)

    for p in range(2):
        ls = slice(p * LANES, (p + 1) * LANES)
        vb = v[:, ls]
        att = jnp.where(tri2, _dot_nt(_stack_heads(qt[:, ls]), kt[:, ls]), 0.0)
        o_intra = _unstack_heads(_dot(att.astype(BF16), vb), TB)
        vt = vb.astype(F32).T.astype(BF16)
        klb = kl[:, ls]
        st = st_ref[p]
        o_inter = [None] * nch
        for ci in order:
            rows = slice(ci * L, (ci + 1) * L)
            o_inter[ci] = _dot_nt(qe[rows, ls], st.astype(BF16))
            kv_t = _dot(vt, jnp.where(rowid == ci, klb, jnp.zeros_like(klb)))
            st = st * dec[ci * L:ci * L + 1, ls] + jnp.where(head_diag, kv_t, 0.0)
        st_ref[p] = st
        o = o_intra + jnp.concatenate(o_inter, axis=0)
        if accumulate:
            o = o + prev_ref[:, ls]
        o_ref[:, ls] = o


def _gla_dir(d, zl, w2, b2, prev, reverse):
    B, S, _ = d.shape
    TB = min(GLA_BLOCK, S)
    nb = S // TB
    blk = (lambda b, i: (b, nb - 1 - i, 0)) if reverse else (lambda b, i: (b, i, 0))
    full = lambda a: pl.BlockSpec(a.shape, lambda b, i: (0,) * a.ndim)
    in_specs = [pl.BlockSpec((None, TB, 1024), blk), pl.BlockSpec((None, TB, LANES), blk), full(w2), full(b2)]
    args = [d, zl, w2, b2]
    if prev is not None:
        in_specs.append(pl.BlockSpec((None, TB, 256), blk))
        args.append(prev)
    return pl.pallas_call(
        functools.partial(_gla_kernel, reverse=reverse, accumulate=prev is not None),
        grid=(B, nb),
        in_specs=in_specs,
        out_specs=pl.BlockSpec((None, TB, 256), blk),
        out_shape=jax.ShapeDtypeStruct((B, S, 256), F32),
        scratch_shapes=[pltpu.VMEM((2, LANES, LANES), F32)],
        compiler_params=_cparams("parallel", "arbitrary"),
        name="gla_bwd" if reverse else "gla_fwd",
    )(*args)


def _merge_kernel(x_ref, oa_ref, ob_ref, oc_ref, og_ref, rd_ref, gmix_ref, wg_ref, wbr_ref, wout_ref,
                  gn_ref, gffn_ref, wrh_ref, wrl_ref, x1_ref, h2_ref, aff_ref):
    x = x_ref[...]
    h = _rms(x, gmix_ref[...]).astype(BF16)
    og = og_ref[...]
    od = og * lax.rsqrt(_group_meansq(og, HEAD_DIM) + NORM_EPS) * gn_ref[...]
    rd = rd_ref[...].astype(F32)
    od = (od * (rd * jax.nn.sigmoid(rd))).astype(BF16)
    branches = (oa_ref[...], ob_ref[...], oc_ref[...], od)
    merged = jnp.zeros(x.shape, F32)
    for i, o in enumerate(branches):
        gate = jax.nn.sigmoid(_dot(h, wg_ref[:, i * D_MODEL:(i + 1) * D_MODEL]))
        merged = merged + gate * _dot(o, wbr_ref[i])
    x1 = x + _dot(merged.astype(BF16), wout_ref[...])
    x1_ref[...] = x1
    h2 = _rms(x1, gffn_ref[...])
    h2_ref[...] = h2.astype(BF16)
    hh, hl = _split2(h2)
    logits = _dot(hh, wrh_ref[...]) + _dot(hh, wrl_ref[...]) + _dot(hl, wrh_ref[...])
    lt = logits.T[:N_EXPERTS]
    e = jnp.exp(lt - jnp.max(lt, axis=0, keepdims=True))
    aff_ref[...] = e / jnp.sum(e, axis=0, keepdims=True)


def _merge(x, oa, ob, oc, og, d, lw):
    N = x.shape[0]
    T = min(256, N)
    tok = lambda w: pl.BlockSpec((T, w), lambda i: (i, 0))
    full = lambda a: pl.BlockSpec(a.shape, lambda i: (0,) * a.ndim)
    consts = (lw['g_mix'], lw['wg'], lw['wbr'], lw['wout'], lw['gn'], lw['g_ffn'], lw['wr_hi'], lw['wr_lo'])
    return pl.pallas_call(
        _merge_kernel,
        grid=(N // T,),
        in_specs=[tok(D_MODEL), tok(256), tok(256), tok(256), tok(256),
                  pl.BlockSpec((T, 256), lambda i: (i, 3))] + [full(a) for a in consts],
        out_specs=[tok(D_MODEL), tok(D_MODEL), pl.BlockSpec((N_EXPERTS, T), lambda i: (0, i))],
        out_shape=[jax.ShapeDtypeStruct((N, D_MODEL), F32), jax.ShapeDtypeStruct((N, D_MODEL), BF16),
                   jax.ShapeDtypeStruct((N_EXPERTS, N), F32)],
        compiler_params=_cparams("parallel"),
        name="merge",
    )(x, oa, ob, oc, og, d, *consts)


def _ffn_kernel(x_ref, wg_ref, wu_ref, wd_ref, gate_ref, y_ref):
    x = x_ref[...]
    g = _dot(x, wg_ref[...])
    hid = (g * jax.nn.sigmoid(g)) * _dot(x, wu_ref[...])
    y_ref[...] = _dot(hid.astype(BF16), wd_ref[...]) * gate_ref[...]


def _ffn(xe, gate, lw):
    E, C, _ = xe.shape
    M = min(512, C)
    wspec = pl.BlockSpec((None, D_MODEL, D_MODEL), lambda e, s: (e, 0, 0))
    return pl.pallas_call(
        _ffn_kernel,
        grid=(E, C // M),
        in_specs=[pl.BlockSpec((None, M, D_MODEL), lambda e, s: (e, s, 0)), wspec, wspec, wspec,
                  pl.BlockSpec((None, M, 1), lambda e, s: (e, s, 0))],
        out_specs=pl.BlockSpec((None, M, D_MODEL), lambda e, s: (e, s, 0)),
        out_shape=jax.ShapeDtypeStruct((E, C, D_MODEL), F32),
        compiler_params=_cparams("parallel", "parallel"),
        name="ffn",
    )(xe, lw['we_g'], lw['we_u'], lw['we_d'], gate)


def _ple_kernel(x_ref, p_ref, g_ref, wgate_ref, wproj_ref, gfin_ref, o_ref, *, final):
    x = x_ref[...]
    h = _rms(x, g_ref[...]).astype(BF16)
    gate = jax.nn.sigmoid(_dot(h, wgate_ref[...]))
    y = x + gate * _dot(p_ref[...].astype(BF16), wproj_ref[...])
    if final:
        y = _rms(y, gfin_ref[...])
    o_ref[...] = y


def _ple(x, p, lw, gfin, final):
    N = x.shape[0]
    T = min(512, N)
    tok = lambda w: pl.BlockSpec((T, w), lambda i: (i, 0))
    full = lambda a: pl.BlockSpec(a.shape, lambda i: (0,) * a.ndim)
    consts = (lw['g_ple'], lw['w_pg'], lw['w_pp'], gfin)
    return pl.pallas_call(
        functools.partial(_ple_kernel, final=final),
        grid=(N // T,),
        in_specs=[tok(D_MODEL), tok(PLE_DIM)] + [full(a) for a in consts],
        out_specs=tok(D_MODEL),
        out_shape=jax.ShapeDtypeStruct((N, D_MODEL), F32),
        compiler_params=_cparams("parallel"),
        name="ple",
    )(x, p, *consts)


def _rope_tables(S):
    lane = np.arange(LANES)
    d = lane % HEAD_DIM
    t = jnp.arange(S)
    inv_a = ROPE_THETA ** (-jnp.arange(0, HEAD_DIM // 2, 2, dtype=F32) / (HEAD_DIM // 2))
    pos_a = jnp.where((d // 32 == 0)[None, :], (t // GRID_W)[:, None], (t % GRID_W)[:, None]).astype(F32)
    ang_a = pos_a * inv_a[d % 16][None, :]
    sign_a = jnp.where(d % 32 < 16, -1.0, 1.0)[None, :]
    inv_c = ROPE_THETA ** (-jnp.arange(0, HEAD_DIM, 2, dtype=F32) / HEAD_DIM)
    ang_c = t.astype(F32)[:, None] * inv_c[d % 32][None, :]
    sign_c = jnp.where(d < 32, -1.0, 1.0)[None, :]
    return (jnp.cos(ang_a), jnp.sin(ang_a) * sign_a, jnp.cos(ang_c), jnp.sin(ang_c) * sign_c)


def _layer_weights(i, w):
    w_in = w['w_in'][i]
    cols = lambda r: w_in[:, r[0]:r[1]]
    bf = lambda a: a.astype(BF16)
    row = lambda a: a.reshape(1, -1).astype(F32)
    wz = jnp.pad(cols(_ZL), ((0, 0), (0, LANES - (_ZL[1] - _ZL[0]))))
    w2 = w['gla_w2'][i]
    w2p = jnp.zeros((2, LANES, 256), F32).at[0, 0:16].set(w2[0]).at[1, 16:32].set(w2[1])
    wr = jnp.pad(w['w_router'][i], ((0, 0), (0, LANES - N_EXPERTS)))
    wr_hi = wr.astype(BF16)
    wbr = w['w_branch'][i]
    wbr = jnp.stack([wbr[0][_QPERM], wbr[1], wbr[2][_QPERM], wbr[3]])
    return dict(
        g_mix=row(w['norm_mix'][i]),
        wa=bf(jnp.concatenate([cols(_QA)[:, _QPERM], cols(_KA), cols(_VA)], axis=1)),
        wb=bf(cols(_UB)),
        wc=bf(jnp.concatenate([cols(_QC)[:, _QPERM], cols(_KC), cols(_VC)], axis=1)),
        wd=bf(cols(_DD)),
        wz=bf(wz),
        gq=row(jnp.tile(w['qk_norm'][i, 0], 4)),
        gk=row(jnp.tile(w['qk_norm'][i, 1], 2)),
        sink=w['sink_logit'][i][_HEAD_PERM].astype(F32),
        conv_w=w['conv_dw'][i].astype(F32),
        conv_b=row(w['conv_dw_b'][i]), ln_g=row(w['conv_ln_g'][i]), ln_b=row(w['conv_ln_b'][i]),
        w2=bf(w2p), b2=w['gla_b2'][i].reshape(2, 1, 256).astype(F32),
        gn=row(jnp.tile(w['gla_norm'][i], 4)),
        wg=bf(w_in[:, _GATES:]), wbr=bf(wbr), wout=bf(w['w_out'][i]),
        g_ffn=row(w['norm_ffn'][i]),
        wr_hi=wr_hi, wr_lo=(wr - wr_hi.astype(F32)).astype(BF16),
        we_g=bf(w['w_gate_e'][i]), we_u=bf(w['w_up_e'][i]), we_d=bf(w['w_down_e'][i]),
        g_ple=row(w['norm_ple'][i]), w_pg=bf(w['w_ple_gate'][i]), w_pp=bf(w['w_ple_proj'][i]),
    )


def _trunk(x3, p4, layers, gfin):
    B, S, _ = x3.shape
    N = B * S
    tabs = _rope_tables(S)
    x = x3.reshape(N, D_MODEL)
    cap = max(1, EC_CAPACITY * N // N_EXPERTS)
    for i, lw in enumerate(layers):
        qa, kat, va, zb, qc, kc, vc, d, zl = _proj(x, lw, tabs, S)
        b3 = lambda a: a.reshape(B, S, a.shape[-1])
        oa = _gattn(b3(qa), kat, b3(va))
        ob = _conv(b3(zb), lw)
        oc = _wattn(b3(qc), b3(kc), b3(vc), lw['sink'])
        og = _gla_dir(b3(d), b3(zl), lw['w2'][0], lw['b2'][0], None, False)
        og = _gla_dir(b3(d), b3(zl), lw['w2'][1], lw['b2'][1], og, True)
        flat = lambda a: a.reshape(N, a.shape[-1])
        x1, h2, aff_t = _merge(x, flat(oa), flat(ob), flat(oc), flat(og), d, lw)
        gate, idx = lax.top_k(aff_t, cap)
        ye = _ffn(h2[idx], gate[..., None], lw)
        x2 = x1.at[idx.reshape(-1)].add(ye.reshape(-1, D_MODEL))
        x = _ple(x2, p4[i].reshape(N, PLE_DIM), lw, gfin, i == len(layers) - 1)
    return x.reshape(B, S, D_MODEL)


def kernel(x_prompt, x_sample, p_prompt, p_sample, norm_mix, w_in, qk_norm, sink_logit, conv_dw, conv_dw_b,
           conv_ln_g, conv_ln_b, gla_w2, gla_b2, gla_norm, w_branch, w_out, norm_ffn, w_router, w_gate_e,
           w_up_e, w_down_e, norm_ple, w_ple_gate, w_ple_proj, norm_final):
    w = dict(norm_mix=norm_mix, w_in=w_in, qk_norm=qk_norm, sink_logit=sink_logit, conv_dw=conv_dw,
             conv_dw_b=conv_dw_b, conv_ln_g=conv_ln_g, conv_ln_b=conv_ln_b, gla_w2=gla_w2, gla_b2=gla_b2,
             gla_norm=gla_norm, w_branch=w_branch, w_out=w_out, norm_ffn=norm_ffn, w_router=w_router,
             w_gate_e=w_gate_e, w_up_e=w_up_e, w_down_e=w_down_e, norm_ple=norm_ple,
             w_ple_gate=w_ple_gate, w_ple_proj=w_ple_proj)
    layers = [_layer_weights(i, w) for i in range(norm_mix.shape[0])]
    gfin = norm_final.reshape(1, -1).astype(F32)
    return (_trunk(x_prompt, p_prompt, layers, gfin), _trunk(x_sample, p_sample, layers, gfin))
```

```python
import dataclasses
import functools

import jax
import jax.numpy as jnp
import numpy as np
from jax import lax
from jax.experimental import pallas as pl
from jax.experimental.pallas import tpu as pltpu
from jax.experimental.pallas import tpu_sc as plsc

F32 = jnp.float32
BF16 = jnp.bfloat16

D_MODEL = 1024
DEPTH = 4
GRID_W = 64
HEAD_DIM = 64
ROPE_THETA = 10000.0
NORM_EPS = 1e-6
CONV_CH = 256
CONV_WIDTH = 31
CONV_HALO = 16
WINDOW = 128
GLA_TAU = 16.0
GLA_CHUNK = 64
GLA_BLOCK = 256
N_EXPERTS = 16
EC_CAPACITY = 2
PLE_DIM = 256
LANES = 128
SUBLANES = 8
VMEM_LIMIT = 56 * 1024 * 1024
LOG2E = 1.4426950408889634
SEL_CHUNK = 512
COMB_WIN = 64
SC_LANES = 16
GATHER_WINDOW = 128

_QA, _KA, _VA = (0, 256), (256, 384), (384, 512)
_UB = (512, 1024)
_QC, _KC, _VC = (1024, 1280), (1280, 1408), (1408, 1536)
_DD = (1536, 2560)
_ZL = (2560, 2592)
_GATES = 2592
_QPERM = np.concatenate([np.arange(0, 64), np.arange(128, 192), np.arange(64, 128), np.arange(192, 256)])
_HEAD_PERM = np.array([0, 2, 1, 3])


def _cparams(*sem):
    return pltpu.CompilerParams(dimension_semantics=sem, vmem_limit_bytes=VMEM_LIMIT)


def _dot(a, b):
    return jnp.dot(a, b, preferred_element_type=F32)


def _dot_nt(a, b):
    return lax.dot_general(a, b, (((1,), (1,)), ((), ())), preferred_element_type=F32)


def _rms(x, g):
    return x * lax.rsqrt(jnp.mean(x * x, axis=-1, keepdims=True) + NORM_EPS) * g


def _split2(x):
    hi = x.astype(BF16)
    lo = (x - hi.astype(F32)).astype(BF16)
    return hi, lo


def _group_ones(width, group):
    r = lax.broadcasted_iota(jnp.int32, (width, width), 0) // group
    c = lax.broadcasted_iota(jnp.int32, (width, width), 1) // group
    return jnp.where(r == c, 1.0, 0.0).astype(BF16)


def _group_meansq(x, group):
    hi, lo = _split2(x * x)
    ones = _group_ones(x.shape[1], group)
    return (_dot(hi, ones) + _dot(lo, ones)) * (1.0 / group)


def _rope(x, cos, sin_signed, half):
    width = x.shape[1]
    lane = lax.broadcasted_iota(jnp.int32, x.shape, 1)
    from_lo = pltpu.roll(x, half, 1)
    from_hi = pltpu.roll(x, width - half, 1)
    partner = jnp.where((lane & half) != 0, from_lo, from_hi)
    return x * cos + partner * sin_signed


def _proj_kernel(x_ref, g_ref, wa_ref, wb_ref, wc_ref, wd_ref, wz_ref, gq_ref, gk_ref,
                 ca_ref, sa_ref, cc_ref, sc_ref,
                 qa_ref, kat_ref, va_ref, zb_ref, qc_ref, kc_ref, vc_ref, d_ref, zl_ref):
    h = _rms(x_ref[...], g_ref[...]).astype(BF16)
    scale = HEAD_DIM ** -0.5

    ua = _dot(h, wa_ref[...])
    ca, sa = ca_ref[...], sa_ref[...]
    q = ua[:, :256]
    q = q * lax.rsqrt(_group_meansq(q, HEAD_DIM) + NORM_EPS) * gq_ref[...]
    q = _rope(q, jnp.concatenate([ca, ca], axis=1), jnp.concatenate([sa, sa], axis=1), 16)
    qa_ref[...] = (q * (scale * LOG2E)).astype(BF16)
    k = ua[:, 256:384]
    k = k * lax.rsqrt(_group_meansq(k, HEAD_DIM) + NORM_EPS) * gk_ref[...]
    kat_ref[...] = _rope(k, ca, sa, 16).T.astype(BF16)
    va_ref[...] = ua[:, 384:].astype(BF16)

    ub = _dot(h, wb_ref[...])
    zb_ref[...] = (ub[:, :CONV_CH] * jax.nn.sigmoid(ub[:, CONV_CH:])).astype(BF16)

    uc = _dot(h, wc_ref[...])
    cc, sc = cc_ref[...], sc_ref[...]
    qc = _rope(uc[:, :256], jnp.concatenate([cc, cc], axis=1), jnp.concatenate([sc, sc], axis=1), 32)
    qc_ref[...] = (qc * scale).astype(BF16)
    kc_ref[...] = _rope(uc[:, 256:384], cc, sc, 32).astype(BF16)
    vc_ref[...] = uc[:, 384:].astype(BF16)

    ud = _dot(h, wd_ref[...])
    d_ref[:, :256] = (ud[:, :256] * scale).astype(BF16)
    d_ref[:, 256:] = ud[:, 256:].astype(BF16)
    zl_ref[...] = _dot(h, wz_ref[...]).astype(BF16)


def _proj(x, lw, tabs, S):
    N = x.shape[0]
    T = min(512, S)
    per_row = S // T
    tok = lambda w: pl.BlockSpec((T, w), lambda i: (i, 0))
    full = lambda a: pl.BlockSpec(a.shape, lambda i: (0,) * a.ndim)
    tab = pl.BlockSpec((T, LANES), lambda i: (i % per_row, 0))
    widths = (256, 128, 128, 256, 256, 128, 128, 1024, 128)
    consts = (lw['g_mix'], lw['wa'], lw['wb'], lw['wc'], lw['wd'], lw['wz'], lw['gq'], lw['gk'])
    return pl.pallas_call(
        _proj_kernel,
        grid=(N // T,),
        in_specs=[tok(D_MODEL)] + [full(a) for a in consts] + [tab] * 4,
        out_specs=[tok(256), pl.BlockSpec((LANES, T), lambda i: (0, i))] + [tok(w) for w in widths[2:]],
        out_shape=[jax.ShapeDtypeStruct((N, 256), BF16), jax.ShapeDtypeStruct((LANES, N), BF16)]
                  + [jax.ShapeDtypeStruct((N, w), BF16) for w in widths[2:]],
        compiler_params=_cparams("parallel"),
        name="proj",
    )(x, *consts, *tabs)


def _stack_heads(qb):
    lane = lax.broadcasted_iota(jnp.int32, qb.shape, 1)
    zero = jnp.zeros_like(qb)
    return jnp.concatenate([jnp.where(lane < HEAD_DIM, qb, zero), jnp.where(lane < HEAD_DIM, zero, qb)], axis=0)


def _unstack_heads(o, T):
    lane = lax.broadcasted_iota(jnp.int32, (T, LANES), 1)
    return jnp.where(lane < HEAD_DIM, o[:T], o[T:])


def _gattn_kernel(q_ref, kt_ref, v_ref, o_ref, *, tk):
    T = q_ref.shape[0]
    S = kt_ref.shape[1]
    lane_v = lax.broadcasted_iota(jnp.int32, (tk, LANES), 1)
    one = jnp.ones((tk, LANES), BF16)
    qs = [_stack_heads(q_ref[:, j * LANES:(j + 1) * LANES]) for j in range(2)]
    m = [jnp.full((2 * T, 1), -jnp.inf, F32) for _ in range(2)]
    acc = [jnp.zeros((2 * T, LANES), F32) for _ in range(2)]
    for c in range(S // tk):
        kt = kt_ref[:, c * tk:(c + 1) * tk]
        v = v_ref[c * tk:(c + 1) * tk, :]
        va = jnp.where(lane_v < HEAD_DIM, v, one)
        vb = jnp.where(lane_v < HEAD_DIM, one, v)
        for j in range(2):
            s = _dot(qs[j], kt)
            m_new = jnp.maximum(m[j], jnp.max(s, axis=1, keepdims=True))
            alpha = jnp.exp2(m[j] - m_new)
            p = jnp.exp2(s - m_new).astype(BF16)
            pv = jnp.concatenate([_dot(p[:T], va), _dot(p[T:], vb)], axis=0)
            acc[j] = alpha * acc[j] + pv
            m[j] = m_new
    lane_o = lax.broadcasted_iota(jnp.int32, (T, LANES), 1)
    for j in range(2):
        a, b = acc[j][:T], acc[j][T:]
        o = jnp.where(lane_o < HEAD_DIM, a / pltpu.roll(a, HEAD_DIM, 1), b / pltpu.roll(b, HEAD_DIM, 1))
        o_ref[:, j * LANES:(j + 1) * LANES] = o.astype(BF16)


def _gattn(q, kt, v):
    B, S, _ = q.shape
    T = min(256, S)
    tk = min(1024, S)
    return pl.pallas_call(
        functools.partial(_gattn_kernel, tk=tk),
        grid=(B, S // T),
        in_specs=[pl.BlockSpec((None, T, 256), lambda b, i: (b, i, 0)),
                  pl.BlockSpec((LANES, S), lambda b, i: (0, b)),
                  pl.BlockSpec((None, S, LANES), lambda b, i: (b, 0, 0))],
        out_specs=pl.BlockSpec((None, T, 256), lambda b, i: (b, i, 0)),
        out_shape=jax.ShapeDtypeStruct((B, S, 256), BF16),
        compiler_params=_cparams("parallel", "parallel"),
        name="gattn",
    )(q, kt, v)


def _wattn_kernel(sink_ref, q_ref, k_ref, v_ref, o_ref, *, kw):
    T = q_ref.shape[0]
    S = k_ref.shape[0]
    i = pl.program_id(1)
    start = pl.multiple_of(jnp.clip(i * T - WINDOW, 0, S - kw), WINDOW)
    kwin = k_ref[pl.ds(start, kw), :]
    vwin = v_ref[pl.ds(start, kw), :]
    row = lax.broadcasted_iota(jnp.int32, (2 * T, kw), 0)
    col = lax.broadcasted_iota(jnp.int32, (2 * T, kw), 1)
    qpos = i * T + jnp.where(row < T, row, row - T)
    valid = jnp.abs(start + col - qpos) <= WINDOW
    first = lax.broadcasted_iota(jnp.int32, (2 * T, 1), 0) < T
    for j in range(2):
        qst = _stack_heads(q_ref[:, j * LANES:(j + 1) * LANES])
        s = jnp.where(valid, _dot_nt(qst, kwin), -jnp.inf)
        sk = jnp.where(first, sink_ref[2 * j], sink_ref[2 * j + 1])
        m = jnp.maximum(jnp.max(s, axis=1, keepdims=True), sk)
        e = jnp.exp(s - m)
        p = e / (jnp.sum(e, axis=1, keepdims=True) + jnp.exp(sk - m))
        o = _dot(p.astype(BF16), vwin)
        o_ref[:, j * LANES:(j + 1) * LANES] = _unstack_heads(o, T).astype(BF16)


def _wattn(q, k, v, sink):
    B, S, _ = q.shape
    T = min(256, S - 2 * WINDOW) if S > 2 * WINDOW else S
    kw = min(T + 2 * WINDOW, S)
    return pl.pallas_call(
        functools.partial(_wattn_kernel, kw=kw),
        grid=(B, S // T),
        in_specs=[pl.BlockSpec(memory_space=pltpu.SMEM),
                  pl.BlockSpec((None, T, 256), lambda b, i: (b, i, 0)),
                  pl.BlockSpec((None, S, LANES), lambda b, i: (b, 0, 0)),
                  pl.BlockSpec((None, S, LANES), lambda b, i: (b, 0, 0))],
        out_specs=pl.BlockSpec((None, T, 256), lambda b, i: (b, i, 0)),
        out_shape=jax.ShapeDtypeStruct((B, S, 256), BF16),
        compiler_params=_cparams("parallel", "parallel"),
        name="wattn",
    )(sink, q, k, v)


def _conv_kernel(zp_ref, zc_ref, zn_ref, w_ref, b_ref, g_ref, beta_ref, o_ref, buf_ref, sh_ref):
    T = zc_ref.shape[0]
    i = pl.program_id(1)
    H = CONV_HALO
    keep_prev = jnp.where(i > 0, 1.0, 0.0)
    keep_next = jnp.where(i < pl.num_programs(1) - 1, 1.0, 0.0)
    buf_ref[0:H, :] = zp_ref[...].astype(F32) * keep_prev
    buf_ref[H:H + T, :] = zc_ref[...].astype(F32)
    buf_ref[H + T:, :] = zn_ref[...].astype(F32) * keep_next
    span = sh_ref.shape[1]
    for r in range(1, SUBLANES):
        sh_ref[r] = buf_ref[r:r + span, :]
    acc = jnp.zeros((T, CONV_CH), F32)
    off = H - CONV_WIDTH // 2
    for tap in range(CONV_WIDTH):
        r = (off + tap) % SUBLANES
        a = off + tap - r
        rows = buf_ref[a:a + T, :] if r == 0 else sh_ref[r, a:a + T, :]
        acc = acc + rows * w_ref[tap:tap + 1, :]
    z = acc + b_ref[...]
    mu = jnp.mean(z, axis=-1, keepdims=True)
    zc = z - mu
    var = jnp.mean(zc * zc, axis=-1, keepdims=True)
    y = zc * lax.rsqrt(var + NORM_EPS) * g_ref[...] + beta_ref[...]
    o_ref[...] = (y * jax.nn.sigmoid(y)).astype(BF16)


def _conv(z, lw):
    B, S, _ = z.shape
    T = min(512, S)
    H = CONV_HALO
    per = T // H
    last = S // H - 1
    full = lambda a: pl.BlockSpec(a.shape, lambda b, i: (0,) * a.ndim)
    consts = (lw['conv_w'], lw['conv_b'], lw['ln_g'], lw['ln_b'])
    return pl.pallas_call(
        _conv_kernel,
        grid=(B, S // T),
        in_specs=[pl.BlockSpec((None, H, CONV_CH), lambda b, i: (b, jnp.maximum(i * per - 1, 0), 0)),
                  pl.BlockSpec((None, T, CONV_CH), lambda b, i: (b, i, 0)),
                  pl.BlockSpec((None, H, CONV_CH), lambda b, i: (b, jnp.minimum((i + 1) * per, last), 0))]
                 + [full(a) for a in consts],
        out_specs=pl.BlockSpec((None, T, CONV_CH), lambda b, i: (b, i, 0)),
        out_shape=jax.ShapeDtypeStruct((B, S, CONV_CH), BF16),
        scratch_shapes=[pltpu.VMEM((T + 2 * H, CONV_CH), F32),
                        pltpu.VMEM((SUBLANES, T + 2 * H - SUBLANES, CONV_CH), F32)],
        compiler_params=_cparams("parallel", "parallel"),
        name="conv",
    )(z, z, z, *consts)


def _gla_block(d_ref, zl_ref, w2, b2, tri, att_mask, st_ref, reverse):
    TB = d_ref.shape[0]
    L = GLA_CHUNK
    nch = TB // L

    pre = _dot(zl_ref[...], w2) + b2
    la = (jnp.minimum(pre, 0.0) - jnp.log(1.0 + jnp.exp(-jnp.abs(pre)))) * (1.0 / GLA_TAU)

    hi = la.astype(BF16)
    r1 = la - hi.astype(F32)
    mid = r1.astype(BF16)
    lo = (r1 - mid.astype(F32)).astype(BF16)
    cums = _dot(tri, jnp.concatenate([hi, mid, lo], axis=1))
    b = cums[:, :256] + cums[:, 256:512] + cums[:, 512:]
    mid_row = L // 2 if reverse else L // 2 - 1
    last_row = 0 if reverse else L - 1
    per_chunk = lambda row: jnp.concatenate(
        [jnp.broadcast_to(b[ci * L + row:ci * L + row + 1], (L, 256)) for ci in range(nch)], axis=0)
    bmid, blast = per_chunk(mid_row), per_chunk(last_row)

    q = d_ref[:, 0:256].astype(F32)
    k = d_ref[:, 256:512].astype(F32)
    v = d_ref[:, 512:768]
    qt = (q * jnp.exp(b - bmid)).astype(BF16)
    kt = (k * jnp.exp(bmid - b)).astype(BF16)
    qe = (q * jnp.exp(b)).astype(BF16)
    kl = (k * jnp.exp(blast - b)).astype(BF16)
    dec = jnp.exp(blast)

    rr = lax.broadcasted_iota(jnp.int32, (LANES, LANES), 0) // HEAD_DIM
    cc = lax.broadcasted_iota(jnp.int32, (LANES, LANES), 1) // HEAD_DIM
    head_diag = rr == cc
    rowid = lax.broadcasted_iota(jnp.int32, (TB, LANES), 0) // L
    order = range(nch - 1, -1, -1) if reverse else range(nch)

    out = []
    for p in range(2):
        ls = slice(p * LANES, (p + 1) * LANES)
        vb = v[:, ls]
        att = _dot_nt(_stack_heads(qt[:, ls]), kt[:, ls]) * att_mask
        o_intra = _unstack_heads(_dot(att.astype(BF16), vb), TB)
        vt = vb.astype(F32).T.astype(BF16)
        klb = kl[:, ls]
        kv_t = [jnp.where(head_diag, _dot(vt, jnp.where(rowid == ci, klb, jnp.zeros_like(klb))), 0.0)
                for ci in range(nch)]
        st = st_ref[p]
        o_inter = [None] * nch
        for ci in order:
            o_inter[ci] = _dot_nt(qe[ci * L:(ci + 1) * L, ls], st.astype(BF16))
            st = st * dec[ci * L:ci * L + 1, ls] + kv_t[ci]
        st_ref[p] = st
        out.append(o_intra + jnp.concatenate(o_inter, axis=0))
    return out


def _gla_kernel(df_ref, zf_ref, db_ref, zb_ref, w2_ref, b2_ref, tri_ref, mask_ref, of_ref, ob_ref, st_ref):
    @pl.when(pl.program_id(1) == 0)
    def _():
        st_ref[...] = jnp.zeros(st_ref.shape, F32)

    fwd = _gla_block(df_ref, zf_ref, w2_ref[0], b2_ref[0], tri_ref[0], mask_ref[0], st_ref.at[0], False)
    bwd = _gla_block(db_ref, zb_ref, w2_ref[1], b2_ref[1], tri_ref[1], mask_ref[1], st_ref.at[1], True)
    for p in range(2):
        of_ref[:, p * LANES:(p + 1) * LANES] = fwd[p]
        ob_ref[:, p * LANES:(p + 1) * LANES] = bwd[p]


def _gla_masks(TB):
    L = GLA_CHUNK
    r = np.arange(TB)[:, None]
    c = np.arange(TB)[None, :]
    same = (r // L) == (c // L)
    tri = np.stack([same & (c <= r), same & (c >= r)]).astype(np.float32)
    return jnp.asarray(tri, BF16), jnp.asarray(np.concatenate([tri, tri], axis=1), F32)


def _gla(d, zl, w2, b2):
    B, S, _ = d.shape
    TB = min(GLA_BLOCK, S)
    nb = S // TB
    fblk = lambda b, i: (b, i, 0)
    bblk = lambda b, i: (b, nb - 1 - i, 0)
    full = lambda a: pl.BlockSpec(a.shape, lambda b, i: (0,) * a.ndim)
    tri, mask = _gla_masks(TB)
    out = jax.ShapeDtypeStruct((B, S, 256), F32)
    return pl.pallas_call(
        _gla_kernel,
        grid=(B, nb),
        in_specs=[pl.BlockSpec((None, TB, 1024), fblk), pl.BlockSpec((None, TB, LANES), fblk),
                  pl.BlockSpec((None, TB, 1024), bblk), pl.BlockSpec((None, TB, LANES), bblk),
                  full(w2), full(b2), full(tri), full(mask)],
        out_specs=[pl.BlockSpec((None, TB, 256), fblk), pl.BlockSpec((None, TB, 256), bblk)],
        out_shape=[out, out],
        scratch_shapes=[pltpu.VMEM((2, 2, LANES, LANES), F32)],
        compiler_params=_cparams("parallel", "arbitrary"),
        name="gla",
    )(d, zl, d, zl, w2, b2, tri, mask)


def _merge_kernel(x_ref, oa_ref, ob_ref, oc_ref, ogf_ref, ogb_ref, rd_ref, gmix_ref, wg_ref, wbr_ref, wout_ref,
                  gn_ref, gffn_ref, wrh_ref, wrl_ref, x1_ref, h2_ref, aff_ref):
    x = x_ref[...]
    h = _rms(x, gmix_ref[...]).astype(BF16)
    og = ogf_ref[...] + ogb_ref[...]
    od = og * lax.rsqrt(_group_meansq(og, HEAD_DIM) + NORM_EPS) * gn_ref[...]
    rd = rd_ref[...].astype(F32)
    od = (od * (rd * jax.nn.sigmoid(rd))).astype(BF16)
    branches = (oa_ref[...], ob_ref[...], oc_ref[...], od)
    merged = jnp.zeros(x.shape, F32)
    for i, o in enumerate(branches):
        gate = jax.nn.sigmoid(_dot(h, wg_ref[:, i * D_MODEL:(i + 1) * D_MODEL]))
        merged = merged + gate * _dot(o, wbr_ref[i])
    x1 = x + _dot(merged.astype(BF16), wout_ref[...])
    x1_ref[...] = x1
    h2 = _rms(x1, gffn_ref[...])
    for k in range(D_MODEL // LANES):
        h2_ref[:, k, :] = h2[:, k * LANES:(k + 1) * LANES]
    hh, hl = _split2(h2)
    logits = _dot(hh, wrh_ref[...]) + _dot(hh, wrl_ref[...]) + _dot(hl, wrh_ref[...])
    lt = logits.T[:N_EXPERTS]
    e = jnp.exp(lt - jnp.max(lt, axis=0, keepdims=True))
    aff_ref[...] = e / jnp.sum(e, axis=0, keepdims=True)


def _merge(x, oa, ob, oc, ogf, ogb, d, lw):
    N = x.shape[0]
    T = min(256, N)
    sub = D_MODEL // LANES
    tok = lambda w: pl.BlockSpec((T, w), lambda i: (i, 0))
    full = lambda a: pl.BlockSpec(a.shape, lambda i: (0,) * a.ndim)
    consts = (lw['g_mix'], lw['wg'], lw['wbr'], lw['wout'], lw['gn'], lw['g_ffn'], lw['wr_hi'], lw['wr_lo'])
    return pl.pallas_call(
        _merge_kernel,
        grid=(N // T,),
        in_specs=[tok(D_MODEL), tok(256), tok(256), tok(256), tok(256), tok(256),
                  pl.BlockSpec((T, 256), lambda i: (i, 3))] + [full(a) for a in consts],
        out_specs=[tok(D_MODEL), pl.BlockSpec((T, sub, LANES), lambda i: (i, 0, 0)),
                   pl.BlockSpec((N_EXPERTS, T), lambda i: (0, i))],
        out_shape=[jax.ShapeDtypeStruct((N, D_MODEL), F32), jax.ShapeDtypeStruct((N, sub, LANES), F32),
                   jax.ShapeDtypeStruct((N_EXPERTS, N), F32)],
        compiler_params=_cparams("parallel"),
        name="merge",
    )(x, oa, ob, oc, ogf, ogb, d, *consts)


def _select_kernel(aff_ref, rank_ref, incl_ref, *, cap):
    E, N = aff_ref.shape
    CH = min(N, 4096)
    SC = min(N, SEL_CHUNK)

    def count(pred):
        def body(c, acc):
            start = pl.multiple_of(c * CH, CH)
            bits = lax.bitcast_convert_type(aff_ref[:, pl.ds(start, CH)], jnp.int32)
            tok = start + lax.broadcasted_iota(jnp.int32, (E, CH), 1)
            return acc + jnp.sum(pred(bits, tok), axis=1, keepdims=True)
        return lax.fori_loop(0, N // CH, body, jnp.zeros((E, 1), F32))

    def value_step(_, lohi):
        lo, hi = lohi
        mid = lo + ((hi - lo) >> 1)
        ok = count(lambda b, t: jnp.where(b >= mid, 1.0, 0.0)) >= cap
        return jnp.where(ok, mid, lo), jnp.where(ok, hi, mid)

    thr, _ = lax.fori_loop(0, 32, value_step,
                           (jnp.zeros((E, 1), jnp.int32), jnp.full((E, 1), 0x7F800000, jnp.int32)))
    need = cap - count(lambda b, t: jnp.where(b > thr, 1.0, 0.0))

    def tie_step(_, lohi):
        lo, hi = lohi
        mid = lo + ((hi - lo) >> 1)
        ok = count(lambda b, t: jnp.where(b == thr, jnp.where(t <= mid, 1.0, 0.0), 0.0)) >= need
        return jnp.where(ok, lo, mid), jnp.where(ok, mid, hi)

    _, cut = lax.fori_loop(0, int(np.ceil(np.log2(N))) + 1, tie_step,
                           (jnp.full((E, 1), -1, jnp.int32), jnp.full((E, 1), N - 1, jnp.int32)))

    r = lax.broadcasted_iota(jnp.int32, (SC, SC), 0)
    c = lax.broadcasted_iota(jnp.int32, (SC, SC), 1)
    tri = jnp.where(r <= c, 1.0, 0.0).astype(BF16)

    def emit(ci, carry):
        start = pl.multiple_of(ci * SC, SC)
        bits = lax.bitcast_convert_type(aff_ref[:, pl.ds(start, SC)], jnp.int32)
        tok = start + lax.broadcasted_iota(jnp.int32, (E, SC), 1)
        picked = jnp.where(bits > thr, 1.0, jnp.where(bits == thr, jnp.where(tok <= cut, 1.0, 0.0), 0.0))
        inc = carry + _dot(picked.astype(BF16), tri)
        incl_ref[:, pl.ds(start, SC)] = inc
        rank_ref[:, pl.ds(start, SC)] = jnp.where(picked > 0.0, inc - 1.0, -1.0).astype(jnp.int32)
        return inc[:, SC - 1:SC]

    lax.fori_loop(0, N // SC, emit, jnp.zeros((E, 1), F32))


def _select(aff_t, cap):
    E, N = aff_t.shape
    full = pl.BlockSpec((E, N), lambda i: (0, 0))
    return pl.pallas_call(
        functools.partial(_select_kernel, cap=cap),
        grid=(1,),
        in_specs=[full],
        out_specs=[full, full],
        out_shape=[jax.ShapeDtypeStruct((E, N), jnp.int32), jax.ShapeDtypeStruct((E, N), F32)],
        compiler_params=_cparams("arbitrary"),
        name="select",
    )(aff_t)


def _sc_params():
    cp = pltpu.CompilerParams()
    if "needs_layout_passes" in pltpu.CompilerParams.__dataclass_fields__:
        cp = dataclasses.replace(cp, needs_layout_passes=False)
    return cp


def _compact(rank, aff_t, cap):
    E, N = rank.shape
    CH = min(N, 4096)
    mesh = plsc.VectorSubcoreMesh(core_axis_name="c", subcore_axis_name="s")

    @pl.kernel(out_type=(jax.ShapeDtypeStruct((E * cap,), jnp.int32), jax.ShapeDtypeStruct((E * cap,), F32)),
               mesh=mesh,
               scratch_types=[pltpu.VMEM((CH,), jnp.int32), pltpu.VMEM((CH,), F32),
                              pltpu.VMEM((cap,), jnp.int32), pltpu.VMEM((cap,), F32)],
               compiler_params=_sc_params())
    def compact(rank_hbm, aff_hbm, idx_hbm, gate_hbm, rbuf, abuf, ibuf, gbuf):
        wid = lax.axis_index("s") * mesh.num_cores + lax.axis_index("c")

        @pl.when(wid < E)
        def _():
            @pl.loop(0, N // CH)
            def _(c):
                base = wid * N + c * CH
                pltpu.sync_copy(rank_hbm.at[pl.ds(base, CH)], rbuf)
                pltpu.sync_copy(aff_hbm.at[pl.ds(base, CH)], abuf)

                @pl.loop(0, CH, step=SC_LANES)
                def _(i):
                    r = rbuf[pl.ds(i, SC_LANES)]
                    picked = r >= 0
                    slot = jnp.where(picked, r, 0)
                    tok = c * CH + i + lax.iota(jnp.int32, SC_LANES)
                    plsc.store_scatter(ibuf, [slot], tok, mask=picked)
                    plsc.store_scatter(gbuf, [slot], abuf[pl.ds(i, SC_LANES)], mask=picked)

            pltpu.sync_copy(ibuf, idx_hbm.at[pl.ds(wid * cap, cap)])
            pltpu.sync_copy(gbuf, gate_hbm.at[pl.ds(wid * cap, cap)])

    return compact(rank.reshape(E * N), aff_t.reshape(E * N))


def _gather_rows(x, idx):
    R = idx.shape[0]
    win = GATHER_WINDOW
    mesh = plsc.VectorSubcoreMesh(core_axis_name="c", subcore_axis_name="s")

    @pl.kernel(out_type=jax.ShapeDtypeStruct((R, LANES), x.dtype), mesh=mesh)
    def gather(x_hbm, i_hbm, o_hbm):
        def body(i_vmem, o_vmem):
            pltpu.sync_copy(x_hbm.at[i_vmem.at[0]], o_vmem)

        pltpu.emit_pipeline(
            body,
            grid=(R // win,),
            in_specs=[pl.BlockSpec((1, win), lambda i: (0, i))],
            out_specs=[pl.BlockSpec((win, LANES), lambda i: (i, 0))],
            core_axis_name=("c", "s"),
            dimension_semantics=(pltpu.PARALLEL,),
            trace_scopes=False,
        )(i_hbm, o_hbm)

    return gather(x, idx.reshape(1, R))


def _ffn_kernel(x_ref, wg_ref, wu_ref, wd_ref, gate_ref, y_ref):
    x = jnp.concatenate([x_ref[:, k, :] for k in range(D_MODEL // LANES)], axis=1).astype(BF16)
    g = _dot(x, wg_ref[...])
    hid = (g * jax.nn.sigmoid(g)) * _dot(x, wu_ref[...])
    y_ref[...] = (_dot(hid.astype(BF16), wd_ref[...]) * gate_ref[...]).astype(BF16)


def _ffn(xe, gate, lw, cap):
    E = N_EXPERTS
    M = min(512, cap)
    per = cap // M
    wspec = pl.BlockSpec((None, D_MODEL, D_MODEL), lambda e, s: (e, 0, 0))
    return pl.pallas_call(
        _ffn_kernel,
        grid=(E, per),
        in_specs=[pl.BlockSpec((M, D_MODEL // LANES, LANES), lambda e, s: (e * per + s, 0, 0)), wspec, wspec, wspec,
                  pl.BlockSpec((M, 1), lambda e, s: (e * per + s, 0))],
        out_specs=pl.BlockSpec((M, D_MODEL), lambda e, s: (e * per + s, 0)),
        out_shape=jax.ShapeDtypeStruct((E * cap, D_MODEL), BF16),
        compiler_params=_cparams("parallel", "parallel"),
        name="ffn",
    )(xe, lw['we_g'], lw['we_u'], lw['we_d'], gate)


def _combine_kernel(starts_ref, rounds_ref, x_ref, p_ref, rank_ref, ye_hbm, g_ref, wgate_ref, wproj_ref,
                    gfin_ref, o_ref, stage_ref, sem_ref, *, cap, final):
    j = pl.program_id(0)
    nj = pl.num_programs(0)
    T = x_ref.shape[0]
    E, W = N_EXPERTS, COMB_WIN
    slot = j % 2

    def window_start(tile, rnd, e):
        return pl.multiple_of(jnp.minimum(starts_ref[tile * E + e] + rnd * W, E * cap - W), 16)

    def window_copy(tile, rnd, e, sl):
        return pltpu.make_async_copy(ye_hbm.at[pl.ds(window_start(tile, rnd, e), W), :],
                                     stage_ref.at[sl, pl.ds(e * W, W), :], sem_ref.at[sl])

    def start_all(tile, rnd, sl):
        for e in range(E):
            window_copy(tile, rnd, e, sl).start()

    def wait_all(tile, rnd, sl):
        for e in range(E):
            window_copy(tile, rnd, e, sl).wait()

    @pl.when(j == 0)
    def _():
        start_all(0, 0, 0)

    @pl.when(j + 1 < nj)
    def _():
        start_all(j + 1, 0, 1 - slot)

    rk = rank_ref[...]
    eoff = lax.broadcasted_iota(jnp.int32, (E, T), 0) * cap
    tgt = jnp.where(rk >= 0, rk + eoff, -1).astype(F32)
    tgt_t = jnp.concatenate([tgt, jnp.full((LANES - E, T), -1.0, F32)], axis=0).T
    per = LANES // W
    lane = lax.broadcasted_iota(jnp.int32, (T, LANES), 1)
    lane_e = lane // W
    lane_r = (lane % W).astype(F32)

    def placed(rnd):
        onehot = []
        for blk in range(E // per):
            rows = jnp.zeros((T, LANES), F32)
            tgt_b = jnp.zeros((T, LANES), F32)
            for u in range(per):
                e = blk * per + u
                mine = lane_e == u
                staged = window_start(j, rnd, e).astype(F32) + lane_r
                staged = jnp.where(staged >= (starts_ref[j * E + e] + rnd * W).astype(F32), staged, -2.0)
                rows = jnp.where(mine, staged, rows)
                tgt_b = jnp.where(mine, tgt_t[:, e:e + 1], tgt_b)
            onehot.append(jnp.where(tgt_b == rows, 1.0, 0.0).astype(BF16))
        return _dot(jnp.concatenate(onehot, axis=1), stage_ref[slot])

    wait_all(j, 0, slot)
    moe = placed(0)

    def extra(rnd, acc):
        start_all(j, rnd, slot)
        wait_all(j, rnd, slot)
        return acc + placed(rnd)

    moe = lax.fori_loop(1, rounds_ref[j], extra, moe)

    x = x_ref[...] + moe
    h = _rms(x, g_ref[...]).astype(BF16)
    gate = jax.nn.sigmoid(_dot(h, wgate_ref[...]))
    y = x + gate * _dot(p_ref[...].astype(BF16), wproj_ref[...])
    if final:
        y = _rms(y, gfin_ref[...])
    o_ref[...] = y


def _combine(x1, p, rank, incl, ye, lw, gfin, cap, final):
    N = x1.shape[0]
    E, W = N_EXPERTS, COMB_WIN
    T = min(256, N)
    nt = N // T
    ends = incl[:, T - 1::T].astype(jnp.int32)
    begins = jnp.concatenate([jnp.zeros((E, 1), jnp.int32), ends[:, :-1]], axis=1)
    base = jnp.arange(E, dtype=jnp.int32)[:, None] * cap
    aligned = (base + begins) // 16 * 16
    rounds = jnp.maximum(1, jnp.max((base + ends - aligned + W - 1) // W, axis=0)).astype(jnp.int32)
    starts = aligned.T.reshape(-1)

    tok = lambda w: pl.BlockSpec((T, w), lambda i, *_: (i, 0))
    full = lambda a: pl.BlockSpec(a.shape, lambda i, *_: (0,) * a.ndim)
    consts = (lw['g_ple'], lw['w_pg'], lw['w_pp'], gfin)
    grid_spec = pltpu.PrefetchScalarGridSpec(
        num_scalar_prefetch=2,
        grid=(nt,),
        in_specs=[tok(D_MODEL), tok(PLE_DIM), pl.BlockSpec((E, T), lambda i, *_: (0, i)),
                  pl.BlockSpec(memory_space=pl.ANY)] + [full(a) for a in consts],
        out_specs=tok(D_MODEL),
        scratch_shapes=[pltpu.VMEM((2, E * W, D_MODEL), BF16), pltpu.SemaphoreType.DMA((2,))],
    )
    return pl.pallas_call(
        functools.partial(_combine_kernel, cap=cap, final=final),
        grid_spec=grid_spec,
        out_shape=jax.ShapeDtypeStruct((N, D_MODEL), F32),
        compiler_params=_cparams("arbitrary"),
        name="combine",
    )(starts, rounds, x1, p, rank, ye, *consts)


def _rope_tables(S):
    lane = np.arange(LANES)
    d = lane % HEAD_DIM
    t = jnp.arange(S)
    inv_a = ROPE_THETA ** (-jnp.arange(0, HEAD_DIM // 2, 2, dtype=F32) / (HEAD_DIM // 2))
    pos_a = jnp.where((d // 32 == 0)[None, :], (t // GRID_W)[:, None], (t % GRID_W)[:, None]).astype(F32)
    ang_a = pos_a * inv_a[d % 16][None, :]
    sign_a = jnp.where(d % 32 < 16, -1.0, 1.0)[None, :]
    inv_c = ROPE_THETA ** (-jnp.arange(0, HEAD_DIM, 2, dtype=F32) / HEAD_DIM)
    ang_c = t.astype(F32)[:, None] * inv_c[d % 32][None, :]
    sign_c = jnp.where(d < 32, -1.0, 1.0)[None, :]
    return (jnp.cos(ang_a), jnp.sin(ang_a) * sign_a, jnp.cos(ang_c), jnp.sin(ang_c) * sign_c)


def _layer_weights(i, w):
    w_in = w['w_in'][i]
    cols = lambda r: w_in[:, r[0]:r[1]]
    bf = lambda a: a.astype(BF16)
    row = lambda a: a.reshape(1, -1).astype(F32)
    wz = jnp.pad(cols(_ZL), ((0, 0), (0, LANES - (_ZL[1] - _ZL[0]))))
    w2 = w['gla_w2'][i]
    w2p = jnp.zeros((2, LANES, 256), F32).at[0, 0:16].set(w2[0]).at[1, 16:32].set(w2[1])
    wr = jnp.pad(w['w_router'][i], ((0, 0), (0, LANES - N_EXPERTS)))
    wr_hi = wr.astype(BF16)
    wbr = w['w_branch'][i]
    wbr = jnp.stack([wbr[0][_QPERM], wbr[1], wbr[2][_QPERM], wbr[3]])
    return dict(
        g_mix=row(w['norm_mix'][i]),
        wa=bf(jnp.concatenate([cols(_QA)[:, _QPERM], cols(_KA), cols(_VA)], axis=1)),
        wb=bf(cols(_UB)),
        wc=bf(jnp.concatenate([cols(_QC)[:, _QPERM], cols(_KC), cols(_VC)], axis=1)),
        wd=bf(cols(_DD)),
        wz=bf(wz),
        gq=row(jnp.tile(w['qk_norm'][i, 0], 4)),
        gk=row(jnp.tile(w['qk_norm'][i, 1], 2)),
        sink=w['sink_logit'][i][_HEAD_PERM].astype(F32),
        conv_w=w['conv_dw'][i].astype(F32),
        conv_b=row(w['conv_dw_b'][i]), ln_g=row(w['conv_ln_g'][i]), ln_b=row(w['conv_ln_b'][i]),
        w2=bf(w2p), b2=w['gla_b2'][i].reshape(2, 1, 256).astype(F32),
        gn=row(jnp.tile(w['gla_norm'][i], 4)),
        wg=bf(w_in[:, _GATES:]), wbr=bf(wbr), wout=bf(w['w_out'][i]),
        g_ffn=row(w['norm_ffn'][i]),
        wr_hi=wr_hi, wr_lo=(wr - wr_hi.astype(F32)).astype(BF16),
        we_g=bf(w['w_gate_e'][i]), we_u=bf(w['w_up_e'][i]), we_d=bf(w['w_down_e'][i]),
        g_ple=row(w['norm_ple'][i]), w_pg=bf(w['w_ple_gate'][i]), w_pp=bf(w['w_ple_proj'][i]),
    )


def _trunk(x3, p4, layers, gfin):
    B, S, _ = x3.shape
    N = B * S
    tabs = _rope_tables(S)
    x = x3.reshape(N, D_MODEL)
    cap = max(1, EC_CAPACITY * N // N_EXPERTS)
    for i, lw in enumerate(layers):
        qa, kat, va, zb, qc, kc, vc, d, zl = _proj(x, lw, tabs, S)
        b3 = lambda a: a.reshape(B, S, a.shape[-1])
        oa = _gattn(b3(qa), kat, b3(va))
        ob = _conv(b3(zb), lw)
        oc = _wattn(b3(qc), b3(kc), b3(vc), lw['sink'])
        ogf, ogb = _gla(b3(d), b3(zl), lw['w2'], lw['b2'])
        flat = lambda a: a.reshape(N, a.shape[-1])
        x1, h2, aff_t = _merge(x, flat(oa), flat(ob), flat(oc), flat(ogf), flat(ogb), d, lw)
        rank, incl = _select(aff_t, cap)
        idx, gate = _compact(rank, aff_t, cap)
        sub = D_MODEL // LANES
        idx8 = (idx[:, None] * sub + jnp.arange(sub, dtype=jnp.int32)[None, :]).reshape(-1)
        xe = _gather_rows(h2.reshape(N * sub, LANES), idx8).reshape(-1, sub, LANES)
        ye = _ffn(xe, gate.reshape(-1, 1), lw, cap)
        x = _combine(x1, p4[i].reshape(N, PLE_DIM), rank, incl, ye, lw, gfin, cap, i == len(layers) - 1)
    return x.reshape(B, S, D_MODEL)


def kernel(x_prompt, x_sample, p_prompt, p_sample, norm_mix, w_in, qk_norm, sink_logit, conv_dw, conv_dw_b,
           conv_ln_g, conv_ln_b, gla_w2, gla_b2, gla_norm, w_branch, w_out, norm_ffn, w_router, w_gate_e,
           w_up_e, w_down_e, norm_ple, w_ple_gate, w_ple_proj, norm_final):
    w = dict(norm_mix=norm_mix, w_in=w_in, qk_norm=qk_norm, sink_logit=sink_logit, conv_dw=conv_dw,
             conv_dw_b=conv_dw_b, conv_ln_g=conv_ln_g, conv_ln_b=conv_ln_b, gla_w2=gla_w2, gla_b2=gla_b2,
             gla_norm=gla_norm, w_branch=w_branch, w_out=w_out, norm_ffn=norm_ffn, w_router=w_router,
             w_gate_e=w_gate_e, w_up_e=w_up_e, w_down_e=w_down_e, norm_ple=norm_ple,
             w_ple_gate=w_ple_gate, w_ple_proj=w_ple_proj)
    layers = [_layer_weights(i, w) for i in range(norm_mix.shape[0])]
    gfin = norm_final.reshape(1, -1).astype(F32)
    return (_trunk(x_prompt, p_prompt, layers, gfin), _trunk(x_sample, p_sample, layers, gfin))
```

```python
import dataclasses
import functools

import jax
import jax.numpy as jnp
import numpy as np
from jax import lax
from jax.experimental import pallas as pl
from jax.experimental.pallas import tpu as pltpu
from jax.experimental.pallas import tpu_sc as plsc

F32 = jnp.float32
BF16 = jnp.bfloat16

D_MODEL = 1024
DEPTH = 4
GRID_W = 64
HEAD_DIM = 64
ROPE_THETA = 10000.0
NORM_EPS = 1e-6
CONV_CH = 256
CONV_WIDTH = 31
CONV_HALO = 16
WINDOW = 128
GLA_TAU = 16.0
GLA_CHUNK = 64
GLA_BLOCK = 256
GLA_ROWS = 2
N_EXPERTS = 16
EC_CAPACITY = 2
PLE_DIM = 256
LANES = 128
SUBLANES = 8
VMEM_LIMIT = 56 * 1024 * 1024
LOG2E = 1.4426950408889634
SEL_CHUNK = 512
COMB_WIN = 64
SC_LANES = 16
GATHER_WINDOW = 128

_QA, _KA, _VA = (0, 256), (256, 384), (384, 512)
_UB = (512, 1024)
_QC, _KC, _VC = (1024, 1280), (1280, 1408), (1408, 1536)
_DD = (1536, 2560)
_ZL = (2560, 2592)
_GATES = 2592
_QPERM = np.concatenate([np.arange(0, 64), np.arange(128, 192), np.arange(64, 128), np.arange(192, 256)])
_HEAD_PERM = np.array([0, 2, 1, 3])


def _cparams(*sem):
    return pltpu.CompilerParams(dimension_semantics=sem, vmem_limit_bytes=VMEM_LIMIT)


def _dot(a, b):
    return jnp.dot(a, b, preferred_element_type=F32)


def _dot_nt(a, b):
    return lax.dot_general(a, b, (((1,), (1,)), ((), ())), preferred_element_type=F32)


def _rms(x, g):
    return x * lax.rsqrt(jnp.mean(x * x, axis=-1, keepdims=True) + NORM_EPS) * g


def _split2(x):
    hi = x.astype(BF16)
    lo = (x - hi.astype(F32)).astype(BF16)
    return hi, lo


def _group_ones(width, group):
    r = lax.broadcasted_iota(jnp.int32, (width, width), 0) // group
    c = lax.broadcasted_iota(jnp.int32, (width, width), 1) // group
    return jnp.where(r == c, 1.0, 0.0).astype(BF16)


def _group_meansq(x, group):
    hi, lo = _split2(x * x)
    ones = _group_ones(x.shape[1], group)
    return (_dot(hi, ones) + _dot(lo, ones)) * (1.0 / group)


def _rope(x, cos, sin_signed, half):
    width = x.shape[1]
    lane = lax.broadcasted_iota(jnp.int32, x.shape, 1)
    from_lo = pltpu.roll(x, half, 1)
    from_hi = pltpu.roll(x, width - half, 1)
    partner = jnp.where((lane & half) != 0, from_lo, from_hi)
    return x * cos + partner * sin_signed


def _proj_kernel(x_ref, g_ref, wa_ref, wb_ref, wc_ref, wd_ref, wz_ref, gq_ref, gk_ref,
                 ca_ref, sa_ref, cc_ref, sc_ref,
                 qa_ref, kat_ref, va_ref, zb_ref, qc_ref, kc_ref, vc_ref, d_ref, zl_ref):
    h = _rms(x_ref[...], g_ref[...]).astype(BF16)
    scale = HEAD_DIM ** -0.5

    ua = _dot(h, wa_ref[...])
    ca, sa = ca_ref[...], sa_ref[...]
    q = ua[:, :256]
    q = q * lax.rsqrt(_group_meansq(q, HEAD_DIM) + NORM_EPS) * gq_ref[...]
    q = _rope(q, jnp.concatenate([ca, ca], axis=1), jnp.concatenate([sa, sa], axis=1), 16)
    qa_ref[...] = (q * (scale * LOG2E)).astype(BF16)
    k = ua[:, 256:384]
    k = k * lax.rsqrt(_group_meansq(k, HEAD_DIM) + NORM_EPS) * gk_ref[...]
    kat_ref[...] = _rope(k, ca, sa, 16).T.astype(BF16)
    va_ref[...] = ua[:, 384:].astype(BF16)

    ub = _dot(h, wb_ref[...])
    zb_ref[...] = (ub[:, :CONV_CH] * jax.nn.sigmoid(ub[:, CONV_CH:])).astype(BF16)

    uc = _dot(h, wc_ref[...])
    cc, sc = cc_ref[...], sc_ref[...]
    qc = _rope(uc[:, :256], jnp.concatenate([cc, cc], axis=1), jnp.concatenate([sc, sc], axis=1), 32)
    qc_ref[...] = (qc * scale).astype(BF16)
    kc_ref[...] = _rope(uc[:, 256:384], cc, sc, 32).astype(BF16)
    vc_ref[...] = uc[:, 384:].astype(BF16)

    ud = _dot(h, wd_ref[...])
    d_ref[:, :256] = (ud[:, :256] * scale).astype(BF16)
    d_ref[:, 256:] = ud[:, 256:].astype(BF16)
    zl_ref[...] = _dot(h, wz_ref[...]).astype(BF16)


def _proj(x, lw, tabs, S):
    N = x.shape[0]
    T = min(512, S)
    per_row = S // T
    tok = lambda w: pl.BlockSpec((T, w), lambda i: (i, 0))
    full = lambda a: pl.BlockSpec(a.shape, lambda i: (0,) * a.ndim)
    tab = pl.BlockSpec((T, LANES), lambda i: (i % per_row, 0))
    widths = (256, 128, 128, 256, 256, 128, 128, 1024, 128)
    consts = (lw['g_mix'], lw['wa'], lw['wb'], lw['wc'], lw['wd'], lw['wz'], lw['gq'], lw['gk'])
    return pl.pallas_call(
        _proj_kernel,
        grid=(N // T,),
        in_specs=[tok(D_MODEL)] + [full(a) for a in consts] + [tab] * 4,
        out_specs=[tok(256), pl.BlockSpec((LANES, T), lambda i: (0, i))] + [tok(w) for w in widths[2:]],
        out_shape=[jax.ShapeDtypeStruct((N, 256), BF16), jax.ShapeDtypeStruct((LANES, N), BF16)]
                  + [jax.ShapeDtypeStruct((N, w), BF16) for w in widths[2:]],
        compiler_params=_cparams("parallel"),
        name="proj",
    )(x, *consts, *tabs)


def _stack_heads(qb):
    lane = lax.broadcasted_iota(jnp.int32, qb.shape, 1)
    zero = jnp.zeros_like(qb)
    return jnp.concatenate([jnp.where(lane < HEAD_DIM, qb, zero), jnp.where(lane < HEAD_DIM, zero, qb)], axis=0)


def _unstack_heads(o, T):
    lane = lax.broadcasted_iota(jnp.int32, (T, LANES), 1)
    return jnp.where(lane < HEAD_DIM, o[:T], o[T:])


def _gattn_kernel(q_ref, kt_ref, v_ref, o_ref, *, tk):
    T = q_ref.shape[0]
    S = kt_ref.shape[1]
    lane_v = lax.broadcasted_iota(jnp.int32, (tk, LANES), 1)
    one = jnp.ones((tk, LANES), BF16)
    qs = [_stack_heads(q_ref[:, j * LANES:(j + 1) * LANES]) for j in range(2)]
    m = [jnp.full((2 * T, 1), -jnp.inf, F32) for _ in range(2)]
    acc = [jnp.zeros((2 * T, LANES), F32) for _ in range(2)]
    for c in range(S // tk):
        kt = kt_ref[:, c * tk:(c + 1) * tk]
        v = v_ref[c * tk:(c + 1) * tk, :]
        va = jnp.where(lane_v < HEAD_DIM, v, one)
        vb = jnp.where(lane_v < HEAD_DIM, one, v)
        for j in range(2):
            s = _dot(qs[j], kt)
            m_new = jnp.maximum(m[j], jnp.max(s, axis=1, keepdims=True))
            alpha = jnp.exp2(m[j] - m_new)
            p = jnp.exp2(s - m_new).astype(BF16)
            pv = jnp.concatenate([_dot(p[:T], va), _dot(p[T:], vb)], axis=0)
            acc[j] = alpha * acc[j] + pv
            m[j] = m_new
    lane_o = lax.broadcasted_iota(jnp.int32, (T, LANES), 1)
    for j in range(2):
        a, b = acc[j][:T], acc[j][T:]
        o = jnp.where(lane_o < HEAD_DIM, a / pltpu.roll(a, HEAD_DIM, 1), b / pltpu.roll(b, HEAD_DIM, 1))
        o_ref[:, j * LANES:(j + 1) * LANES] = o.astype(BF16)


def _gattn(q, kt, v):
    B, S, _ = q.shape
    T = min(256, S)
    tk = min(1024, S)
    return pl.pallas_call(
        functools.partial(_gattn_kernel, tk=tk),
        grid=(B, S // T),
        in_specs=[pl.BlockSpec((None, T, 256), lambda b, i: (b, i, 0)),
                  pl.BlockSpec((LANES, S), lambda b, i: (0, b)),
                  pl.BlockSpec((None, S, LANES), lambda b, i: (b, 0, 0))],
        out_specs=pl.BlockSpec((None, T, 256), lambda b, i: (b, i, 0)),
        out_shape=jax.ShapeDtypeStruct((B, S, 256), BF16),
        compiler_params=_cparams("parallel", "parallel"),
        name="gattn",
    )(q, kt, v)


def _wattn_kernel(sink_ref, q_ref, k_ref, v_ref, o_ref, *, kw):
    T = q_ref.shape[0]
    S = k_ref.shape[0]
    i = pl.program_id(1)
    start = pl.multiple_of(jnp.clip(i * T - WINDOW, 0, S - kw), WINDOW)
    kwin = k_ref[pl.ds(start, kw), :]
    vwin = v_ref[pl.ds(start, kw), :]
    row = lax.broadcasted_iota(jnp.int32, (2 * T, kw), 0)
    col = lax.broadcasted_iota(jnp.int32, (2 * T, kw), 1)
    qpos = i * T + jnp.where(row < T, row, row - T)
    valid = jnp.abs(start + col - qpos) <= WINDOW
    first = lax.broadcasted_iota(jnp.int32, (2 * T, 1), 0) < T
    for j in range(2):
        qst = _stack_heads(q_ref[:, j * LANES:(j + 1) * LANES])
        s = jnp.where(valid, _dot_nt(qst, kwin), -jnp.inf)
        sk = jnp.where(first, sink_ref[2 * j], sink_ref[2 * j + 1])
        m = jnp.maximum(jnp.max(s, axis=1, keepdims=True), sk)
        e = jnp.exp(s - m)
        p = e / (jnp.sum(e, axis=1, keepdims=True) + jnp.exp(sk - m))
        o = _dot(p.astype(BF16), vwin)
        o_ref[:, j * LANES:(j + 1) * LANES] = _unstack_heads(o, T).astype(BF16)


def _wattn(q, k, v, sink):
    B, S, _ = q.shape
    T = min(256, S - 2 * WINDOW) if S > 2 * WINDOW else S
    kw = min(T + 2 * WINDOW, S)
    return pl.pallas_call(
        functools.partial(_wattn_kernel, kw=kw),
        grid=(B, S // T),
        in_specs=[pl.BlockSpec(memory_space=pltpu.SMEM),
                  pl.BlockSpec((None, T, 256), lambda b, i: (b, i, 0)),
                  pl.BlockSpec((None, S, LANES), lambda b, i: (b, 0, 0)),
                  pl.BlockSpec((None, S, LANES), lambda b, i: (b, 0, 0))],
        out_specs=pl.BlockSpec((None, T, 256), lambda b, i: (b, i, 0)),
        out_shape=jax.ShapeDtypeStruct((B, S, 256), BF16),
        compiler_params=_cparams("parallel", "parallel"),
        name="wattn",
    )(sink, q, k, v)


def _conv_kernel(zp_ref, zc_ref, zn_ref, w_ref, b_ref, g_ref, beta_ref, o_ref, buf_ref, sh_ref):
    T = zc_ref.shape[0]
    i = pl.program_id(1)
    H = CONV_HALO
    keep_prev = jnp.where(i > 0, 1.0, 0.0)
    keep_next = jnp.where(i < pl.num_programs(1) - 1, 1.0, 0.0)
    buf_ref[0:H, :] = zp_ref[...].astype(F32) * keep_prev
    buf_ref[H:H + T, :] = zc_ref[...].astype(F32)
    buf_ref[H + T:, :] = zn_ref[...].astype(F32) * keep_next
    span = sh_ref.shape[1]
    for r in range(1, SUBLANES):
        sh_ref[r] = buf_ref[r:r + span, :]
    acc = jnp.zeros((T, CONV_CH), F32)
    off = H - CONV_WIDTH // 2
    for tap in range(CONV_WIDTH):
        r = (off + tap) % SUBLANES
        a = off + tap - r
        rows = buf_ref[a:a + T, :] if r == 0 else sh_ref[r, a:a + T, :]
        acc = acc + rows * w_ref[tap:tap + 1, :]
    z = acc + b_ref[...]
    mu = jnp.mean(z, axis=-1, keepdims=True)
    zc = z - mu
    var = jnp.mean(zc * zc, axis=-1, keepdims=True)
    y = zc * lax.rsqrt(var + NORM_EPS) * g_ref[...] + beta_ref[...]
    o_ref[...] = (y * jax.nn.sigmoid(y)).astype(BF16)


def _conv(z, lw):
    B, S, _ = z.shape
    T = min(512, S)
    H = CONV_HALO
    per = T // H
    last = S // H - 1
    full = lambda a: pl.BlockSpec(a.shape, lambda b, i: (0,) * a.ndim)
    consts = (lw['conv_w'], lw['conv_b'], lw['ln_g'], lw['ln_b'])
    return pl.pallas_call(
        _conv_kernel,
        grid=(B, S // T),
        in_specs=[pl.BlockSpec((None, H, CONV_CH), lambda b, i: (b, jnp.maximum(i * per - 1, 0), 0)),
                  pl.BlockSpec((None, T, CONV_CH), lambda b, i: (b, i, 0)),
                  pl.BlockSpec((None, H, CONV_CH), lambda b, i: (b, jnp.minimum((i + 1) * per, last), 0))]
                 + [full(a) for a in consts],
        out_specs=pl.BlockSpec((None, T, CONV_CH), lambda b, i: (b, i, 0)),
        out_shape=jax.ShapeDtypeStruct((B, S, CONV_CH), BF16),
        scratch_shapes=[pltpu.VMEM((T + 2 * H, CONV_CH), F32),
                        pltpu.VMEM((SUBLANES, T + 2 * H - SUBLANES, CONV_CH), F32)],
        compiler_params=_cparams("parallel", "parallel"),
        name="conv",
    )(z, z, z, *consts)


def _gla_block(d_ref, zl_ref, w2, b2, tri, att_mask, st_ref, reverse):
    TB = d_ref.shape[0]
    L = GLA_CHUNK
    nch = TB // L

    pre = _dot(zl_ref[...], w2) + b2
    la = (jnp.minimum(pre, 0.0) - jnp.log(1.0 + jnp.exp(-jnp.abs(pre)))) * (1.0 / GLA_TAU)

    hi = la.astype(BF16)
    r1 = la - hi.astype(F32)
    mid = r1.astype(BF16)
    lo = (r1 - mid.astype(F32)).astype(BF16)
    cums = _dot(tri, jnp.concatenate([hi, mid, lo], axis=1))
    b = cums[:, :256] + cums[:, 256:512] + cums[:, 512:]
    mid_row = L // 2 if reverse else L // 2 - 1
    last_row = 0 if reverse else L - 1
    per_chunk = lambda row: jnp.concatenate(
        [jnp.broadcast_to(b[ci * L + row:ci * L + row + 1], (L, 256)) for ci in range(nch)], axis=0)
    bmid, blast = per_chunk(mid_row), per_chunk(last_row)

    q = d_ref[:, 0:256].astype(F32)
    k = d_ref[:, 256:512].astype(F32)
    v = d_ref[:, 512:768]
    qt = (q * jnp.exp(b - bmid)).astype(BF16)
    kt = (k * jnp.exp(bmid - b)).astype(BF16)
    qe = (q * jnp.exp(b)).astype(BF16)
    kl = (k * jnp.exp(blast - b)).astype(BF16)
    dec = jnp.exp(blast)

    rr = lax.broadcasted_iota(jnp.int32, (LANES, LANES), 0) // HEAD_DIM
    cc = lax.broadcasted_iota(jnp.int32, (LANES, LANES), 1) // HEAD_DIM
    head_diag = rr == cc
    rowid = lax.broadcasted_iota(jnp.int32, (TB, LANES), 0) // L
    order = range(nch - 1, -1, -1) if reverse else range(nch)

    out = []
    for p in range(2):
        ls = slice(p * LANES, (p + 1) * LANES)
        vb = v[:, ls]
        att = _dot_nt(_stack_heads(qt[:, ls]), kt[:, ls]) * att_mask
        o_intra = _unstack_heads(_dot(att.astype(BF16), vb), TB)
        vt = vb.astype(F32).T.astype(BF16)
        klb = kl[:, ls]
        kv_t = [jnp.where(head_diag, _dot(vt, jnp.where(rowid == ci, klb, jnp.zeros_like(klb))), 0.0)
                for ci in range(nch)]
        st = st_ref[p]
        o_inter = [None] * nch
        for ci in order:
            o_inter[ci] = _dot_nt(qe[ci * L:(ci + 1) * L, ls], st.astype(BF16))
            st = st * dec[ci * L:ci * L + 1, ls] + kv_t[ci]
        st_ref[p] = st
        out.append(o_intra + jnp.concatenate(o_inter, axis=0))
    return out


def _gla_kernel(df_ref, zf_ref, db_ref, zb_ref, w2_ref, b2_ref, tri_ref, mask_ref, of_ref, ob_ref, st_ref):
    @pl.when(pl.program_id(1) == 0)
    def _():
        st_ref[...] = jnp.zeros(st_ref.shape, F32)

    for r in range(df_ref.shape[0]):
        fwd = _gla_block(df_ref.at[r], zf_ref.at[r], w2_ref[0], b2_ref[0], tri_ref[0], mask_ref[0],
                         st_ref.at[r, 0], False)
        bwd = _gla_block(db_ref.at[r], zb_ref.at[r], w2_ref[1], b2_ref[1], tri_ref[1], mask_ref[1],
                         st_ref.at[r, 1], True)
        for p in range(2):
            of_ref[r, :, p * LANES:(p + 1) * LANES] = fwd[p]
            ob_ref[r, :, p * LANES:(p + 1) * LANES] = bwd[p]


def _gla_masks(TB):
    L = GLA_CHUNK
    r = np.arange(TB)[:, None]
    c = np.arange(TB)[None, :]
    same = (r // L) == (c // L)
    tri = np.stack([same & (c <= r), same & (c >= r)]).astype(np.float32)
    return jnp.asarray(tri, BF16), jnp.asarray(np.concatenate([tri, tri], axis=1), F32)


def _gla(d, zl, w2, b2):
    B, S, _ = d.shape
    TB = min(GLA_BLOCK, S)
    nb = S // TB
    R = GLA_ROWS if B % GLA_ROWS == 0 else 1
    fblk = lambda b, i: (b, i, 0)
    bblk = lambda b, i: (b, nb - 1 - i, 0)
    full = lambda a: pl.BlockSpec(a.shape, lambda b, i: (0,) * a.ndim)
    tri, mask = _gla_masks(TB)
    out = jax.ShapeDtypeStruct((B, S, 256), F32)
    return pl.pallas_call(
        _gla_kernel,
        grid=(B // R, nb),
        in_specs=[pl.BlockSpec((R, TB, 1024), fblk), pl.BlockSpec((R, TB, LANES), fblk),
                  pl.BlockSpec((R, TB, 1024), bblk), pl.BlockSpec((R, TB, LANES), bblk),
                  full(w2), full(b2), full(tri), full(mask)],
        out_specs=[pl.BlockSpec((R, TB, 256), fblk), pl.BlockSpec((R, TB, 256), bblk)],
        out_shape=[out, out],
        scratch_shapes=[pltpu.VMEM((R, 2, 2, LANES, LANES), F32)],
        compiler_params=_cparams("parallel", "arbitrary"),
        name="gla",
    )(d, zl, d, zl, w2, b2, tri, mask)


def _merge_kernel(x_ref, oa_ref, ob_ref, oc_ref, ogf_ref, ogb_ref, rd_ref, gmix_ref, wg_ref, wbr_ref, wout_ref,
                  gn_ref, gffn_ref, wrh_ref, wrl_ref, x1_ref, h2_ref, aff_ref):
    x = x_ref[...]
    h = _rms(x, gmix_ref[...]).astype(BF16)
    og = ogf_ref[...] + ogb_ref[...]
    od = og * lax.rsqrt(_group_meansq(og, HEAD_DIM) + NORM_EPS) * gn_ref[...]
    rd = rd_ref[...].astype(F32)
    od = (od * (rd * jax.nn.sigmoid(rd))).astype(BF16)
    branches = (oa_ref[...], ob_ref[...], oc_ref[...], od)
    merged = jnp.zeros(x.shape, F32)
    for i, o in enumerate(branches):
        gate = jax.nn.sigmoid(_dot(h, wg_ref[:, i * D_MODEL:(i + 1) * D_MODEL]))
        merged = merged + gate * _dot(o, wbr_ref[i])
    x1 = x + _dot(merged.astype(BF16), wout_ref[...])
    x1_ref[...] = x1
    h2 = _rms(x1, gffn_ref[...])
    h2_ref[...] = h2
    hh, hl = _split2(h2)
    logits = _dot(hh, wrh_ref[...]) + _dot(hh, wrl_ref[...]) + _dot(hl, wrh_ref[...])
    lt = logits.T[:N_EXPERTS]
    e = jnp.exp(lt - jnp.max(lt, axis=0, keepdims=True))
    aff_ref[...] = e / jnp.sum(e, axis=0, keepdims=True)


def _merge(x, oa, ob, oc, ogf, ogb, d, lw):
    N = x.shape[0]
    T = min(256, N)
    tok = lambda w: pl.BlockSpec((T, w), lambda i: (i, 0))
    full = lambda a: pl.BlockSpec(a.shape, lambda i: (0,) * a.ndim)
    consts = (lw['g_mix'], lw['wg'], lw['wbr'], lw['wout'], lw['gn'], lw['g_ffn'], lw['wr_hi'], lw['wr_lo'])
    return pl.pallas_call(
        _merge_kernel,
        grid=(N // T,),
        in_specs=[tok(D_MODEL), tok(256), tok(256), tok(256), tok(256), tok(256),
                  pl.BlockSpec((T, 256), lambda i: (i, 3))] + [full(a) for a in consts],
        out_specs=[tok(D_MODEL), tok(D_MODEL), pl.BlockSpec((N_EXPERTS, T), lambda i: (0, i))],
        out_shape=[jax.ShapeDtypeStruct((N, D_MODEL), F32), jax.ShapeDtypeStruct((N, D_MODEL), F32),
                   jax.ShapeDtypeStruct((N_EXPERTS, N), F32)],
        compiler_params=_cparams("parallel"),
        name="merge",
    )(x, oa, ob, oc, ogf, ogb, d, *consts)


def _select_kernel(aff_ref, rank_ref, incl_ref, *, cap):
    E, N = aff_ref.shape
    CH = min(N, 4096)
    SC = min(N, SEL_CHUNK)

    def count(pred):
        def body(c, acc):
            start = pl.multiple_of(c * CH, CH)
            bits = lax.bitcast_convert_type(aff_ref[:, pl.ds(start, CH)], jnp.int32)
            tok = start + lax.broadcasted_iota(jnp.int32, (E, CH), 1)
            return acc + jnp.sum(pred(bits, tok), axis=1, keepdims=True)
        return lax.fori_loop(0, N // CH, body, jnp.zeros((E, 1), F32))

    def value_step(_, lohi):
        lo, hi = lohi
        mid = lo + ((hi - lo) >> 1)
        ok = count(lambda b, t: jnp.where(b >= mid, 1.0, 0.0)) >= cap
        return jnp.where(ok, mid, lo), jnp.where(ok, hi, mid)

    thr, _ = lax.fori_loop(0, 32, value_step,
                           (jnp.zeros((E, 1), jnp.int32), jnp.full((E, 1), 0x7F800000, jnp.int32)))
    need = cap - count(lambda b, t: jnp.where(b > thr, 1.0, 0.0))

    def tie_step(_, lohi):
        lo, hi = lohi
        mid = lo + ((hi - lo) >> 1)
        ok = count(lambda b, t: jnp.where(b == thr, jnp.where(t <= mid, 1.0, 0.0), 0.0)) >= need
        return jnp.where(ok, lo, mid), jnp.where(ok, mid, hi)

    _, cut = lax.fori_loop(0, int(np.ceil(np.log2(N))) + 1, tie_step,
                           (jnp.full((E, 1), -1, jnp.int32), jnp.full((E, 1), N - 1, jnp.int32)))

    r = lax.broadcasted_iota(jnp.int32, (SC, SC), 0)
    c = lax.broadcasted_iota(jnp.int32, (SC, SC), 1)
    tri = jnp.where(r <= c, 1.0, 0.0).astype(BF16)

    def emit(ci, carry):
        start = pl.multiple_of(ci * SC, SC)
        bits = lax.bitcast_convert_type(aff_ref[:, pl.ds(start, SC)], jnp.int32)
        tok = start + lax.broadcasted_iota(jnp.int32, (E, SC), 1)
        picked = jnp.where(bits > thr, 1.0, jnp.where(bits == thr, jnp.where(tok <= cut, 1.0, 0.0), 0.0))
        inc = carry + _dot(picked.astype(BF16), tri)
        incl_ref[:, pl.ds(start, SC)] = inc
        rank_ref[:, pl.ds(start, SC)] = jnp.where(picked > 0.0, inc - 1.0, -1.0).astype(jnp.int32)
        return inc[:, SC - 1:SC]

    lax.fori_loop(0, N // SC, emit, jnp.zeros((E, 1), F32))


def _select(aff_t, cap):
    E, N = aff_t.shape
    full = pl.BlockSpec((E, N), lambda i: (0, 0))
    return pl.pallas_call(
        functools.partial(_select_kernel, cap=cap),
        grid=(1,),
        in_specs=[full],
        out_specs=[full, full],
        out_shape=[jax.ShapeDtypeStruct((E, N), jnp.int32), jax.ShapeDtypeStruct((E, N), F32)],
        compiler_params=_cparams("arbitrary"),
        name="select",
    )(aff_t)


def _sc_params():
    cp = pltpu.CompilerParams()
    if "needs_layout_passes" in pltpu.CompilerParams.__dataclass_fields__:
        cp = dataclasses.replace(cp, needs_layout_passes=False)
    return cp


def _compact(rank, aff_t, cap):
    E, N = rank.shape
    CH = min(N, 4096)
    mesh = plsc.VectorSubcoreMesh(core_axis_name="c", subcore_axis_name="s")

    @pl.kernel(out_type=(jax.ShapeDtypeStruct((E * cap,), jnp.int32), jax.ShapeDtypeStruct((E * cap,), F32)),
               mesh=mesh,
               scratch_types=[pltpu.VMEM((CH,), jnp.int32), pltpu.VMEM((CH,), F32),
                              pltpu.VMEM((cap,), jnp.int32), pltpu.VMEM((cap,), F32)],
               compiler_params=_sc_params())
    def compact(rank_hbm, aff_hbm, idx_hbm, gate_hbm, rbuf, abuf, ibuf, gbuf):
        wid = lax.axis_index("s") * mesh.num_cores + lax.axis_index("c")

        @pl.when(wid < E)
        def _():
            @pl.loop(0, N // CH)
            def _(c):
                base = wid * N + c * CH
                pltpu.sync_copy(rank_hbm.at[pl.ds(base, CH)], rbuf)
                pltpu.sync_copy(aff_hbm.at[pl.ds(base, CH)], abuf)

                @pl.loop(0, CH, step=SC_LANES)
                def _(i):
                    r = rbuf[pl.ds(i, SC_LANES)]
                    picked = r >= 0
                    slot = jnp.where(picked, r, 0)
                    tok = c * CH + i + lax.iota(jnp.int32, SC_LANES)
                    plsc.store_scatter(ibuf, [slot], tok, mask=picked)
                    plsc.store_scatter(gbuf, [slot], abuf[pl.ds(i, SC_LANES)], mask=picked)

            pltpu.sync_copy(ibuf, idx_hbm.at[pl.ds(wid * cap, cap)])
            pltpu.sync_copy(gbuf, gate_hbm.at[pl.ds(wid * cap, cap)])

    return compact(rank.reshape(E * N), aff_t.reshape(E * N))


def _gather_rows(x, idx):
    R = idx.shape[0]
    win = GATHER_WINDOW
    mesh = plsc.VectorSubcoreMesh(core_axis_name="c", subcore_axis_name="s")

    @pl.kernel(out_type=jax.ShapeDtypeStruct((R, LANES), x.dtype), mesh=mesh)
    def gather(x_hbm, i_hbm, o_hbm):
        def body(i_vmem, o_vmem):
            pltpu.sync_copy(x_hbm.at[i_vmem.at[0]], o_vmem)

        pltpu.emit_pipeline(
            body,
            grid=(R // win,),
            in_specs=[pl.BlockSpec((1, win), lambda i: (0, i))],
            out_specs=[pl.BlockSpec((win, LANES), lambda i: (i, 0))],
            core_axis_name=("c", "s"),
            dimension_semantics=(pltpu.PARALLEL,),
            trace_scopes=False,
        )(i_hbm, o_hbm)

    return gather(x, idx.reshape(1, R))


def _ffn_kernel(x_ref, wg_ref, wu_ref, wd_ref, gate_ref, y_ref, w_ref):
    @pl.when(pl.program_id(1) == 0)
    def _():
        w_ref[0] = wg_ref[...].astype(BF16)
        w_ref[1] = wu_ref[...].astype(BF16)
        w_ref[2] = wd_ref[...].astype(BF16)

    x = jnp.concatenate([x_ref[k] for k in range(D_MODEL // LANES)], axis=1).astype(BF16)
    g = _dot(x, w_ref[0])
    hid = (g * jax.nn.sigmoid(g)) * _dot(x, w_ref[1])
    y_ref[...] = (_dot(hid.astype(BF16), w_ref[2]) * gate_ref[...]).astype(BF16)


def _ffn(xe, gate, lw, cap):
    E = N_EXPERTS
    M = min(512, cap)
    per = cap // M
    wspec = pl.BlockSpec((None, D_MODEL, D_MODEL), lambda e, s: (e, 0, 0))
    return pl.pallas_call(
        _ffn_kernel,
        grid=(E, per),
        in_specs=[pl.BlockSpec((D_MODEL // LANES, M, LANES), lambda e, s: (0, e * per + s, 0)), wspec, wspec, wspec,
                  pl.BlockSpec((M, 1), lambda e, s: (e * per + s, 0))],
        out_specs=pl.BlockSpec((M, D_MODEL), lambda e, s: (e * per + s, 0)),
        out_shape=jax.ShapeDtypeStruct((E * cap, D_MODEL), BF16),
        scratch_shapes=[pltpu.VMEM((3, D_MODEL, D_MODEL), BF16)],
        compiler_params=_cparams("parallel", "arbitrary"),
        name="ffn",
    )(xe, lw['we_g'], lw['we_u'], lw['we_d'], gate)


def _combine_kernel(starts_ref, rounds_ref, x_ref, p_ref, rank_ref, ye_hbm, g_ref, wgate_ref, wproj_ref,
                    gfin_ref, o_ref, stage_ref, sem_ref, *, cap, final):
    j = pl.program_id(0)
    nj = pl.num_programs(0)
    T = x_ref.shape[0]
    E, W = N_EXPERTS, COMB_WIN
    slot = j % 2

    def window_start(tile, rnd, e):
        return pl.multiple_of(jnp.minimum(starts_ref[tile * E + e] + rnd * W, E * cap - W), 16)

    def window_copy(tile, rnd, e, sl):
        return pltpu.make_async_copy(ye_hbm.at[pl.ds(window_start(tile, rnd, e), W), :],
                                     stage_ref.at[sl, pl.ds(e * W, W), :], sem_ref.at[sl])

    def start_all(tile, rnd, sl):
        for e in range(E):
            window_copy(tile, rnd, e, sl).start()

    def wait_all(tile, rnd, sl):
        for e in range(E):
            window_copy(tile, rnd, e, sl).wait()

    @pl.when(j == 0)
    def _():
        start_all(0, 0, 0)

    @pl.when(j + 1 < nj)
    def _():
        start_all(j + 1, 0, 1 - slot)

    rk = rank_ref[...]
    eoff = lax.broadcasted_iota(jnp.int32, (E, T), 0) * cap
    tgt = jnp.where(rk >= 0, rk + eoff, -1).astype(F32)
    tgt_t = jnp.concatenate([tgt, jnp.full((LANES - E, T), -1.0, F32)], axis=0).T
    per = LANES // W
    lane = lax.broadcasted_iota(jnp.int32, (T, LANES), 1)
    lane_e = lane // W
    lane_r = (lane % W).astype(F32)

    def placed(rnd):
        onehot = []
        for blk in range(E // per):
            rows = jnp.zeros((T, LANES), F32)
            tgt_b = jnp.zeros((T, LANES), F32)
            for u in range(per):
                e = blk * per + u
                mine = lane_e == u
                staged = window_start(j, rnd, e).astype(F32) + lane_r
                staged = jnp.where(staged >= (starts_ref[j * E + e] + rnd * W).astype(F32), staged, -2.0)
                rows = jnp.where(mine, staged, rows)
                tgt_b = jnp.where(mine, tgt_t[:, e:e + 1], tgt_b)
            onehot.append(jnp.where(tgt_b == rows, 1.0, 0.0).astype(BF16))
        return _dot(jnp.concatenate(onehot, axis=1), stage_ref[slot])

    wait_all(j, 0, slot)
    moe = placed(0)

    def extra(rnd, acc):
        start_all(j, rnd, slot)
        wait_all(j, rnd, slot)
        return acc + placed(rnd)

    moe = lax.fori_loop(1, rounds_ref[j], extra, moe)

    x = x_ref[...] + moe
    h = _rms(x, g_ref[...]).astype(BF16)
    gate = jax.nn.sigmoid(_dot(h, wgate_ref[...]))
    y = x + gate * _dot(p_ref[...].astype(BF16), wproj_ref[...])
    if final:
        y = _rms(y, gfin_ref[...])
    o_ref[...] = y


def _combine(x1, p, rank, incl, ye, lw, gfin, cap, final):
    N = x1.shape[0]
    E, W = N_EXPERTS, COMB_WIN
    T = min(256, N)
    nt = N // T
    ends = incl[:, T - 1::T].astype(jnp.int32)
    begins = jnp.concatenate([jnp.zeros((E, 1), jnp.int32), ends[:, :-1]], axis=1)
    base = jnp.arange(E, dtype=jnp.int32)[:, None] * cap
    aligned = (base + begins) // 16 * 16
    rounds = jnp.maximum(1, jnp.max((base + ends - aligned + W - 1) // W, axis=0)).astype(jnp.int32)
    starts = aligned.T.reshape(-1)

    tok = lambda w: pl.BlockSpec((T, w), lambda i, *_: (i, 0))
    full = lambda a: pl.BlockSpec(a.shape, lambda i, *_: (0,) * a.ndim)
    consts = (lw['g_ple'], lw['w_pg'], lw['w_pp'], gfin)
    grid_spec = pltpu.PrefetchScalarGridSpec(
        num_scalar_prefetch=2,
        grid=(nt,),
        in_specs=[tok(D_MODEL), tok(PLE_DIM), pl.BlockSpec((E, T), lambda i, *_: (0, i)),
                  pl.BlockSpec(memory_space=pl.ANY)] + [full(a) for a in consts],
        out_specs=tok(D_MODEL),
        scratch_shapes=[pltpu.VMEM((2, E * W, D_MODEL), BF16), pltpu.SemaphoreType.DMA((2,))],
    )
    return pl.pallas_call(
        functools.partial(_combine_kernel, cap=cap, final=final),
        grid_spec=grid_spec,
        out_shape=jax.ShapeDtypeStruct((N, D_MODEL), F32),
        compiler_params=_cparams("arbitrary"),
        name="combine",
    )(starts, rounds, x1, p, rank, ye, *consts)


def _rope_tables(S):
    lane = np.arange(LANES)
    d = lane % HEAD_DIM
    t = jnp.arange(S)
    inv_a = ROPE_THETA ** (-jnp.arange(0, HEAD_DIM // 2, 2, dtype=F32) / (HEAD_DIM // 2))
    pos_a = jnp.where((d // 32 == 0)[None, :], (t // GRID_W)[:, None], (t % GRID_W)[:, None]).astype(F32)
    ang_a = pos_a * inv_a[d % 16][None, :]
    sign_a = jnp.where(d % 32 < 16, -1.0, 1.0)[None, :]
    inv_c = ROPE_THETA ** (-jnp.arange(0, HEAD_DIM, 2, dtype=F32) / HEAD_DIM)
    ang_c = t.astype(F32)[:, None] * inv_c[d % 32][None, :]
    sign_c = jnp.where(d < 32, -1.0, 1.0)[None, :]
    return (jnp.cos(ang_a), jnp.sin(ang_a) * sign_a, jnp.cos(ang_c), jnp.sin(ang_c) * sign_c)


def _layer_weights(i, w):
    w_in = w['w_in'][i]
    cols = lambda r: w_in[:, r[0]:r[1]]
    bf = lambda a: a.astype(BF16)
    row = lambda a: a.reshape(1, -1).astype(F32)
    wz = jnp.pad(cols(_ZL), ((0, 0), (0, LANES - (_ZL[1] - _ZL[0]))))
    w2 = w['gla_w2'][i]
    w2p = jnp.zeros((2, LANES, 256), F32).at[0, 0:16].set(w2[0]).at[1, 16:32].set(w2[1])
    wr = jnp.pad(w['w_router'][i], ((0, 0), (0, LANES - N_EXPERTS)))
    wr_hi = wr.astype(BF16)
    wbr = w['w_branch'][i]
    wbr = jnp.stack([wbr[0][_QPERM], wbr[1], wbr[2][_QPERM], wbr[3]])
    return dict(
        g_mix=row(w['norm_mix'][i]),
        wa=bf(jnp.concatenate([cols(_QA)[:, _QPERM], cols(_KA), cols(_VA)], axis=1)),
        wb=bf(cols(_UB)),
        wc=bf(jnp.concatenate([cols(_QC)[:, _QPERM], cols(_KC), cols(_VC)], axis=1)),
        wd=bf(cols(_DD)),
        wz=bf(wz),
        gq=row(jnp.tile(w['qk_norm'][i, 0], 4)),
        gk=row(jnp.tile(w['qk_norm'][i, 1], 2)),
        sink=w['sink_logit'][i][_HEAD_PERM].astype(F32),
        conv_w=w['conv_dw'][i].astype(F32),
        conv_b=row(w['conv_dw_b'][i]), ln_g=row(w['conv_ln_g'][i]), ln_b=row(w['conv_ln_b'][i]),
        w2=bf(w2p), b2=w['gla_b2'][i].reshape(2, 1, 256).astype(F32),
        gn=row(jnp.tile(w['gla_norm'][i], 4)),
        wg=bf(w_in[:, _GATES:]), wbr=bf(wbr), wout=bf(w['w_out'][i]),
        g_ffn=row(w['norm_ffn'][i]),
        wr_hi=wr_hi, wr_lo=(wr - wr_hi.astype(F32)).astype(BF16),
        we_g=w['w_gate_e'][i], we_u=w['w_up_e'][i], we_d=w['w_down_e'][i],
        g_ple=row(w['norm_ple'][i]), w_pg=bf(w['w_ple_gate'][i]), w_pp=bf(w['w_ple_proj'][i]),
    )


def _trunk(x3, p4, layers, gfin):
    B, S, _ = x3.shape
    N = B * S
    tabs = _rope_tables(S)
    x = x3.reshape(N, D_MODEL)
    cap = max(1, EC_CAPACITY * N // N_EXPERTS)
    for i, lw in enumerate(layers):
        qa, kat, va, zb, qc, kc, vc, d, zl = _proj(x, lw, tabs, S)
        b3 = lambda a: a.reshape(B, S, a.shape[-1])
        oa = _gattn(b3(qa), kat, b3(va))
        ob = _conv(b3(zb), lw)
        oc = _wattn(b3(qc), b3(kc), b3(vc), lw['sink'])
        ogf, ogb = _gla(b3(d), b3(zl), lw['w2'], lw['b2'])
        flat = lambda a: a.reshape(N, a.shape[-1])
        x1, h2, aff_t = _merge(x, flat(oa), flat(ob), flat(oc), flat(ogf), flat(ogb), d, lw)
        rank, incl = _select(aff_t, cap)
        idx, gate = _compact(rank, aff_t, cap)
        sub = D_MODEL // LANES
        h2_rows = h2.reshape(N // SUBLANES, SUBLANES, sub, LANES).transpose(0, 2, 1, 3).reshape(N * sub, LANES)
        piece = jnp.arange(sub, dtype=jnp.int32)[:, None]
        rows = ((idx // SUBLANES)[None, :] * sub + piece) * SUBLANES + (idx % SUBLANES)[None, :]
        xe = _gather_rows(h2_rows, rows.reshape(-1)).reshape(sub, -1, LANES)
        ye = _ffn(xe, gate.reshape(-1, 1), lw, cap)
        x = _combine(x1, p4[i].reshape(N, PLE_DIM), rank, incl, ye, lw, gfin, cap, i == len(layers) - 1)
    return x.reshape(B, S, D_MODEL)


def kernel(x_prompt, x_sample, p_prompt, p_sample, norm_mix, w_in, qk_norm, sink_logit, conv_dw, conv_dw_b,
           conv_ln_g, conv_ln_b, gla_w2, gla_b2, gla_norm, w_branch, w_out, norm_ffn, w_router, w_gate_e,
           w_up_e, w_down_e, norm_ple, w_ple_gate, w_ple_proj, norm_final):
    w = dict(norm_mix=norm_mix, w_in=w_in, qk_norm=qk_norm, sink_logit=sink_logit, conv_dw=conv_dw,
             conv_dw_b=conv_dw_b, conv_ln_g=conv_ln_g, conv_ln_b=conv_ln_b, gla_w2=gla_w2, gla_b2=gla_b2,
             gla_norm=gla_norm, w_branch=w_branch, w_out=w_out, norm_ffn=norm_ffn, w_router=w_router,
             w_gate_e=w_gate_e, w_up_e=w_up_e, w_down_e=w_down_e, norm_ple=norm_ple,
             w_ple_gate=w_ple_gate, w_ple_proj=w_ple_proj)
    layers = [_layer_weights(i, w) for i in range(norm_mix.shape[0])]
    gfin = norm_final.reshape(1, -1).astype(F32)
    return (_trunk(x_prompt, p_prompt, layers, gfin), _trunk(x_sample, p_sample, layers, gfin))
```

```python
import dataclasses
import functools

import jax
import jax.numpy as jnp
import numpy as np
from jax import lax
from jax.experimental import pallas as pl
from jax.experimental.pallas import tpu as pltpu
from jax.experimental.pallas import tpu_sc as plsc

F32 = jnp.float32
BF16 = jnp.bfloat16

D_MODEL = 1024
DEPTH = 4
GRID_W = 64
HEAD_DIM = 64
ROPE_THETA = 10000.0
NORM_EPS = 1e-6
CONV_CH = 256
CONV_WIDTH = 31
CONV_HALO = 16
WINDOW = 128
GLA_TAU = 16.0
GLA_CHUNK = 64
GLA_BLOCK = 256
GLA_ROWS = 2
N_EXPERTS = 16
EC_CAPACITY = 2
PLE_DIM = 256
LANES = 128
SUBLANES = 8
VMEM_LIMIT = 56 * 1024 * 1024
LOG2E = 1.4426950408889634
SEL_CHUNK = 512
COMB_WIN = 64
SC_LANES = 16
GATHER_WINDOW = 128

_QA, _KA, _VA = (0, 256), (256, 384), (384, 512)
_UB = (512, 1024)
_QC, _KC, _VC = (1024, 1280), (1280, 1408), (1408, 1536)
_DD = (1536, 2560)
_ZL = (2560, 2592)
_GATES = 2592
_QPERM = np.concatenate([np.arange(0, 64), np.arange(128, 192), np.arange(64, 128), np.arange(192, 256)])
_HEAD_PERM = np.array([0, 2, 1, 3])


def _cparams(*sem):
    return pltpu.CompilerParams(dimension_semantics=sem, vmem_limit_bytes=VMEM_LIMIT)


def _dot(a, b):
    return jnp.dot(a, b, preferred_element_type=F32)


def _dot_nt(a, b):
    return lax.dot_general(a, b, (((1,), (1,)), ((), ())), preferred_element_type=F32)


def _rms(x, g):
    return x * lax.rsqrt(jnp.mean(x * x, axis=-1, keepdims=True) + NORM_EPS) * g


def _split2(x):
    hi = x.astype(BF16)
    lo = (x - hi.astype(F32)).astype(BF16)
    return hi, lo


def _group_ones(width, group):
    r = lax.broadcasted_iota(jnp.int32, (width, width), 0) // group
    c = lax.broadcasted_iota(jnp.int32, (width, width), 1) // group
    return jnp.where(r == c, 1.0, 0.0).astype(BF16)


def _group_meansq(x, group):
    hi, lo = _split2(x * x)
    ones = _group_ones(x.shape[1], group)
    return (_dot(hi, ones) + _dot(lo, ones)) * (1.0 / group)


def _rope(x, cos, sin_signed, half):
    width = x.shape[1]
    lane = lax.broadcasted_iota(jnp.int32, x.shape, 1)
    from_lo = pltpu.roll(x, half, 1)
    from_hi = pltpu.roll(x, width - half, 1)
    partner = jnp.where((lane & half) != 0, from_lo, from_hi)
    return x * cos + partner * sin_signed


def _proj_kernel(x_ref, g_ref, wa_ref, wb_ref, wc_ref, wd_ref, wz_ref, gq_ref, gk_ref,
                 ca_ref, sa_ref, cc_ref, sc_ref,
                 qa_ref, kat_ref, va_ref, zb_ref, qc_ref, kc_ref, vc_ref, d_ref, zl_ref):
    h = _rms(x_ref[...], g_ref[...]).astype(BF16)
    scale = HEAD_DIM ** -0.5

    ua = _dot(h, wa_ref[...])
    ca, sa = ca_ref[...], sa_ref[...]
    q = ua[:, :256]
    q = q * lax.rsqrt(_group_meansq(q, HEAD_DIM) + NORM_EPS) * gq_ref[...]
    q = _rope(q, jnp.concatenate([ca, ca], axis=1), jnp.concatenate([sa, sa], axis=1), 16)
    qa_ref[...] = (q * (scale * LOG2E)).astype(BF16)
    k = ua[:, 256:384]
    k = k * lax.rsqrt(_group_meansq(k, HEAD_DIM) + NORM_EPS) * gk_ref[...]
    kat_ref[...] = _rope(k, ca, sa, 16).T.astype(BF16)
    va_ref[...] = ua[:, 384:].astype(BF16)

    ub = _dot(h, wb_ref[...])
    zb_ref[...] = (ub[:, :CONV_CH] * jax.nn.sigmoid(ub[:, CONV_CH:])).astype(BF16)

    uc = _dot(h, wc_ref[...])
    cc, sc = cc_ref[...], sc_ref[...]
    qc = _rope(uc[:, :256], jnp.concatenate([cc, cc], axis=1), jnp.concatenate([sc, sc], axis=1), 32)
    qc_ref[...] = (qc * (scale * LOG2E)).astype(BF16)
    kc_ref[...] = _rope(uc[:, 256:384], cc, sc, 32).astype(BF16)
    vc_ref[...] = uc[:, 384:].astype(BF16)

    ud = _dot(h, wd_ref[...])
    d_ref[:, :256] = (ud[:, :256] * scale).astype(BF16)
    d_ref[:, 256:] = ud[:, 256:].astype(BF16)
    zl_ref[...] = _dot(h, wz_ref[...]).astype(BF16)


def _proj(x, lw, tabs, S):
    N = x.shape[0]
    T = min(512, S)
    per_row = S // T
    tok = lambda w: pl.BlockSpec((T, w), lambda i: (i, 0))
    full = lambda a: pl.BlockSpec(a.shape, lambda i: (0,) * a.ndim)
    tab = pl.BlockSpec((T, LANES), lambda i: (i % per_row, 0))
    widths = (256, 128, 128, 256, 256, 128, 128, 1024, 128)
    consts = (lw['g_mix'], lw['wa'], lw['wb'], lw['wc'], lw['wd'], lw['wz'], lw['gq'], lw['gk'])
    return pl.pallas_call(
        _proj_kernel,
        grid=(N // T,),
        in_specs=[tok(D_MODEL)] + [full(a) for a in consts] + [tab] * 4,
        out_specs=[tok(256), pl.BlockSpec((LANES, T), lambda i: (0, i))] + [tok(w) for w in widths[2:]],
        out_shape=[jax.ShapeDtypeStruct((N, 256), BF16), jax.ShapeDtypeStruct((LANES, N), BF16)]
                  + [jax.ShapeDtypeStruct((N, w), BF16) for w in widths[2:]],
        compiler_params=_cparams("parallel"),
        name="proj",
    )(x, *consts, *tabs)


def _stack_heads(qb):
    lane = lax.broadcasted_iota(jnp.int32, qb.shape, 1)
    zero = jnp.zeros_like(qb)
    return jnp.concatenate([jnp.where(lane < HEAD_DIM, qb, zero), jnp.where(lane < HEAD_DIM, zero, qb)], axis=0)


def _unstack_heads(o, T):
    lane = lax.broadcasted_iota(jnp.int32, (T, LANES), 1)
    return jnp.where(lane < HEAD_DIM, o[:T], o[T:])


def _gattn_kernel(q_ref, kt_ref, v_ref, o_ref, *, tk):
    T = q_ref.shape[0]
    S = kt_ref.shape[1]
    lane_v = lax.broadcasted_iota(jnp.int32, (tk, LANES), 1)
    one = jnp.ones((tk, LANES), BF16)
    qs = [_stack_heads(q_ref[:, j * LANES:(j + 1) * LANES]) for j in range(2)]
    m = [jnp.full((2 * T, 1), -jnp.inf, F32) for _ in range(2)]
    acc = [jnp.zeros((2 * T, LANES), F32) for _ in range(2)]
    for c in range(S // tk):
        kt = kt_ref[:, c * tk:(c + 1) * tk]
        v = v_ref[c * tk:(c + 1) * tk, :]
        va = jnp.where(lane_v < HEAD_DIM, v, one)
        vb = jnp.where(lane_v < HEAD_DIM, one, v)
        for j in range(2):
            s = _dot(qs[j], kt)
            m_new = jnp.maximum(m[j], jnp.max(s, axis=1, keepdims=True))
            alpha = jnp.exp2(m[j] - m_new)
            p = jnp.exp2(s - m_new).astype(BF16)
            pv = jnp.concatenate([_dot(p[:T], va), _dot(p[T:], vb)], axis=0)
            acc[j] = alpha * acc[j] + pv
            m[j] = m_new
    lane_o = lax.broadcasted_iota(jnp.int32, (T, LANES), 1)
    for j in range(2):
        a, b = acc[j][:T], acc[j][T:]
        o = jnp.where(lane_o < HEAD_DIM, a / pltpu.roll(a, HEAD_DIM, 1), b / pltpu.roll(b, HEAD_DIM, 1))
        o_ref[:, j * LANES:(j + 1) * LANES] = o.astype(BF16)


def _gattn(q, kt, v):
    B, S, _ = q.shape
    T = min(512, S)
    tk = min(1024, S)
    return pl.pallas_call(
        functools.partial(_gattn_kernel, tk=tk),
        grid=(B, S // T),
        in_specs=[pl.BlockSpec((None, T, 256), lambda b, i: (b, i, 0)),
                  pl.BlockSpec((LANES, S), lambda b, i: (0, b)),
                  pl.BlockSpec((None, S, LANES), lambda b, i: (b, 0, 0))],
        out_specs=pl.BlockSpec((None, T, 256), lambda b, i: (b, i, 0)),
        out_shape=jax.ShapeDtypeStruct((B, S, 256), BF16),
        compiler_params=_cparams("parallel", "parallel"),
        name="gattn",
    )(q, kt, v)


def _wattn_kernel(sink_ref, q_ref, k_ref, v_ref, bias_ref, o_ref, *, kw):
    T = q_ref.shape[0]
    S = k_ref.shape[0]
    i = pl.program_id(1)
    start = pl.multiple_of(jnp.clip(i * T - WINDOW, 0, S - kw), WINDOW)
    kwin = k_ref[pl.ds(start, kw), :]
    vwin = v_ref[pl.ds(start, kw), :]
    band = bias_ref[(i * T - start) // WINDOW]
    bias = jnp.concatenate([band, band], axis=0)
    lane_v = lax.broadcasted_iota(jnp.int32, (kw, LANES), 1)
    one = jnp.ones((kw, LANES), BF16)
    va = jnp.where(lane_v < HEAD_DIM, vwin, one)
    vb = jnp.where(lane_v < HEAD_DIM, one, vwin)
    first = lax.broadcasted_iota(jnp.int32, (2 * T, 1), 0) < T
    lane_o = lax.broadcasted_iota(jnp.int32, (T, LANES), 1)
    for j in range(2):
        qst = _stack_heads(q_ref[:, j * LANES:(j + 1) * LANES])
        s = _dot_nt(qst, kwin) + bias
        sk = jnp.where(first, sink_ref[2 * j], sink_ref[2 * j + 1])
        m = jnp.maximum(jnp.max(s, axis=1, keepdims=True), sk)
        p = jnp.exp2(s - m).astype(BF16)
        sunk = jnp.exp2(sk - m)
        a = _dot(p[:T], va)
        b = _dot(p[T:], vb)
        o = jnp.where(lane_o < HEAD_DIM, a / (pltpu.roll(a, HEAD_DIM, 1) + sunk[:T]),
                      b / (pltpu.roll(b, HEAD_DIM, 1) + sunk[T:]))
        o_ref[:, j * LANES:(j + 1) * LANES] = o.astype(BF16)


def _band_bias(T, kw):
    r = np.arange(T)[None, :, None]
    c = np.arange(kw)[None, None, :]
    off = np.arange(3)[:, None, None] * WINDOW
    return jnp.asarray(np.where(np.abs(c - off - r) <= WINDOW, 0.0, -np.inf), F32)


def _wattn(q, k, v, sink):
    B, S, _ = q.shape
    T = min(256, S - 2 * WINDOW) if S > 2 * WINDOW else S
    kw = min(T + 2 * WINDOW, S)
    bias = _band_bias(T, kw)
    return pl.pallas_call(
        functools.partial(_wattn_kernel, kw=kw),
        grid=(B, S // T),
        in_specs=[pl.BlockSpec(memory_space=pltpu.SMEM),
                  pl.BlockSpec((None, T, 256), lambda b, i: (b, i, 0)),
                  pl.BlockSpec((None, S, LANES), lambda b, i: (b, 0, 0)),
                  pl.BlockSpec((None, S, LANES), lambda b, i: (b, 0, 0)),
                  pl.BlockSpec(bias.shape, lambda b, i: (0, 0, 0))],
        out_specs=pl.BlockSpec((None, T, 256), lambda b, i: (b, i, 0)),
        out_shape=jax.ShapeDtypeStruct((B, S, 256), BF16),
        compiler_params=_cparams("parallel", "parallel"),
        name="wattn",
    )(sink, q, k, v, bias)


def _conv_kernel(zp_ref, zc_ref, zn_ref, w_ref, b_ref, g_ref, beta_ref, o_ref, buf_ref, sh_ref):
    T = zc_ref.shape[0]
    i = pl.program_id(1)
    H = CONV_HALO
    keep_prev = jnp.where(i > 0, 1.0, 0.0)
    keep_next = jnp.where(i < pl.num_programs(1) - 1, 1.0, 0.0)
    buf_ref[0:H, :] = zp_ref[...].astype(F32) * keep_prev
    buf_ref[H:H + T, :] = zc_ref[...].astype(F32)
    buf_ref[H + T:, :] = zn_ref[...].astype(F32) * keep_next
    span = sh_ref.shape[1]
    for r in range(1, SUBLANES):
        sh_ref[r] = buf_ref[r:r + span, :]
    acc = jnp.zeros((T, CONV_CH), F32)
    off = H - CONV_WIDTH // 2
    for tap in range(CONV_WIDTH):
        r = (off + tap) % SUBLANES
        a = off + tap - r
        rows = buf_ref[a:a + T, :] if r == 0 else sh_ref[r, a:a + T, :]
        acc = acc + rows * w_ref[tap:tap + 1, :]
    z = acc + b_ref[...]
    mu = jnp.mean(z, axis=-1, keepdims=True)
    zc = z - mu
    var = jnp.mean(zc * zc, axis=-1, keepdims=True)
    y = zc * lax.rsqrt(var + NORM_EPS) * g_ref[...] + beta_ref[...]
    o_ref[...] = (y * jax.nn.sigmoid(y)).astype(BF16)


def _conv(z, lw):
    B, S, _ = z.shape
    T = min(512, S)
    H = CONV_HALO
    per = T // H
    last = S // H - 1
    full = lambda a: pl.BlockSpec(a.shape, lambda b, i: (0,) * a.ndim)
    consts = (lw['conv_w'], lw['conv_b'], lw['ln_g'], lw['ln_b'])
    return pl.pallas_call(
        _conv_kernel,
        grid=(B, S // T),
        in_specs=[pl.BlockSpec((None, H, CONV_CH), lambda b, i: (b, jnp.maximum(i * per - 1, 0), 0)),
                  pl.BlockSpec((None, T, CONV_CH), lambda b, i: (b, i, 0)),
                  pl.BlockSpec((None, H, CONV_CH), lambda b, i: (b, jnp.minimum((i + 1) * per, last), 0))]
                 + [full(a) for a in consts],
        out_specs=pl.BlockSpec((None, T, CONV_CH), lambda b, i: (b, i, 0)),
        out_shape=jax.ShapeDtypeStruct((B, S, CONV_CH), BF16),
        scratch_shapes=[pltpu.VMEM((T + 2 * H, CONV_CH), F32),
                        pltpu.VMEM((SUBLANES, T + 2 * H - SUBLANES, CONV_CH), F32)],
        compiler_params=_cparams("parallel", "parallel"),
        name="conv",
    )(z, z, z, *consts)


def _gla_block(d_ref, zl_ref, w2, b2, tri, att_mask, st_ref, reverse):
    TB = d_ref.shape[0]
    L = GLA_CHUNK
    nch = TB // L

    pre = _dot(zl_ref[...], w2) + b2
    la = (jnp.minimum(pre, 0.0) - jnp.log(1.0 + jnp.exp(-jnp.abs(pre)))) * (1.0 / GLA_TAU)

    hi = la.astype(BF16)
    r1 = la - hi.astype(F32)
    mid = r1.astype(BF16)
    lo = (r1 - mid.astype(F32)).astype(BF16)
    cums = _dot(tri, jnp.concatenate([hi, mid, lo], axis=1))
    b = cums[:, :256] + cums[:, 256:512] + cums[:, 512:]
    mid_row = L // 2 if reverse else L // 2 - 1
    last_row = 0 if reverse else L - 1
    per_chunk = lambda row: jnp.concatenate(
        [jnp.broadcast_to(b[ci * L + row:ci * L + row + 1], (L, 256)) for ci in range(nch)], axis=0)
    bmid, blast = per_chunk(mid_row), per_chunk(last_row)

    q = d_ref[:, 0:256].astype(F32)
    k = d_ref[:, 256:512].astype(F32)
    v = d_ref[:, 512:768]
    qt = (q * jnp.exp(b - bmid)).astype(BF16)
    kt = (k * jnp.exp(bmid - b)).astype(BF16)
    qe = (q * jnp.exp(b)).astype(BF16)
    kl = (k * jnp.exp(blast - b)).astype(BF16)
    dec = jnp.exp(blast)

    rr = lax.broadcasted_iota(jnp.int32, (LANES, LANES), 0) // HEAD_DIM
    cc = lax.broadcasted_iota(jnp.int32, (LANES, LANES), 1) // HEAD_DIM
    head_diag = rr == cc
    rowid = lax.broadcasted_iota(jnp.int32, (TB, LANES), 0) // L
    order = range(nch - 1, -1, -1) if reverse else range(nch)

    out = []
    for p in range(2):
        ls = slice(p * LANES, (p + 1) * LANES)
        vb = v[:, ls]
        att = _dot_nt(_stack_heads(qt[:, ls]), kt[:, ls]) * att_mask
        o_intra = _unstack_heads(_dot(att.astype(BF16), vb), TB)
        vt = vb.astype(F32).T.astype(BF16)
        klb = kl[:, ls]
        kv_t = [jnp.where(head_diag, _dot(vt, jnp.where(rowid == ci, klb, jnp.zeros_like(klb))), 0.0)
                for ci in range(nch)]
        st = st_ref[p]
        o_inter = [None] * nch
        for ci in order:
            o_inter[ci] = _dot_nt(qe[ci * L:(ci + 1) * L, ls], st.astype(BF16))
            st = st * dec[ci * L:ci * L + 1, ls] + kv_t[ci]
        st_ref[p] = st
        out.append(o_intra + jnp.concatenate(o_inter, axis=0))
    return out


def _gla_kernel(df_ref, zf_ref, db_ref, zb_ref, w2_ref, b2_ref, tri_ref, mask_ref, of_ref, ob_ref, st_ref):
    @pl.when(pl.program_id(1) == 0)
    def _():
        st_ref[...] = jnp.zeros(st_ref.shape, F32)

    for r in range(df_ref.shape[0]):
        fwd = _gla_block(df_ref.at[r], zf_ref.at[r], w2_ref[0], b2_ref[0], tri_ref[0], mask_ref[0],
                         st_ref.at[r, 0], False)
        bwd = _gla_block(db_ref.at[r], zb_ref.at[r], w2_ref[1], b2_ref[1], tri_ref[1], mask_ref[1],
                         st_ref.at[r, 1], True)
        for p in range(2):
            of_ref[r, :, p * LANES:(p + 1) * LANES] = fwd[p]
            ob_ref[r, :, p * LANES:(p + 1) * LANES] = bwd[p]


def _gla_masks(TB):
    L = GLA_CHUNK
    r = np.arange(TB)[:, None]
    c = np.arange(TB)[None, :]
    same = (r // L) == (c // L)
    tri = np.stack([same & (c <= r), same & (c >= r)]).astype(np.float32)
    return jnp.asarray(tri, BF16), jnp.asarray(np.concatenate([tri, tri], axis=1), F32)


def _gla(d, zl, w2, b2):
    B, S, _ = d.shape
    TB = min(GLA_BLOCK, S)
    nb = S // TB
    R = GLA_ROWS if B % GLA_ROWS == 0 else 1
    fblk = lambda b, i: (b, i, 0)
    bblk = lambda b, i: (b, nb - 1 - i, 0)
    full = lambda a: pl.BlockSpec(a.shape, lambda b, i: (0,) * a.ndim)
    tri, mask = _gla_masks(TB)
    out = jax.ShapeDtypeStruct((B, S, 256), F32)
    return pl.pallas_call(
        _gla_kernel,
        grid=(B // R, nb),
        in_specs=[pl.BlockSpec((R, TB, 1024), fblk), pl.BlockSpec((R, TB, LANES), fblk),
                  pl.BlockSpec((R, TB, 1024), bblk), pl.BlockSpec((R, TB, LANES), bblk),
                  full(w2), full(b2), full(tri), full(mask)],
        out_specs=[pl.BlockSpec((R, TB, 256), fblk), pl.BlockSpec((R, TB, 256), bblk)],
        out_shape=[out, out],
        scratch_shapes=[pltpu.VMEM((R, 2, 2, LANES, LANES), F32)],
        compiler_params=_cparams("parallel", "arbitrary"),
        name="gla",
    )(d, zl, d, zl, w2, b2, tri, mask)


def _merge_kernel(x_ref, oa_ref, ob_ref, oc_ref, ogf_ref, ogb_ref, rd_ref, gmix_ref, wg_ref, wbr_ref, wout_ref,
                  gn_ref, gffn_ref, wrh_ref, wrl_ref, x1_ref, h2_ref, aff_ref):
    x = x_ref[...]
    h = _rms(x, gmix_ref[...]).astype(BF16)
    og = ogf_ref[...] + ogb_ref[...]
    od = og * lax.rsqrt(_group_meansq(og, HEAD_DIM) + NORM_EPS) * gn_ref[...]
    rd = rd_ref[...].astype(F32)
    od = (od * (rd * jax.nn.sigmoid(rd))).astype(BF16)
    branches = (oa_ref[...], ob_ref[...], oc_ref[...], od)
    merged = jnp.zeros(x.shape, F32)
    for i, o in enumerate(branches):
        gate = jax.nn.sigmoid(_dot(h, wg_ref[:, i * D_MODEL:(i + 1) * D_MODEL]))
        merged = merged + gate * _dot(o, wbr_ref[i])
    x1 = x + _dot(merged.astype(BF16), wout_ref[...])
    x1_ref[...] = x1
    h2 = _rms(x1, gffn_ref[...])
    h2_ref[...] = h2
    hh, hl = _split2(h2)
    logits = _dot(hh, wrh_ref[...]) + _dot(hh, wrl_ref[...]) + _dot(hl, wrh_ref[...])
    lt = logits.T[:N_EXPERTS]
    e = jnp.exp(lt - jnp.max(lt, axis=0, keepdims=True))
    aff_ref[...] = e / jnp.sum(e, axis=0, keepdims=True)


def _merge(x, oa, ob, oc, ogf, ogb, d, lw):
    N = x.shape[0]
    T = min(256, N)
    tok = lambda w: pl.BlockSpec((T, w), lambda i: (i, 0))
    full = lambda a: pl.BlockSpec(a.shape, lambda i: (0,) * a.ndim)
    consts = (lw['g_mix'], lw['wg'], lw['wbr'], lw['wout'], lw['gn'], lw['g_ffn'], lw['wr_hi'], lw['wr_lo'])
    return pl.pallas_call(
        _merge_kernel,
        grid=(N // T,),
        in_specs=[tok(D_MODEL), tok(256), tok(256), tok(256), tok(256), tok(256),
                  pl.BlockSpec((T, 256), lambda i: (i, 3))] + [full(a) for a in consts],
        out_specs=[tok(D_MODEL), tok(D_MODEL), pl.BlockSpec((N_EXPERTS, T), lambda i: (0, i))],
        out_shape=[jax.ShapeDtypeStruct((N, D_MODEL), F32), jax.ShapeDtypeStruct((N, D_MODEL), F32),
                   jax.ShapeDtypeStruct((N_EXPERTS, N), F32)],
        compiler_params=_cparams("parallel"),
        name="merge",
    )(x, oa, ob, oc, ogf, ogb, d, *consts)


def _select_kernel(aff_ref, rank_ref, incl_ref, *, cap):
    E, N = aff_ref.shape
    CH = min(N, 4096)
    SC = min(N, SEL_CHUNK)

    def count(pred):
        def body(c, acc):
            start = pl.multiple_of(c * CH, CH)
            bits = lax.bitcast_convert_type(aff_ref[:, pl.ds(start, CH)], jnp.int32)
            tok = start + lax.broadcasted_iota(jnp.int32, (E, CH), 1)
            return acc + jnp.sum(pred(bits, tok), axis=1, keepdims=True)
        return lax.fori_loop(0, N // CH, body, jnp.zeros((E, 1), F32))

    def value_step(_, lohi):
        lo, hi = lohi
        mid = lo + ((hi - lo) >> 1)
        ok = count(lambda b, t: jnp.where(b >= mid, 1.0, 0.0)) >= cap
        return jnp.where(ok, mid, lo), jnp.where(ok, hi, mid)

    thr, _ = lax.fori_loop(0, 32, value_step,
                           (jnp.zeros((E, 1), jnp.int32), jnp.full((E, 1), 0x7F800000, jnp.int32)))
    need = cap - count(lambda b, t: jnp.where(b > thr, 1.0, 0.0))

    def tie_step(_, lohi):
        lo, hi = lohi
        mid = lo + ((hi - lo) >> 1)
        ok = count(lambda b, t: jnp.where(b == thr, jnp.where(t <= mid, 1.0, 0.0), 0.0)) >= need
        return jnp.where(ok, lo, mid), jnp.where(ok, mid, hi)

    _, cut = lax.fori_loop(0, int(np.ceil(np.log2(N))) + 1, tie_step,
                           (jnp.full((E, 1), -1, jnp.int32), jnp.full((E, 1), N - 1, jnp.int32)))

    r = lax.broadcasted_iota(jnp.int32, (SC, SC), 0)
    c = lax.broadcasted_iota(jnp.int32, (SC, SC), 1)
    tri = jnp.where(r <= c, 1.0, 0.0).astype(BF16)

    def emit(ci, carry):
        start = pl.multiple_of(ci * SC, SC)
        bits = lax.bitcast_convert_type(aff_ref[:, pl.ds(start, SC)], jnp.int32)
        tok = start + lax.broadcasted_iota(jnp.int32, (E, SC), 1)
        picked = jnp.where(bits > thr, 1.0, jnp.where(bits == thr, jnp.where(tok <= cut, 1.0, 0.0), 0.0))
        inc = carry + _dot(picked.astype(BF16), tri)
        incl_ref[:, pl.ds(start, SC)] = inc
        rank_ref[:, pl.ds(start, SC)] = jnp.where(picked > 0.0, inc - 1.0, -1.0).astype(jnp.int32)
        return inc[:, SC - 1:SC]

    lax.fori_loop(0, N // SC, emit, jnp.zeros((E, 1), F32))


def _select(aff_t, cap):
    E, N = aff_t.shape
    full = pl.BlockSpec((E, N), lambda i: (0, 0))
    return pl.pallas_call(
        functools.partial(_select_kernel, cap=cap),
        grid=(1,),
        in_specs=[full],
        out_specs=[full, full],
        out_shape=[jax.ShapeDtypeStruct((E, N), jnp.int32), jax.ShapeDtypeStruct((E, N), F32)],
        compiler_params=_cparams("arbitrary"),
        name="select",
    )(aff_t)


def _sc_params():
    cp = pltpu.CompilerParams()
    if "needs_layout_passes" in pltpu.CompilerParams.__dataclass_fields__:
        cp = dataclasses.replace(cp, needs_layout_passes=False)
    return cp


def _compact(rank, aff_t, cap):
    E, N = rank.shape
    CH = min(N, 4096)
    mesh = plsc.VectorSubcoreMesh(core_axis_name="c", subcore_axis_name="s")

    @pl.kernel(out_type=(jax.ShapeDtypeStruct((E * cap,), jnp.int32), jax.ShapeDtypeStruct((E * cap,), F32)),
               mesh=mesh,
               scratch_types=[pltpu.VMEM((CH,), jnp.int32), pltpu.VMEM((CH,), F32),
                              pltpu.VMEM((cap,), jnp.int32), pltpu.VMEM((cap,), F32)],
               compiler_params=_sc_params())
    def compact(rank_hbm, aff_hbm, idx_hbm, gate_hbm, rbuf, abuf, ibuf, gbuf):
        wid = lax.axis_index("s") * mesh.num_cores + lax.axis_index("c")

        @pl.when(wid < E)
        def _():
            @pl.loop(0, N // CH)
            def _(c):
                base = wid * N + c * CH
                pltpu.sync_copy(rank_hbm.at[pl.ds(base, CH)], rbuf)
                pltpu.sync_copy(aff_hbm.at[pl.ds(base, CH)], abuf)

                @pl.loop(0, CH, step=SC_LANES)
                def _(i):
                    r = rbuf[pl.ds(i, SC_LANES)]
                    picked = r >= 0
                    slot = jnp.where(picked, r, 0)
                    tok = c * CH + i + lax.iota(jnp.int32, SC_LANES)
                    plsc.store_scatter(ibuf, [slot], tok, mask=picked)
                    plsc.store_scatter(gbuf, [slot], abuf[pl.ds(i, SC_LANES)], mask=picked)

            pltpu.sync_copy(ibuf, idx_hbm.at[pl.ds(wid * cap, cap)])
            pltpu.sync_copy(gbuf, gate_hbm.at[pl.ds(wid * cap, cap)])

    return compact(rank.reshape(E * N), aff_t.reshape(E * N))


def _gather_rows(x, idx):
    R = idx.shape[0]
    win = GATHER_WINDOW
    mesh = plsc.VectorSubcoreMesh(core_axis_name="c", subcore_axis_name="s")

    @pl.kernel(out_type=jax.ShapeDtypeStruct((R, LANES), x.dtype), mesh=mesh)
    def gather(x_hbm, i_hbm, o_hbm):
        def body(i_vmem, o_vmem):
            pltpu.sync_copy(x_hbm.at[i_vmem.at[0]], o_vmem)

        pltpu.emit_pipeline(
            body,
            grid=(R // win,),
            in_specs=[pl.BlockSpec((1, win), lambda i: (0, i))],
            out_specs=[pl.BlockSpec((win, LANES), lambda i: (i, 0))],
            core_axis_name=("c", "s"),
            dimension_semantics=(pltpu.PARALLEL,),
            trace_scopes=False,
        )(i_hbm, o_hbm)

    return gather(x, idx.reshape(1, R))


def _ffn_kernel(x_ref, wg_ref, wu_ref, wd_ref, gate_ref, y_ref, w_ref):
    @pl.when(pl.program_id(1) == 0)
    def _():
        w_ref[0] = wg_ref[...].astype(BF16)
        w_ref[1] = wu_ref[...].astype(BF16)
        w_ref[2] = wd_ref[...].astype(BF16)

    x = jnp.concatenate([x_ref[k] for k in range(D_MODEL // LANES)], axis=1).astype(BF16)
    g = _dot(x, w_ref[0])
    hid = (g * jax.nn.sigmoid(g)) * _dot(x, w_ref[1])
    y_ref[...] = (_dot(hid.astype(BF16), w_ref[2]) * gate_ref[...]).astype(BF16)


def _ffn(xe, gate, lw, cap):
    E = N_EXPERTS
    M = min(512, cap)
    per = cap // M
    layer = lw['layer']
    wspec = pl.BlockSpec((None, None, D_MODEL, D_MODEL), lambda e, s: (layer, e, 0, 0))
    return pl.pallas_call(
        _ffn_kernel,
        grid=(E, per),
        in_specs=[pl.BlockSpec((D_MODEL // LANES, M, LANES), lambda e, s: (0, e * per + s, 0)), wspec, wspec, wspec,
                  pl.BlockSpec((M, 1), lambda e, s: (e * per + s, 0))],
        out_specs=pl.BlockSpec((M, D_MODEL), lambda e, s: (e * per + s, 0)),
        out_shape=jax.ShapeDtypeStruct((E * cap, D_MODEL), BF16),
        scratch_shapes=[pltpu.VMEM((3, D_MODEL, D_MODEL), BF16)],
        compiler_params=_cparams("parallel", "arbitrary"),
        name="ffn",
    )(xe, lw['we_g'], lw['we_u'], lw['we_d'], gate)


def _combine_kernel(starts_ref, rounds_ref, x_ref, p_ref, rank_ref, ye_hbm, g_ref, wgate_ref, wproj_ref,
                    gfin_ref, o_ref, stage_ref, sem_ref, *, cap, final):
    j = pl.program_id(0)
    nj = pl.num_programs(0)
    T = x_ref.shape[0]
    E, W = N_EXPERTS, COMB_WIN
    slot = j % 2

    def window_start(tile, rnd, e):
        return pl.multiple_of(jnp.minimum(starts_ref[tile * E + e] + rnd * W, E * cap - W), 16)

    def window_copy(tile, rnd, e, sl):
        return pltpu.make_async_copy(ye_hbm.at[pl.ds(window_start(tile, rnd, e), W), :],
                                     stage_ref.at[sl, pl.ds(e * W, W), :], sem_ref.at[sl])

    def start_all(tile, rnd, sl):
        for e in range(E):
            window_copy(tile, rnd, e, sl).start()

    def wait_all(tile, rnd, sl):
        for e in range(E):
            window_copy(tile, rnd, e, sl).wait()

    @pl.when(j == 0)
    def _():
        start_all(0, 0, 0)

    @pl.when(j + 1 < nj)
    def _():
        start_all(j + 1, 0, 1 - slot)

    rk = rank_ref[...]
    eoff = lax.broadcasted_iota(jnp.int32, (E, T), 0) * cap
    tgt = jnp.where(rk >= 0, rk + eoff, -1).astype(F32)
    tgt_t = jnp.concatenate([tgt, jnp.full((LANES - E, T), -1.0, F32)], axis=0).T
    per = LANES // W
    lane = lax.broadcasted_iota(jnp.int32, (T, LANES), 1)
    lane_e = lane // W
    lane_r = (lane % W).astype(F32)

    def placed(rnd):
        onehot = []
        for blk in range(E // per):
            rows = jnp.zeros((T, LANES), F32)
            tgt_b = jnp.zeros((T, LANES), F32)
            for u in range(per):
                e = blk * per + u
                mine = lane_e == u
                staged = window_start(j, rnd, e).astype(F32) + lane_r
                staged = jnp.where(staged >= (starts_ref[j * E + e] + rnd * W).astype(F32), staged, -2.0)
                rows = jnp.where(mine, staged, rows)
                tgt_b = jnp.where(mine, tgt_t[:, e:e + 1], tgt_b)
            onehot.append(jnp.where(tgt_b == rows, 1.0, 0.0).astype(BF16))
        return _dot(jnp.concatenate(onehot, axis=1), stage_ref[slot])

    wait_all(j, 0, slot)
    moe = placed(0)

    def extra(rnd, acc):
        start_all(j, rnd, slot)
        wait_all(j, rnd, slot)
        return acc + placed(rnd)

    moe = lax.fori_loop(1, rounds_ref[j], extra, moe)

    x = x_ref[...] + moe
    h = _rms(x, g_ref[...]).astype(BF16)
    gate = jax.nn.sigmoid(_dot(h, wgate_ref[...]))
    y = x + gate * _dot(p_ref[...].astype(BF16), wproj_ref[...])
    if final:
        y = _rms(y, gfin_ref[...])
    o_ref[...] = y


def _combine(x1, p, rank, incl, ye, lw, gfin, cap, final):
    N = x1.shape[0]
    E, W = N_EXPERTS, COMB_WIN
    T = min(256, N)
    nt = N // T
    ends = incl[:, T - 1::T].astype(jnp.int32)
    begins = jnp.concatenate([jnp.zeros((E, 1), jnp.int32), ends[:, :-1]], axis=1)
    base = jnp.arange(E, dtype=jnp.int32)[:, None] * cap
    aligned = (base + begins) // 16 * 16
    rounds = jnp.maximum(1, jnp.max((base + ends - aligned + W - 1) // W, axis=0)).astype(jnp.int32)
    starts = aligned.T.reshape(-1)

    tok = lambda w: pl.BlockSpec((T, w), lambda i, *_: (i, 0))
    full = lambda a: pl.BlockSpec(a.shape, lambda i, *_: (0,) * a.ndim)
    consts = (lw['g_ple'], lw['w_pg'], lw['w_pp'], gfin)
    grid_spec = pltpu.PrefetchScalarGridSpec(
        num_scalar_prefetch=2,
        grid=(nt,),
        in_specs=[tok(D_MODEL), tok(PLE_DIM), pl.BlockSpec((E, T), lambda i, *_: (0, i)),
                  pl.BlockSpec(memory_space=pl.ANY)] + [full(a) for a in consts],
        out_specs=tok(D_MODEL),
        scratch_shapes=[pltpu.VMEM((2, E * W, D_MODEL), BF16), pltpu.SemaphoreType.DMA((2,))],
    )
    return pl.pallas_call(
        functools.partial(_combine_kernel, cap=cap, final=final),
        grid_spec=grid_spec,
        out_shape=jax.ShapeDtypeStruct((N, D_MODEL), F32),
        compiler_params=_cparams("arbitrary"),
        name="combine",
    )(starts, rounds, x1, p, rank, ye, *consts)


def _rope_tables(S):
    lane = np.arange(LANES)
    d = lane % HEAD_DIM
    t = jnp.arange(S)
    inv_a = ROPE_THETA ** (-jnp.arange(0, HEAD_DIM // 2, 2, dtype=F32) / (HEAD_DIM // 2))
    pos_a = jnp.where((d // 32 == 0)[None, :], (t // GRID_W)[:, None], (t % GRID_W)[:, None]).astype(F32)
    ang_a = pos_a * inv_a[d % 16][None, :]
    sign_a = jnp.where(d % 32 < 16, -1.0, 1.0)[None, :]
    inv_c = ROPE_THETA ** (-jnp.arange(0, HEAD_DIM, 2, dtype=F32) / HEAD_DIM)
    ang_c = t.astype(F32)[:, None] * inv_c[d % 32][None, :]
    sign_c = jnp.where(d < 32, -1.0, 1.0)[None, :]
    return (jnp.cos(ang_a), jnp.sin(ang_a) * sign_a, jnp.cos(ang_c), jnp.sin(ang_c) * sign_c)


def _layer_weights(i, w):
    w_in = w['w_in'][i]
    cols = lambda r: w_in[:, r[0]:r[1]]
    bf = lambda a: a.astype(BF16)
    row = lambda a: a.reshape(1, -1).astype(F32)
    wz = jnp.pad(cols(_ZL), ((0, 0), (0, LANES - (_ZL[1] - _ZL[0]))))
    w2 = w['gla_w2'][i]
    w2p = jnp.zeros((2, LANES, 256), F32).at[0, 0:16].set(w2[0]).at[1, 16:32].set(w2[1])
    wr = jnp.pad(w['w_router'][i], ((0, 0), (0, LANES - N_EXPERTS)))
    wr_hi = wr.astype(BF16)
    wbr = w['w_branch'][i]
    wbr = jnp.stack([wbr[0][_QPERM], wbr[1], wbr[2][_QPERM], wbr[3]])
    return dict(
        g_mix=row(w['norm_mix'][i]),
        wa=bf(jnp.concatenate([cols(_QA)[:, _QPERM], cols(_KA), cols(_VA)], axis=1)),
        wb=bf(cols(_UB)),
        wc=bf(jnp.concatenate([cols(_QC)[:, _QPERM], cols(_KC), cols(_VC)], axis=1)),
        wd=bf(cols(_DD)),
        wz=bf(wz),
        gq=row(jnp.tile(w['qk_norm'][i, 0], 4)),
        gk=row(jnp.tile(w['qk_norm'][i, 1], 2)),
        sink=w['sink_logit'][i][_HEAD_PERM].astype(F32) * LOG2E,
        conv_w=w['conv_dw'][i].astype(F32),
        conv_b=row(w['conv_dw_b'][i]), ln_g=row(w['conv_ln_g'][i]), ln_b=row(w['conv_ln_b'][i]),
        w2=bf(w2p), b2=w['gla_b2'][i].reshape(2, 1, 256).astype(F32),
        gn=row(jnp.tile(w['gla_norm'][i], 4)),
        wg=bf(w_in[:, _GATES:]), wbr=bf(wbr), wout=bf(w['w_out'][i]),
        g_ffn=row(w['norm_ffn'][i]),
        wr_hi=wr_hi, wr_lo=(wr - wr_hi.astype(F32)).astype(BF16),
        layer=i, we_g=w['w_gate_e'], we_u=w['w_up_e'], we_d=w['w_down_e'],
        g_ple=row(w['norm_ple'][i]), w_pg=bf(w['w_ple_gate'][i]), w_pp=bf(w['w_ple_proj'][i]),
    )


def _trunk(x3, p4, layers, gfin):
    B, S, _ = x3.shape
    N = B * S
    tabs = _rope_tables(S)
    x = x3.reshape(N, D_MODEL)
    cap = max(1, EC_CAPACITY * N // N_EXPERTS)
    for i, lw in enumerate(layers):
        qa, kat, va, zb, qc, kc, vc, d, zl = _proj(x, lw, tabs, S)
        b3 = lambda a: a.reshape(B, S, a.shape[-1])
        oa = _gattn(b3(qa), kat, b3(va))
        ob = _conv(b3(zb), lw)
        oc = _wattn(b3(qc), b3(kc), b3(vc), lw['sink'])
        ogf, ogb = _gla(b3(d), b3(zl), lw['w2'], lw['b2'])
        flat = lambda a: a.reshape(N, a.shape[-1])
        x1, h2, aff_t = _merge(x, flat(oa), flat(ob), flat(oc), flat(ogf), flat(ogb), d, lw)
        rank, incl = _select(aff_t, cap)
        idx, gate = _compact(rank, aff_t, cap)
        sub = D_MODEL // LANES
        h2_rows = h2.reshape(N // SUBLANES, SUBLANES, sub, LANES).transpose(0, 2, 1, 3).reshape(N * sub, LANES)
        piece = jnp.arange(sub, dtype=jnp.int32)[:, None]
        rows = ((idx // SUBLANES)[None, :] * sub + piece) * SUBLANES + (idx % SUBLANES)[None, :]
        xe = _gather_rows(h2_rows, rows.reshape(-1)).reshape(sub, -1, LANES)
        ye = _ffn(xe, gate.reshape(-1, 1), lw, cap)
        x = _combine(x1, p4[i].reshape(N, PLE_DIM), rank, incl, ye, lw, gfin, cap, i == len(layers) - 1)
    return x.reshape(B, S, D_MODEL)


def kernel(x_prompt, x_sample, p_prompt, p_sample, norm_mix, w_in, qk_norm, sink_logit, conv_dw, conv_dw_b,
           conv_ln_g, conv_ln_b, gla_w2, gla_b2, gla_norm, w_branch, w_out, norm_ffn, w_router, w_gate_e,
           w_up_e, w_down_e, norm_ple, w_ple_gate, w_ple_proj, norm_final):
    w = dict(norm_mix=norm_mix, w_in=w_in, qk_norm=qk_norm, sink_logit=sink_logit, conv_dw=conv_dw,
             conv_dw_b=conv_dw_b, conv_ln_g=conv_ln_g, conv_ln_b=conv_ln_b, gla_w2=gla_w2, gla_b2=gla_b2,
             gla_norm=gla_norm, w_branch=w_branch, w_out=w_out, norm_ffn=norm_ffn, w_router=w_router,
             w_gate_e=w_gate_e, w_up_e=w_up_e, w_down_e=w_down_e, norm_ple=norm_ple,
             w_ple_gate=w_ple_gate, w_ple_proj=w_ple_proj)
    layers = [_layer_weights(i, w) for i in range(norm_mix.shape[0])]
    gfin = norm_final.reshape(1, -1).astype(F32)
    return (_trunk(x_prompt, p_prompt, layers, gfin), _trunk(x_sample, p_sample, layers, gfin))
```

```python
import dataclasses
import functools

import jax
import jax.numpy as jnp
import numpy as np
from jax import lax
from jax.experimental import pallas as pl
from jax.experimental.pallas import tpu as pltpu
from jax.experimental.pallas import tpu_sc as plsc

F32 = jnp.float32
BF16 = jnp.bfloat16

D_MODEL = 1024
DEPTH = 4
GRID_W = 64
HEAD_DIM = 64
ROPE_THETA = 10000.0
NORM_EPS = 1e-6
CONV_CH = 256
CONV_WIDTH = 31
CONV_HALO = 16
WINDOW = 128
GLA_TAU = 16.0
GLA_CHUNK = 64
GLA_BLOCK = 256
GLA_ROWS = 2
N_EXPERTS = 16
EC_CAPACITY = 2
PLE_DIM = 256
LANES = 128
SUBLANES = 8
VMEM_LIMIT = 56 * 1024 * 1024
LOG2E = 1.4426950408889634
SEL_CHUNK = 512
COMB_WIN = 64
SC_LANES = 16
GATHER_WINDOW = 128

_QA, _KA, _VA = (0, 256), (256, 384), (384, 512)
_UB = (512, 1024)
_QC, _KC, _VC = (1024, 1280), (1280, 1408), (1408, 1536)
_DD = (1536, 2560)
_ZL = (2560, 2592)
_GATES = 2592
_QPERM = np.concatenate([np.arange(0, 64), np.arange(128, 192), np.arange(64, 128), np.arange(192, 256)])
_HEAD_PERM = np.array([0, 2, 1, 3])


def _cparams(*sem):
    return pltpu.CompilerParams(dimension_semantics=sem, vmem_limit_bytes=VMEM_LIMIT)


def _dot(a, b):
    return jnp.dot(a, b, preferred_element_type=F32)


def _dot_nt(a, b):
    return lax.dot_general(a, b, (((1,), (1,)), ((), ())), preferred_element_type=F32)


def _rms(x, g):
    return x * lax.rsqrt(jnp.mean(x * x, axis=-1, keepdims=True) + NORM_EPS) * g


def _split2(x):
    hi = x.astype(BF16)
    lo = (x - hi.astype(F32)).astype(BF16)
    return hi, lo


def _group_ones(width, group):
    r = lax.broadcasted_iota(jnp.int32, (width, width), 0) // group
    c = lax.broadcasted_iota(jnp.int32, (width, width), 1) // group
    return jnp.where(r == c, 1.0, 0.0).astype(BF16)


def _group_meansq(x, group):
    hi, lo = _split2(x * x)
    ones = _group_ones(x.shape[1], group)
    return (_dot(hi, ones) + _dot(lo, ones)) * (1.0 / group)


def _rope(x, cos, sin_signed, half):
    width = x.shape[1]
    lane = lax.broadcasted_iota(jnp.int32, x.shape, 1)
    from_lo = pltpu.roll(x, half, 1)
    from_hi = pltpu.roll(x, width - half, 1)
    partner = jnp.where((lane & half) != 0, from_lo, from_hi)
    return x * cos + partner * sin_signed


def _proj_kernel(x_ref, g_ref, wa_ref, wb_ref, wc_ref, wd_ref, wz_ref, gq_ref, gk_ref,
                 ca_ref, sa_ref, cc_ref, sc_ref,
                 qa_ref, kat_ref, va_ref, zb_ref, qc_ref, kc_ref, vc_ref, d_ref, zl_ref):
    h = _rms(x_ref[...], g_ref[...]).astype(BF16)
    scale = HEAD_DIM ** -0.5

    ua = _dot(h, wa_ref[...])
    ca, sa = ca_ref[...], sa_ref[...]
    q = ua[:, :256]
    q = q * lax.rsqrt(_group_meansq(q, HEAD_DIM) + NORM_EPS) * gq_ref[...]
    q = _rope(q, jnp.concatenate([ca, ca], axis=1), jnp.concatenate([sa, sa], axis=1), 16)
    qa_ref[...] = (q * (scale * LOG2E)).astype(BF16)
    k = ua[:, 256:384]
    k = k * lax.rsqrt(_group_meansq(k, HEAD_DIM) + NORM_EPS) * gk_ref[...]
    kat_ref[...] = _rope(k, ca, sa, 16).T.astype(BF16)
    va_ref[...] = ua[:, 384:].astype(BF16)

    ub = _dot(h, wb_ref[...])
    zb_ref[...] = (ub[:, :CONV_CH] * jax.nn.sigmoid(ub[:, CONV_CH:])).astype(BF16)

    uc = _dot(h, wc_ref[...])
    cc, sc = cc_ref[...], sc_ref[...]
    qc = _rope(uc[:, :256], jnp.concatenate([cc, cc], axis=1), jnp.concatenate([sc, sc], axis=1), 32)
    qc_ref[...] = (qc * (scale * LOG2E)).astype(BF16)
    kc_ref[...] = _rope(uc[:, 256:384], cc, sc, 32).astype(BF16)
    vc_ref[...] = uc[:, 384:].astype(BF16)

    ud = _dot(h, wd_ref[...])
    d_ref[:, :256] = (ud[:, :256] * scale).astype(BF16)
    d_ref[:, 256:] = ud[:, 256:].astype(BF16)
    zl_ref[...] = _dot(h, wz_ref[...]).astype(BF16)


def _proj(x, lw, tabs, S):
    N = x.shape[0]
    T = min(512, S)
    per_row = S // T
    tok = lambda w: pl.BlockSpec((T, w), lambda i: (i, 0))
    full = lambda a: pl.BlockSpec(a.shape, lambda i: (0,) * a.ndim)
    tab = pl.BlockSpec((T, LANES), lambda i: (i % per_row, 0))
    widths = (256, 128, 128, 256, 256, 128, 128, 1024, 128)
    consts = (lw['g_mix'], lw['wa'], lw['wb'], lw['wc'], lw['wd'], lw['wz'], lw['gq'], lw['gk'])
    return pl.pallas_call(
        _proj_kernel,
        grid=(N // T,),
        in_specs=[tok(D_MODEL)] + [full(a) for a in consts] + [tab] * 4,
        out_specs=[tok(256), pl.BlockSpec((LANES, T), lambda i: (0, i))] + [tok(w) for w in widths[2:]],
        out_shape=[jax.ShapeDtypeStruct((N, 256), BF16), jax.ShapeDtypeStruct((LANES, N), BF16)]
                  + [jax.ShapeDtypeStruct((N, w), BF16) for w in widths[2:]],
        compiler_params=_cparams("parallel"),
        name="proj",
    )(x, *consts, *tabs)


def _stack_heads(qb):
    lane = lax.broadcasted_iota(jnp.int32, qb.shape, 1)
    zero = jnp.zeros_like(qb)
    return jnp.concatenate([jnp.where(lane < HEAD_DIM, qb, zero), jnp.where(lane < HEAD_DIM, zero, qb)], axis=0)


def _unstack_heads(o, T):
    lane = lax.broadcasted_iota(jnp.int32, (T, LANES), 1)
    return jnp.where(lane < HEAD_DIM, o[:T], o[T:])


def _gattn_kernel(q_ref, kt_ref, v_ref, o_ref, *, tk):
    T = q_ref.shape[0]
    S = kt_ref.shape[1]
    lane_v = lax.broadcasted_iota(jnp.int32, (tk, LANES), 1)
    one = jnp.ones((tk, LANES), BF16)
    qs = [_stack_heads(q_ref[:, j * LANES:(j + 1) * LANES]) for j in range(2)]
    m = [jnp.full((2 * T, 1), -jnp.inf, F32) for _ in range(2)]
    acc = [jnp.zeros((2 * T, LANES), F32) for _ in range(2)]
    for c in range(S // tk):
        kt = kt_ref[:, c * tk:(c + 1) * tk]
        v = v_ref[c * tk:(c + 1) * tk, :]
        va = jnp.where(lane_v < HEAD_DIM, v, one)
        vb = jnp.where(lane_v < HEAD_DIM, one, v)
        for j in range(2):
            s = _dot(qs[j], kt)
            m_new = jnp.maximum(m[j], jnp.max(s, axis=1, keepdims=True))
            alpha = jnp.exp2(m[j] - m_new)
            p = jnp.exp2(s - m_new).astype(BF16)
            pv = jnp.concatenate([_dot(p[:T], va), _dot(p[T:], vb)], axis=0)
            acc[j] = alpha * acc[j] + pv
            m[j] = m_new
    lane_o = lax.broadcasted_iota(jnp.int32, (T, LANES), 1)
    for j in range(2):
        a, b = acc[j][:T], acc[j][T:]
        o = jnp.where(lane_o < HEAD_DIM, a / pltpu.roll(a, HEAD_DIM, 1), b / pltpu.roll(b, HEAD_DIM, 1))
        o_ref[:, j * LANES:(j + 1) * LANES] = o.astype(BF16)


def _gattn(q, kt, v):
    B, S, _ = q.shape
    T = min(512, S)
    tk = min(1024, S)
    return pl.pallas_call(
        functools.partial(_gattn_kernel, tk=tk),
        grid=(B, S // T),
        in_specs=[pl.BlockSpec((None, T, 256), lambda b, i: (b, i, 0)),
                  pl.BlockSpec((LANES, S), lambda b, i: (0, b)),
                  pl.BlockSpec((None, S, LANES), lambda b, i: (b, 0, 0))],
        out_specs=pl.BlockSpec((None, T, 256), lambda b, i: (b, i, 0)),
        out_shape=jax.ShapeDtypeStruct((B, S, 256), BF16),
        compiler_params=_cparams("parallel", "parallel"),
        name="gattn",
    )(q, kt, v)


def _wattn_kernel(sink_ref, q_ref, k_ref, v_ref, bias_ref, o_ref, *, kw):
    T = q_ref.shape[0]
    S = k_ref.shape[0]
    i = pl.program_id(1)
    start = pl.multiple_of(jnp.clip(i * T - WINDOW, 0, S - kw), WINDOW)
    kwin = k_ref[pl.ds(start, kw), :]
    vwin = v_ref[pl.ds(start, kw), :]
    band = bias_ref[(i * T - start) // WINDOW]
    bias = jnp.concatenate([band, band], axis=0)
    lane_v = lax.broadcasted_iota(jnp.int32, (kw, LANES), 1)
    one = jnp.ones((kw, LANES), BF16)
    va = jnp.where(lane_v < HEAD_DIM, vwin, one)
    vb = jnp.where(lane_v < HEAD_DIM, one, vwin)
    first = lax.broadcasted_iota(jnp.int32, (2 * T, 1), 0) < T
    lane_o = lax.broadcasted_iota(jnp.int32, (T, LANES), 1)
    for j in range(2):
        qst = _stack_heads(q_ref[:, j * LANES:(j + 1) * LANES])
        s = _dot_nt(qst, kwin) + bias
        sk = jnp.where(first, sink_ref[2 * j], sink_ref[2 * j + 1])
        m = jnp.maximum(jnp.max(s, axis=1, keepdims=True), sk)
        p = jnp.exp2(s - m).astype(BF16)
        sunk = jnp.exp2(sk - m)
        a = _dot(p[:T], va)
        b = _dot(p[T:], vb)
        o = jnp.where(lane_o < HEAD_DIM, a / (pltpu.roll(a, HEAD_DIM, 1) + sunk[:T]),
                      b / (pltpu.roll(b, HEAD_DIM, 1) + sunk[T:]))
        o_ref[:, j * LANES:(j + 1) * LANES] = o.astype(BF16)


def _band_bias(T, kw):
    r = np.arange(T)[None, :, None]
    c = np.arange(kw)[None, None, :]
    off = np.arange(3)[:, None, None] * WINDOW
    return jnp.asarray(np.where(np.abs(c - off - r) <= WINDOW, 0.0, -np.inf), F32)


def _wattn(q, k, v, sink):
    B, S, _ = q.shape
    T = min(256, S - 2 * WINDOW) if S > 2 * WINDOW else S
    kw = min(T + 2 * WINDOW, S)
    bias = _band_bias(T, kw)
    return pl.pallas_call(
        functools.partial(_wattn_kernel, kw=kw),
        grid=(B, S // T),
        in_specs=[pl.BlockSpec(memory_space=pltpu.SMEM),
                  pl.BlockSpec((None, T, 256), lambda b, i: (b, i, 0)),
                  pl.BlockSpec((None, S, LANES), lambda b, i: (b, 0, 0)),
                  pl.BlockSpec((None, S, LANES), lambda b, i: (b, 0, 0)),
                  pl.BlockSpec(bias.shape, lambda b, i: (0, 0, 0))],
        out_specs=pl.BlockSpec((None, T, 256), lambda b, i: (b, i, 0)),
        out_shape=jax.ShapeDtypeStruct((B, S, 256), BF16),
        compiler_params=_cparams("parallel", "parallel"),
        name="wattn",
    )(sink, q, k, v, bias)


def _conv_kernel(zp_ref, zc_ref, zn_ref, w_ref, b_ref, g_ref, beta_ref, o_ref, buf_ref, sh_ref):
    T = zc_ref.shape[0]
    i = pl.program_id(1)
    H = CONV_HALO
    keep_prev = jnp.where(i > 0, 1.0, 0.0)
    keep_next = jnp.where(i < pl.num_programs(1) - 1, 1.0, 0.0)
    buf_ref[0:H, :] = zp_ref[...].astype(F32) * keep_prev
    buf_ref[H:H + T, :] = zc_ref[...].astype(F32)
    buf_ref[H + T:, :] = zn_ref[...].astype(F32) * keep_next
    span = sh_ref.shape[1]
    for r in range(1, SUBLANES):
        sh_ref[r] = buf_ref[r:r + span, :]
    acc = jnp.zeros((T, CONV_CH), F32)
    off = H - CONV_WIDTH // 2
    for tap in range(CONV_WIDTH):
        r = (off + tap) % SUBLANES
        a = off + tap - r
        rows = buf_ref[a:a + T, :] if r == 0 else sh_ref[r, a:a + T, :]
        acc = acc + rows * w_ref[tap:tap + 1, :]
    z = acc + b_ref[...]
    mu = jnp.mean(z, axis=-1, keepdims=True)
    zc = z - mu
    var = jnp.mean(zc * zc, axis=-1, keepdims=True)
    y = zc * lax.rsqrt(var + NORM_EPS) * g_ref[...] + beta_ref[...]
    o_ref[...] = (y * jax.nn.sigmoid(y)).astype(BF16)


def _conv(z, lw):
    B, S, _ = z.shape
    T = min(512, S)
    H = CONV_HALO
    per = T // H
    last = S // H - 1
    full = lambda a: pl.BlockSpec(a.shape, lambda b, i: (0,) * a.ndim)
    consts = (lw['conv_w'], lw['conv_b'], lw['ln_g'], lw['ln_b'])
    return pl.pallas_call(
        _conv_kernel,
        grid=(B, S // T),
        in_specs=[pl.BlockSpec((None, H, CONV_CH), lambda b, i: (b, jnp.maximum(i * per - 1, 0), 0)),
                  pl.BlockSpec((None, T, CONV_CH), lambda b, i: (b, i, 0)),
                  pl.BlockSpec((None, H, CONV_CH), lambda b, i: (b, jnp.minimum((i + 1) * per, last), 0))]
                 + [full(a) for a in consts],
        out_specs=pl.BlockSpec((None, T, CONV_CH), lambda b, i: (b, i, 0)),
        out_shape=jax.ShapeDtypeStruct((B, S, CONV_CH), BF16),
        scratch_shapes=[pltpu.VMEM((T + 2 * H, CONV_CH), F32),
                        pltpu.VMEM((SUBLANES, T + 2 * H - SUBLANES, CONV_CH), F32)],
        compiler_params=_cparams("parallel", "parallel"),
        name="conv",
    )(z, z, z, *consts)


def _gla_block(d_ref, zl_ref, w2, b2, tri, att_mask, st_ref, reverse):
    TB = d_ref.shape[0]
    L = GLA_CHUNK
    nch = TB // L

    pre = _dot(zl_ref[...], w2) + b2
    la = (jnp.minimum(pre, 0.0) - jnp.log(1.0 + jnp.exp(-jnp.abs(pre)))) * (1.0 / GLA_TAU)

    hi = la.astype(BF16)
    r1 = la - hi.astype(F32)
    mid = r1.astype(BF16)
    lo = (r1 - mid.astype(F32)).astype(BF16)
    cums = _dot(tri, jnp.concatenate([hi, mid, lo], axis=1))
    b = cums[:, :256] + cums[:, 256:512] + cums[:, 512:]
    mid_row = L // 2 if reverse else L // 2 - 1
    last_row = 0 if reverse else L - 1
    per_chunk = lambda row: jnp.concatenate(
        [jnp.broadcast_to(b[ci * L + row:ci * L + row + 1], (L, 256)) for ci in range(nch)], axis=0)
    bmid, blast = per_chunk(mid_row), per_chunk(last_row)

    q = d_ref[:, 0:256].astype(F32)
    k = d_ref[:, 256:512].astype(F32)
    v = d_ref[:, 512:768]
    qt = (q * jnp.exp(b - bmid)).astype(BF16)
    kt = (k * jnp.exp(bmid - b)).astype(BF16)
    qe = (q * jnp.exp(b)).astype(BF16)
    kl = (k * jnp.exp(blast - b)).astype(BF16)
    dec = jnp.exp(blast)

    rr = lax.broadcasted_iota(jnp.int32, (LANES, LANES), 0) // HEAD_DIM
    cc = lax.broadcasted_iota(jnp.int32, (LANES, LANES), 1) // HEAD_DIM
    head_diag = rr == cc
    rowid = lax.broadcasted_iota(jnp.int32, (TB, LANES), 0) // L
    order = range(nch - 1, -1, -1) if reverse else range(nch)

    out = []
    for p in range(2):
        ls = slice(p * LANES, (p + 1) * LANES)
        vb = v[:, ls]
        att = _dot_nt(_stack_heads(qt[:, ls]), kt[:, ls]) * att_mask
        o_intra = _unstack_heads(_dot(att.astype(BF16), vb), TB)
        vt = vb.astype(F32).T.astype(BF16)
        klb = kl[:, ls]
        kv_t = [jnp.where(head_diag, _dot(vt, jnp.where(rowid == ci, klb, jnp.zeros_like(klb))), 0.0)
                for ci in range(nch)]
        st = st_ref[p]
        o_inter = [None] * nch
        for ci in order:
            o_inter[ci] = _dot_nt(qe[ci * L:(ci + 1) * L, ls], st.astype(BF16))
            st = st * dec[ci * L:ci * L + 1, ls] + kv_t[ci]
        st_ref[p] = st
        out.append(o_intra + jnp.concatenate(o_inter, axis=0))
    return out


def _gla_kernel(df_ref, zf_ref, db_ref, zb_ref, w2_ref, b2_ref, tri_ref, mask_ref, of_ref, ob_ref, st_ref):
    @pl.when(pl.program_id(1) == 0)
    def _():
        st_ref[...] = jnp.zeros(st_ref.shape, F32)

    for r in range(df_ref.shape[0]):
        fwd = _gla_block(df_ref.at[r], zf_ref.at[r], w2_ref[0], b2_ref[0], tri_ref[0], mask_ref[0],
                         st_ref.at[r, 0], False)
        bwd = _gla_block(db_ref.at[r], zb_ref.at[r], w2_ref[1], b2_ref[1], tri_ref[1], mask_ref[1],
                         st_ref.at[r, 1], True)
        for p in range(2):
            of_ref[r, :, p * LANES:(p + 1) * LANES] = fwd[p]
            ob_ref[r, :, p * LANES:(p + 1) * LANES] = bwd[p]


def _gla_masks(TB):
    L = GLA_CHUNK
    r = np.arange(TB)[:, None]
    c = np.arange(TB)[None, :]
    same = (r // L) == (c // L)
    tri = np.stack([same & (c <= r), same & (c >= r)]).astype(np.float32)
    return jnp.asarray(tri, BF16), jnp.asarray(np.concatenate([tri, tri], axis=1), F32)


def _gla(d, zl, w2, b2):
    B, S, _ = d.shape
    TB = min(GLA_BLOCK, S)
    nb = S // TB
    R = GLA_ROWS if B % GLA_ROWS == 0 else 1
    fblk = lambda b, i: (b, i, 0)
    bblk = lambda b, i: (b, nb - 1 - i, 0)
    full = lambda a: pl.BlockSpec(a.shape, lambda b, i: (0,) * a.ndim)
    tri, mask = _gla_masks(TB)
    out = jax.ShapeDtypeStruct((B, S, 256), F32)
    return pl.pallas_call(
        _gla_kernel,
        grid=(B // R, nb),
        in_specs=[pl.BlockSpec((R, TB, 1024), fblk), pl.BlockSpec((R, TB, LANES), fblk),
                  pl.BlockSpec((R, TB, 1024), bblk), pl.BlockSpec((R, TB, LANES), bblk),
                  full(w2), full(b2), full(tri), full(mask)],
        out_specs=[pl.BlockSpec((R, TB, 256), fblk), pl.BlockSpec((R, TB, 256), bblk)],
        out_shape=[out, out],
        scratch_shapes=[pltpu.VMEM((R, 2, 2, LANES, LANES), F32)],
        compiler_params=_cparams("parallel", "arbitrary"),
        name="gla",
    )(d, zl, d, zl, w2, b2, tri, mask)


def _merge_kernel(x_ref, oa_ref, ob_ref, oc_ref, ogf_ref, ogb_ref, rd_ref, gmix_ref, wg_ref, wbr_ref, wout_ref,
                  gn_ref, gffn_ref, wrh_ref, wrl_ref, x1_ref, h2_ref, aff_ref):
    x = x_ref[...]
    h = _rms(x, gmix_ref[...]).astype(BF16)
    og = ogf_ref[...] + ogb_ref[...]
    od = og * lax.rsqrt(_group_meansq(og, HEAD_DIM) + NORM_EPS) * gn_ref[...]
    rd = rd_ref[...].astype(F32)
    od = (od * (rd * jax.nn.sigmoid(rd))).astype(BF16)
    branches = (oa_ref[...], ob_ref[...], oc_ref[...], od)
    merged = jnp.zeros(x.shape, F32)
    for i, o in enumerate(branches):
        gate = jax.nn.sigmoid(_dot(h, wg_ref[:, i * D_MODEL:(i + 1) * D_MODEL]))
        merged = merged + gate * _dot(o, wbr_ref[i])
    x1 = x + _dot(merged.astype(BF16), wout_ref[...])
    x1_ref[...] = x1
    h2 = _rms(x1, gffn_ref[...])
    h2_ref[...] = h2
    hh, hl = _split2(h2)
    logits = _dot(hh, wrh_ref[...]) + _dot(hh, wrl_ref[...]) + _dot(hl, wrh_ref[...])
    lt = logits.T[:N_EXPERTS]
    e = jnp.exp(lt - jnp.max(lt, axis=0, keepdims=True))
    aff_ref[...] = e / jnp.sum(e, axis=0, keepdims=True)


def _merge(x, oa, ob, oc, ogf, ogb, d, lw):
    N = x.shape[0]
    T = min(256, N)
    tok = lambda w: pl.BlockSpec((T, w), lambda i: (i, 0))
    full = lambda a: pl.BlockSpec(a.shape, lambda i: (0,) * a.ndim)
    consts = (lw['g_mix'], lw['wg'], lw['wbr'], lw['wout'], lw['gn'], lw['g_ffn'], lw['wr_hi'], lw['wr_lo'])
    return pl.pallas_call(
        _merge_kernel,
        grid=(N // T,),
        in_specs=[tok(D_MODEL), tok(256), tok(256), tok(256), tok(256), tok(256),
                  pl.BlockSpec((T, 256), lambda i: (i, 3))] + [full(a) for a in consts],
        out_specs=[tok(D_MODEL), tok(D_MODEL), pl.BlockSpec((N_EXPERTS, T), lambda i: (0, i))],
        out_shape=[jax.ShapeDtypeStruct((N, D_MODEL), F32), jax.ShapeDtypeStruct((N, D_MODEL), F32),
                   jax.ShapeDtypeStruct((N_EXPERTS, N), F32)],
        compiler_params=_cparams("parallel"),
        name="merge",
    )(x, oa, ob, oc, ogf, ogb, d, *consts)


def _select_kernel(aff_ref, rank_ref, incl_ref, *, cap):
    E, N = aff_ref.shape
    CH = min(N, 4096)
    SC = min(N, SEL_CHUNK)

    def count(pred):
        def body(c, acc):
            start = pl.multiple_of(c * CH, CH)
            bits = lax.bitcast_convert_type(aff_ref[:, pl.ds(start, CH)], jnp.int32)
            tok = start + lax.broadcasted_iota(jnp.int32, (E, CH), 1)
            return acc + jnp.sum(pred(bits, tok), axis=1, keepdims=True)
        return lax.fori_loop(0, N // CH, body, jnp.zeros((E, 1), F32))

    def value_step(_, lohi):
        lo, hi = lohi
        mid = lo + ((hi - lo) >> 1)
        ok = count(lambda b, t: jnp.where(b >= mid, 1.0, 0.0)) >= cap
        return jnp.where(ok, mid, lo), jnp.where(ok, hi, mid)

    thr, _ = lax.fori_loop(0, 32, value_step,
                           (jnp.zeros((E, 1), jnp.int32), jnp.full((E, 1), 0x7F800000, jnp.int32)))
    need = cap - count(lambda b, t: jnp.where(b > thr, 1.0, 0.0))

    def tie_step(_, lohi):
        lo, hi = lohi
        mid = lo + ((hi - lo) >> 1)
        ok = count(lambda b, t: jnp.where(b == thr, jnp.where(t <= mid, 1.0, 0.0), 0.0)) >= need
        return jnp.where(ok, lo, mid), jnp.where(ok, mid, hi)

    on_thr = count(lambda b, t: jnp.where(b == thr, 1.0, 0.0))
    all_in = jnp.full((E, 1), N - 1, jnp.int32)
    cut = lax.cond(
        jnp.max(on_thr - need) <= 0.0,
        lambda: all_in,
        lambda: lax.fori_loop(0, int(np.ceil(np.log2(N))) + 1, tie_step,
                              (jnp.full((E, 1), -1, jnp.int32), all_in))[1])

    r = lax.broadcasted_iota(jnp.int32, (SC, SC), 0)
    c = lax.broadcasted_iota(jnp.int32, (SC, SC), 1)
    tri = jnp.where(r <= c, 1.0, 0.0).astype(BF16)

    def emit(ci, carry):
        start = pl.multiple_of(ci * SC, SC)
        bits = lax.bitcast_convert_type(aff_ref[:, pl.ds(start, SC)], jnp.int32)
        tok = start + lax.broadcasted_iota(jnp.int32, (E, SC), 1)
        picked = jnp.where(bits > thr, 1.0, jnp.where(bits == thr, jnp.where(tok <= cut, 1.0, 0.0), 0.0))
        inc = carry + _dot(picked.astype(BF16), tri)
        incl_ref[:, pl.ds(start, SC)] = inc
        rank_ref[:, pl.ds(start, SC)] = jnp.where(picked > 0.0, inc - 1.0, -1.0).astype(jnp.int32)
        return inc[:, SC - 1:SC]

    lax.fori_loop(0, N // SC, emit, jnp.zeros((E, 1), F32))


def _select(aff_t, cap):
    E, N = aff_t.shape
    full = pl.BlockSpec((E, N), lambda i: (0, 0))
    return pl.pallas_call(
        functools.partial(_select_kernel, cap=cap),
        grid=(1,),
        in_specs=[full],
        out_specs=[full, full],
        out_shape=[jax.ShapeDtypeStruct((E, N), jnp.int32), jax.ShapeDtypeStruct((E, N), F32)],
        compiler_params=_cparams("arbitrary"),
        name="select",
    )(aff_t)


def _sc_params():
    cp = pltpu.CompilerParams()
    if "needs_layout_passes" in pltpu.CompilerParams.__dataclass_fields__:
        cp = dataclasses.replace(cp, needs_layout_passes=False)
    return cp


def _compact(rank, aff_t, cap):
    E, N = rank.shape
    CH = min(N, 4096)
    mesh = plsc.VectorSubcoreMesh(core_axis_name="c", subcore_axis_name="s")

    @pl.kernel(out_type=(jax.ShapeDtypeStruct((E * cap,), jnp.int32), jax.ShapeDtypeStruct((E * cap,), F32)),
               mesh=mesh,
               scratch_types=[pltpu.VMEM((CH,), jnp.int32), pltpu.VMEM((CH,), F32),
                              pltpu.VMEM((cap,), jnp.int32), pltpu.VMEM((cap,), F32)],
               compiler_params=_sc_params())
    def compact(rank_hbm, aff_hbm, idx_hbm, gate_hbm, rbuf, abuf, ibuf, gbuf):
        wid = lax.axis_index("s") * mesh.num_cores + lax.axis_index("c")

        @pl.when(wid < E)
        def _():
            @pl.loop(0, N // CH)
            def _(c):
                base = wid * N + c * CH
                pltpu.sync_copy(rank_hbm.at[pl.ds(base, CH)], rbuf)
                pltpu.sync_copy(aff_hbm.at[pl.ds(base, CH)], abuf)

                @pl.loop(0, CH, step=SC_LANES)
                def _(i):
                    r = rbuf[pl.ds(i, SC_LANES)]
                    picked = r >= 0
                    slot = jnp.where(picked, r, 0)
                    tok = c * CH + i + lax.iota(jnp.int32, SC_LANES)
                    plsc.store_scatter(ibuf, [slot], tok, mask=picked)
                    plsc.store_scatter(gbuf, [slot], abuf[pl.ds(i, SC_LANES)], mask=picked)

            pltpu.sync_copy(ibuf, idx_hbm.at[pl.ds(wid * cap, cap)])
            pltpu.sync_copy(gbuf, gate_hbm.at[pl.ds(wid * cap, cap)])

    return compact(rank.reshape(E * N), aff_t.reshape(E * N))


def _gather_rows(x, idx):
    R = idx.shape[0]
    win = GATHER_WINDOW
    mesh = plsc.VectorSubcoreMesh(core_axis_name="c", subcore_axis_name="s")

    @pl.kernel(out_type=jax.ShapeDtypeStruct((R, LANES), x.dtype), mesh=mesh)
    def gather(x_hbm, i_hbm, o_hbm):
        def body(i_vmem, o_vmem):
            pltpu.sync_copy(x_hbm.at[i_vmem.at[0]], o_vmem)

        pltpu.emit_pipeline(
            body,
            grid=(R // win,),
            in_specs=[pl.BlockSpec((1, win), lambda i: (0, i))],
            out_specs=[pl.BlockSpec((win, LANES), lambda i: (i, 0))],
            core_axis_name=("c", "s"),
            dimension_semantics=(pltpu.PARALLEL,),
            trace_scopes=False,
        )(i_hbm, o_hbm)

    return gather(x, idx.reshape(1, R))


def _ffn_kernel(x_ref, wg_ref, wu_ref, wd_ref, gate_ref, y_ref, w_ref):
    @pl.when(pl.program_id(1) == 0)
    def _():
        w_ref[0] = wg_ref[...].astype(BF16)
        w_ref[1] = wu_ref[...].astype(BF16)
        w_ref[2] = wd_ref[...].astype(BF16)

    x = jnp.concatenate([x_ref[k] for k in range(D_MODEL // LANES)], axis=1).astype(BF16)
    g = _dot(x, w_ref[0])
    hid = (g * jax.nn.sigmoid(g)) * _dot(x, w_ref[1])
    y_ref[...] = (_dot(hid.astype(BF16), w_ref[2]) * gate_ref[...]).astype(BF16)


def _ffn(xe, gate, lw, cap):
    E = N_EXPERTS
    M = min(512, cap)
    per = cap // M
    layer = lw['layer']
    wspec = pl.BlockSpec((None, None, D_MODEL, D_MODEL), lambda e, s: (layer, e, 0, 0))
    return pl.pallas_call(
        _ffn_kernel,
        grid=(E, per),
        in_specs=[pl.BlockSpec((D_MODEL // LANES, M, LANES), lambda e, s: (0, e * per + s, 0)), wspec, wspec, wspec,
                  pl.BlockSpec((M, 1), lambda e, s: (e * per + s, 0))],
        out_specs=pl.BlockSpec((M, D_MODEL), lambda e, s: (e * per + s, 0)),
        out_shape=jax.ShapeDtypeStruct((E * cap, D_MODEL), BF16),
        scratch_shapes=[pltpu.VMEM((3, D_MODEL, D_MODEL), BF16)],
        compiler_params=_cparams("parallel", "arbitrary"),
        name="ffn",
    )(xe, lw['we_g'], lw['we_u'], lw['we_d'], gate)


def _combine_kernel(starts_ref, rounds_ref, x_ref, p_ref, rank_ref, ye_hbm, g_ref, wgate_ref, wproj_ref,
                    gfin_ref, spread_ref, o_ref, stage_ref, sem_ref, *, cap, final):
    j = pl.program_id(0)
    nj = pl.num_programs(0)
    T = x_ref.shape[0]
    E, W = N_EXPERTS, COMB_WIN
    slot = j % 2

    def window_start(tile, rnd, e):
        return pl.multiple_of(jnp.minimum(starts_ref[tile * E + e] + rnd * W, E * cap - W), 16)

    def window_copy(tile, rnd, e, sl):
        return pltpu.make_async_copy(ye_hbm.at[pl.ds(window_start(tile, rnd, e), W), :],
                                     stage_ref.at[sl, pl.ds(e * W, W), :], sem_ref.at[sl])

    def start_all(tile, rnd, sl):
        for e in range(E):
            window_copy(tile, rnd, e, sl).start()

    def wait_all(tile, rnd, sl):
        for e in range(E):
            window_copy(tile, rnd, e, sl).wait()

    @pl.when(j == 0)
    def _():
        start_all(0, 0, 0)

    @pl.when(j + 1 < nj)
    def _():
        start_all(j + 1, 0, 1 - slot)

    rk = rank_ref[...]
    erow = lax.broadcasted_iota(jnp.int32, (E, 1), 0)
    tgt = rk + erow * cap

    def per_expert(fn):
        v = jnp.zeros((E, 1), jnp.int32)
        for e in range(E):
            v = jnp.where(erow == e, fn(e), v)
        return v

    def placed(rnd):
        staged_from = per_expert(lambda e: window_start(j, rnd, e))
        fresh = per_expert(lambda e: starts_ref[j * E + e] + rnd * W)
        loc = jnp.where(rk >= 0, jnp.where(tgt >= fresh, tgt - staged_from, -1), -1)
        loc = jnp.where(loc < W, loc, -1).astype(F32)
        lhs = jnp.concatenate([loc, jnp.ones((1, T), F32), jnp.zeros((LANES - E - 1, T), F32)], axis=0).T
        spread = _dot(lhs.astype(BF16), spread_ref[...])
        return _dot(jnp.where(spread == 0.0, 1.0, 0.0).astype(BF16), stage_ref[slot])

    wait_all(j, 0, slot)
    moe = placed(0)

    def extra(rnd, acc):
        start_all(j, rnd, slot)
        wait_all(j, rnd, slot)
        return acc + placed(rnd)

    moe = lax.fori_loop(1, rounds_ref[j], extra, moe)

    x = x_ref[...] + moe
    h = _rms(x, g_ref[...]).astype(BF16)
    gate = jax.nn.sigmoid(_dot(h, wgate_ref[...]))
    y = x + gate * _dot(p_ref[...].astype(BF16), wproj_ref[...])
    if final:
        y = _rms(y, gfin_ref[...])
    o_ref[...] = y


def _combine(x1, p, rank, incl, ye, lw, gfin, cap, final):
    N = x1.shape[0]
    E, W = N_EXPERTS, COMB_WIN
    T = min(256, N)
    nt = N // T
    ends = incl[:, T - 1::T].astype(jnp.int32)
    begins = jnp.concatenate([jnp.zeros((E, 1), jnp.int32), ends[:, :-1]], axis=1)
    base = jnp.arange(E, dtype=jnp.int32)[:, None] * cap
    aligned = (base + begins) // 16 * 16
    rounds = jnp.maximum(1, jnp.max((base + ends - aligned + W - 1) // W, axis=0)).astype(jnp.int32)
    starts = aligned.T.reshape(-1)

    tok = lambda w: pl.BlockSpec((T, w), lambda i, *_: (i, 0))
    full = lambda a: pl.BlockSpec(a.shape, lambda i, *_: (0,) * a.ndim)
    col = np.arange(E * W)[None, :]
    row = np.arange(LANES)[:, None]
    spread = np.where(row == col // W, 1.0, 0.0) + np.where(row == E, -(col % W), 0.0)
    consts = (lw['g_ple'], lw['w_pg'], lw['w_pp'], gfin, jnp.asarray(spread, BF16))
    grid_spec = pltpu.PrefetchScalarGridSpec(
        num_scalar_prefetch=2,
        grid=(nt,),
        in_specs=[tok(D_MODEL), tok(PLE_DIM), pl.BlockSpec((E, T), lambda i, *_: (0, i)),
                  pl.BlockSpec(memory_space=pl.ANY)] + [full(a) for a in consts],
        out_specs=tok(D_MODEL),
        scratch_shapes=[pltpu.VMEM((2, E * W, D_MODEL), BF16), pltpu.SemaphoreType.DMA((2,))],
    )
    return pl.pallas_call(
        functools.partial(_combine_kernel, cap=cap, final=final),
        grid_spec=grid_spec,
        out_shape=jax.ShapeDtypeStruct((N, D_MODEL), F32),
        compiler_params=_cparams("arbitrary"),
        name="combine",
    )(starts, rounds, x1, p, rank, ye, *consts)


def _rope_tables(S):
    lane = np.arange(LANES)
    d = lane % HEAD_DIM
    t = jnp.arange(S)
    inv_a = ROPE_THETA ** (-jnp.arange(0, HEAD_DIM // 2, 2, dtype=F32) / (HEAD_DIM // 2))
    pos_a = jnp.where((d // 32 == 0)[None, :], (t // GRID_W)[:, None], (t % GRID_W)[:, None]).astype(F32)
    ang_a = pos_a * inv_a[d % 16][None, :]
    sign_a = jnp.where(d % 32 < 16, -1.0, 1.0)[None, :]
    inv_c = ROPE_THETA ** (-jnp.arange(0, HEAD_DIM, 2, dtype=F32) / HEAD_DIM)
    ang_c = t.astype(F32)[:, None] * inv_c[d % 32][None, :]
    sign_c = jnp.where(d < 32, -1.0, 1.0)[None, :]
    return (jnp.cos(ang_a), jnp.sin(ang_a) * sign_a, jnp.cos(ang_c), jnp.sin(ang_c) * sign_c)


def _layer_weights(i, w):
    w_in = w['w_in'][i]
    cols = lambda r: w_in[:, r[0]:r[1]]
    bf = lambda a: a.astype(BF16)
    row = lambda a: a.reshape(1, -1).astype(F32)
    wz = jnp.pad(cols(_ZL), ((0, 0), (0, LANES - (_ZL[1] - _ZL[0]))))
    w2 = w['gla_w2'][i]
    w2p = jnp.zeros((2, LANES, 256), F32).at[0, 0:16].set(w2[0]).at[1, 16:32].set(w2[1])
    wr = jnp.pad(w['w_router'][i], ((0, 0), (0, LANES - N_EXPERTS)))
    wr_hi = wr.astype(BF16)
    wbr = w['w_branch'][i]
    wbr = jnp.stack([wbr[0][_QPERM], wbr[1], wbr[2][_QPERM], wbr[3]])
    return dict(
        g_mix=row(w['norm_mix'][i]),
        wa=bf(jnp.concatenate([cols(_QA)[:, _QPERM], cols(_KA), cols(_VA)], axis=1)),
        wb=bf(cols(_UB)),
        wc=bf(jnp.concatenate([cols(_QC)[:, _QPERM], cols(_KC), cols(_VC)], axis=1)),
        wd=bf(cols(_DD)),
        wz=bf(wz),
        gq=row(jnp.tile(w['qk_norm'][i, 0], 4)),
        gk=row(jnp.tile(w['qk_norm'][i, 1], 2)),
        sink=w['sink_logit'][i][_HEAD_PERM].astype(F32) * LOG2E,
        conv_w=w['conv_dw'][i].astype(F32),
        conv_b=row(w['conv_dw_b'][i]), ln_g=row(w['conv_ln_g'][i]), ln_b=row(w['conv_ln_b'][i]),
        w2=bf(w2p), b2=w['gla_b2'][i].reshape(2, 1, 256).astype(F32),
        gn=row(jnp.tile(w['gla_norm'][i], 4)),
        wg=bf(w_in[:, _GATES:]), wbr=bf(wbr), wout=bf(w['w_out'][i]),
        g_ffn=row(w['norm_ffn'][i]),
        wr_hi=wr_hi, wr_lo=(wr - wr_hi.astype(F32)).astype(BF16),
        layer=i, we_g=w['w_gate_e'], we_u=w['w_up_e'], we_d=w['w_down_e'],
        g_ple=row(w['norm_ple'][i]), w_pg=bf(w['w_ple_gate'][i]), w_pp=bf(w['w_ple_proj'][i]),
    )


def _trunk(x3, p4, layers, gfin):
    B, S, _ = x3.shape
    N = B * S
    tabs = _rope_tables(S)
    x = x3.reshape(N, D_MODEL)
    cap = max(1, EC_CAPACITY * N // N_EXPERTS)
    for i, lw in enumerate(layers):
        qa, kat, va, zb, qc, kc, vc, d, zl = _proj(x, lw, tabs, S)
        b3 = lambda a: a.reshape(B, S, a.shape[-1])
        oa = _gattn(b3(qa), kat, b3(va))
        ob = _conv(b3(zb), lw)
        oc = _wattn(b3(qc), b3(kc), b3(vc), lw['sink'])
        ogf, ogb = _gla(b3(d), b3(zl), lw['w2'], lw['b2'])
        flat = lambda a: a.reshape(N, a.shape[-1])
        x1, h2, aff_t = _merge(x, flat(oa), flat(ob), flat(oc), flat(ogf), flat(ogb), d, lw)
        rank, incl = _select(aff_t, cap)
        idx, gate = _compact(rank, aff_t, cap)
        sub = D_MODEL // LANES
        h2_rows = h2.reshape(N // SUBLANES, SUBLANES, sub, LANES).transpose(0, 2, 1, 3).reshape(N * sub, LANES)
        piece = jnp.arange(sub, dtype=jnp.int32)[:, None]
        rows = ((idx // SUBLANES)[None, :] * sub + piece) * SUBLANES + (idx % SUBLANES)[None, :]
        xe = _gather_rows(h2_rows, rows.reshape(-1)).reshape(sub, -1, LANES)
        ye = _ffn(xe, gate.reshape(-1, 1), lw, cap)
        x = _combine(x1, p4[i].reshape(N, PLE_DIM), rank, incl, ye, lw, gfin, cap, i == len(layers) - 1)
    return x.reshape(B, S, D_MODEL)


def kernel(x_prompt, x_sample, p_prompt, p_sample, norm_mix, w_in, qk_norm, sink_logit, conv_dw, conv_dw_b,
           conv_ln_g, conv_ln_b, gla_w2, gla_b2, gla_norm, w_branch, w_out, norm_ffn, w_router, w_gate_e,
           w_up_e, w_down_e, norm_ple, w_ple_gate, w_ple_proj, norm_final):
    w = dict(norm_mix=norm_mix, w_in=w_in, qk_norm=qk_norm, sink_logit=sink_logit, conv_dw=conv_dw,
             conv_dw_b=conv_dw_b, conv_ln_g=conv_ln_g, conv_ln_b=conv_ln_b, gla_w2=gla_w2, gla_b2=gla_b2,
             gla_norm=gla_norm, w_branch=w_branch, w_out=w_out, norm_ffn=norm_ffn, w_router=w_router,
             w_gate_e=w_gate_e, w_up_e=w_up_e, w_down_e=w_down_e, norm_ple=norm_ple,
             w_ple_gate=w_ple_gate, w_ple_proj=w_ple_proj)
    layers = [_layer_weights(i, w) for i in range(norm_mix.shape[0])]
    gfin = norm_final.reshape(1, -1).astype(F32)
    return (_trunk(x_prompt, p_prompt, layers, gfin), _trunk(x_sample, p_sample, layers, gfin))
```

```python
import dataclasses
import functools

import jax
import jax.numpy as jnp
import numpy as np
from jax import lax
from jax.experimental import pallas as pl
from jax.experimental.pallas import tpu as pltpu
from jax.experimental.pallas import tpu_sc as plsc

F32 = jnp.float32
BF16 = jnp.bfloat16

D_MODEL = 1024
DEPTH = 4
GRID_W = 64
HEAD_DIM = 64
ROPE_THETA = 10000.0
NORM_EPS = 1e-6
CONV_CH = 256
CONV_WIDTH = 31
CONV_HALO = 16
WINDOW = 128
GLA_TAU = 16.0
GLA_CHUNK = 64
GLA_BLOCK = 256
GLA_ROWS = 2
N_EXPERTS = 16
EC_CAPACITY = 2
PLE_DIM = 256
LANES = 128
SUBLANES = 8
VMEM_LIMIT = 56 * 1024 * 1024
LOG2E = 1.4426950408889634
SEL_CHUNK = 512
COMB_WIN = 64
FFN_GROUPS = 2
SC_LANES = 16
GATHER_WINDOW = 128

_QA, _KA, _VA = (0, 256), (256, 384), (384, 512)
_UB = (512, 1024)
_QC, _KC, _VC = (1024, 1280), (1280, 1408), (1408, 1536)
_DD = (1536, 2560)
_ZL = (2560, 2592)
_GATES = 2592
_QPERM = np.concatenate([np.arange(0, 64), np.arange(128, 192), np.arange(64, 128), np.arange(192, 256)])
_HEAD_PERM = np.array([0, 2, 1, 3])


def _cparams(*sem):
    return pltpu.CompilerParams(dimension_semantics=sem, vmem_limit_bytes=VMEM_LIMIT)


def _dot(a, b):
    return jnp.dot(a, b, preferred_element_type=F32)


def _dot_nt(a, b):
    return lax.dot_general(a, b, (((1,), (1,)), ((), ())), preferred_element_type=F32)


def _rms(x, g):
    return x * lax.rsqrt(jnp.mean(x * x, axis=-1, keepdims=True) + NORM_EPS) * g


def _split2(x):
    hi = x.astype(BF16)
    lo = (x - hi.astype(F32)).astype(BF16)
    return hi, lo


def _group_ones(width, group):
    r = lax.broadcasted_iota(jnp.int32, (width, width), 0) // group
    c = lax.broadcasted_iota(jnp.int32, (width, width), 1) // group
    return jnp.where(r == c, 1.0, 0.0).astype(BF16)


def _group_meansq(x, group):
    hi, lo = _split2(x * x)
    ones = _group_ones(x.shape[1], group)
    return (_dot(hi, ones) + _dot(lo, ones)) * (1.0 / group)


def _rope(x, cos, sin_signed, half):
    width = x.shape[1]
    lane = lax.broadcasted_iota(jnp.int32, x.shape, 1)
    from_lo = pltpu.roll(x, half, 1)
    from_hi = pltpu.roll(x, width - half, 1)
    partner = jnp.where((lane & half) != 0, from_lo, from_hi)
    return x * cos + partner * sin_signed


def _proj_kernel(x_ref, g_ref, wa_ref, wb_ref, wc_ref, wd_ref, wz_ref, gq_ref, gk_ref,
                 ca_ref, sa_ref, cc_ref, sc_ref,
                 qa_ref, kat_ref, va_ref, zb_ref, qc_ref, kc_ref, vc_ref, d_ref, zl_ref):
    h = _rms(x_ref[...], g_ref[...]).astype(BF16)
    scale = HEAD_DIM ** -0.5

    ua = _dot(h, wa_ref[...])
    ca, sa = ca_ref[...], sa_ref[...]
    q = ua[:, :256]
    q = q * lax.rsqrt(_group_meansq(q, HEAD_DIM) + NORM_EPS) * gq_ref[...]
    q = _rope(q, jnp.concatenate([ca, ca], axis=1), jnp.concatenate([sa, sa], axis=1), 16)
    qa_ref[...] = (q * (scale * LOG2E)).astype(BF16)
    k = ua[:, 256:384]
    k = k * lax.rsqrt(_group_meansq(k, HEAD_DIM) + NORM_EPS) * gk_ref[...]
    kat_ref[...] = _rope(k, ca, sa, 16).T.astype(BF16)
    va_ref[...] = ua[:, 384:].astype(BF16)

    ub = _dot(h, wb_ref[...])
    zb_ref[...] = (ub[:, :CONV_CH] * jax.nn.sigmoid(ub[:, CONV_CH:])).astype(BF16)

    uc = _dot(h, wc_ref[...])
    cc, sc = cc_ref[...], sc_ref[...]
    qc = _rope(uc[:, :256], jnp.concatenate([cc, cc], axis=1), jnp.concatenate([sc, sc], axis=1), 32)
    qc_ref[...] = (qc * (scale * LOG2E)).astype(BF16)
    kc_ref[...] = _rope(uc[:, 256:384], cc, sc, 32).astype(BF16)
    vc_ref[...] = uc[:, 384:].astype(BF16)

    ud = _dot(h, wd_ref[...])
    d_ref[:, :256] = (ud[:, :256] * scale).astype(BF16)
    d_ref[:, 256:] = ud[:, 256:].astype(BF16)
    zl_ref[...] = _dot(h, wz_ref[...]).astype(BF16)


def _proj(x, lw, tabs, S):
    N = x.shape[0]
    T = min(512, S)
    per_row = S // T
    tok = lambda w: pl.BlockSpec((T, w), lambda i: (i, 0))
    full = lambda a: pl.BlockSpec(a.shape, lambda i: (0,) * a.ndim)
    tab = pl.BlockSpec((T, LANES), lambda i: (i % per_row, 0))
    widths = (256, 128, 128, 256, 256, 128, 128, 1024, 128)
    consts = (lw['g_mix'], lw['wa'], lw['wb'], lw['wc'], lw['wd'], lw['wz'], lw['gq'], lw['gk'])
    return pl.pallas_call(
        _proj_kernel,
        grid=(N // T,),
        in_specs=[tok(D_MODEL)] + [full(a) for a in consts] + [tab] * 4,
        out_specs=[tok(256), pl.BlockSpec((LANES, T), lambda i: (0, i))] + [tok(w) for w in widths[2:]],
        out_shape=[jax.ShapeDtypeStruct((N, 256), BF16), jax.ShapeDtypeStruct((LANES, N), BF16)]
                  + [jax.ShapeDtypeStruct((N, w), BF16) for w in widths[2:]],
        compiler_params=_cparams("parallel"),
        name="proj",
    )(x, *consts, *tabs)


def _stack_heads(qb):
    lane = lax.broadcasted_iota(jnp.int32, qb.shape, 1)
    zero = jnp.zeros_like(qb)
    return jnp.concatenate([jnp.where(lane < HEAD_DIM, qb, zero), jnp.where(lane < HEAD_DIM, zero, qb)], axis=0)


def _unstack_heads(o, T):
    lane = lax.broadcasted_iota(jnp.int32, (T, LANES), 1)
    return jnp.where(lane < HEAD_DIM, o[:T], o[T:])


def _gattn_kernel(q_ref, kt_ref, v_ref, o_ref, *, tk):
    T = q_ref.shape[0]
    S = kt_ref.shape[1]
    lane_v = lax.broadcasted_iota(jnp.int32, (tk, LANES), 1)
    one = jnp.ones((tk, LANES), BF16)
    qs = [_stack_heads(q_ref[:, j * LANES:(j + 1) * LANES]) for j in range(2)]
    m = [jnp.full((2 * T, 1), -jnp.inf, F32) for _ in range(2)]
    acc = [jnp.zeros((2 * T, LANES), F32) for _ in range(2)]
    for c in range(S // tk):
        kt = kt_ref[:, c * tk:(c + 1) * tk]
        v = v_ref[c * tk:(c + 1) * tk, :]
        va = jnp.where(lane_v < HEAD_DIM, v, one)
        vb = jnp.where(lane_v < HEAD_DIM, one, v)
        for j in range(2):
            s = _dot(qs[j], kt)
            m_new = jnp.maximum(m[j], jnp.max(s, axis=1, keepdims=True))
            alpha = jnp.exp2(m[j] - m_new)
            p = jnp.exp2(s - m_new).astype(BF16)
            pv = jnp.concatenate([_dot(p[:T], va), _dot(p[T:], vb)], axis=0)
            acc[j] = alpha * acc[j] + pv
            m[j] = m_new
    lane_o = lax.broadcasted_iota(jnp.int32, (T, LANES), 1)
    for j in range(2):
        a, b = acc[j][:T], acc[j][T:]
        o = jnp.where(lane_o < HEAD_DIM, a / pltpu.roll(a, HEAD_DIM, 1), b / pltpu.roll(b, HEAD_DIM, 1))
        o_ref[:, j * LANES:(j + 1) * LANES] = o.astype(BF16)


def _gattn(q, kt, v):
    B, S, _ = q.shape
    T = min(512, S)
    tk = min(1024, S)
    return pl.pallas_call(
        functools.partial(_gattn_kernel, tk=tk),
        grid=(B, S // T),
        in_specs=[pl.BlockSpec((None, T, 256), lambda b, i: (b, i, 0)),
                  pl.BlockSpec((LANES, S), lambda b, i: (0, b)),
                  pl.BlockSpec((None, S, LANES), lambda b, i: (b, 0, 0))],
        out_specs=pl.BlockSpec((None, T, 256), lambda b, i: (b, i, 0)),
        out_shape=jax.ShapeDtypeStruct((B, S, 256), BF16),
        compiler_params=_cparams("parallel", "parallel"),
        name="gattn",
    )(q, kt, v)


def _wattn_kernel(sink_ref, q_ref, k_ref, v_ref, bias_ref, o_ref, *, kw):
    T = q_ref.shape[0]
    S = k_ref.shape[0]
    i = pl.program_id(1)
    start = pl.multiple_of(jnp.clip(i * T - WINDOW, 0, S - kw), WINDOW)
    kwin = k_ref[pl.ds(start, kw), :]
    vwin = v_ref[pl.ds(start, kw), :]
    band = bias_ref[(i * T - start) // WINDOW]
    bias = jnp.concatenate([band, band], axis=0)
    lane_v = lax.broadcasted_iota(jnp.int32, (kw, LANES), 1)
    one = jnp.ones((kw, LANES), BF16)
    va = jnp.where(lane_v < HEAD_DIM, vwin, one)
    vb = jnp.where(lane_v < HEAD_DIM, one, vwin)
    first = lax.broadcasted_iota(jnp.int32, (2 * T, 1), 0) < T
    lane_o = lax.broadcasted_iota(jnp.int32, (T, LANES), 1)
    for j in range(2):
        qst = _stack_heads(q_ref[:, j * LANES:(j + 1) * LANES])
        s = _dot_nt(qst, kwin) + bias
        sk = jnp.where(first, sink_ref[2 * j], sink_ref[2 * j + 1])
        m = jnp.maximum(jnp.max(s, axis=1, keepdims=True), sk)
        p = jnp.exp2(s - m).astype(BF16)
        sunk = jnp.exp2(sk - m)
        a = _dot(p[:T], va)
        b = _dot(p[T:], vb)
        o = jnp.where(lane_o < HEAD_DIM, a / (pltpu.roll(a, HEAD_DIM, 1) + sunk[:T]),
                      b / (pltpu.roll(b, HEAD_DIM, 1) + sunk[T:]))
        o_ref[:, j * LANES:(j + 1) * LANES] = o.astype(BF16)


def _band_bias(T, kw):
    r = np.arange(T)[None, :, None]
    c = np.arange(kw)[None, None, :]
    off = np.arange(3)[:, None, None] * WINDOW
    return jnp.asarray(np.where(np.abs(c - off - r) <= WINDOW, 0.0, -np.inf), F32)


def _wattn(q, k, v, sink):
    B, S, _ = q.shape
    T = min(256, S - 2 * WINDOW) if S > 2 * WINDOW else S
    kw = min(T + 2 * WINDOW, S)
    bias = _band_bias(T, kw)
    return pl.pallas_call(
        functools.partial(_wattn_kernel, kw=kw),
        grid=(B, S // T),
        in_specs=[pl.BlockSpec(memory_space=pltpu.SMEM),
                  pl.BlockSpec((None, T, 256), lambda b, i: (b, i, 0)),
                  pl.BlockSpec((None, S, LANES), lambda b, i: (b, 0, 0)),
                  pl.BlockSpec((None, S, LANES), lambda b, i: (b, 0, 0)),
                  pl.BlockSpec(bias.shape, lambda b, i: (0, 0, 0))],
        out_specs=pl.BlockSpec((None, T, 256), lambda b, i: (b, i, 0)),
        out_shape=jax.ShapeDtypeStruct((B, S, 256), BF16),
        compiler_params=_cparams("parallel", "parallel"),
        name="wattn",
    )(sink, q, k, v, bias)


def _conv_kernel(zp_ref, zc_ref, zn_ref, w_ref, b_ref, g_ref, beta_ref, o_ref, buf_ref, sh_ref):
    T = zc_ref.shape[0]
    i = pl.program_id(1)
    H = CONV_HALO
    keep_prev = jnp.where(i > 0, 1.0, 0.0)
    keep_next = jnp.where(i < pl.num_programs(1) - 1, 1.0, 0.0)
    buf_ref[0:H, :] = zp_ref[...].astype(F32) * keep_prev
    buf_ref[H:H + T, :] = zc_ref[...].astype(F32)
    buf_ref[H + T:, :] = zn_ref[...].astype(F32) * keep_next
    span = sh_ref.shape[1]
    for r in range(1, SUBLANES):
        sh_ref[r] = buf_ref[r:r + span, :]
    acc = jnp.zeros((T, CONV_CH), F32)
    off = H - CONV_WIDTH // 2
    for tap in range(CONV_WIDTH):
        r = (off + tap) % SUBLANES
        a = off + tap - r
        rows = buf_ref[a:a + T, :] if r == 0 else sh_ref[r, a:a + T, :]
        acc = acc + rows * w_ref[tap:tap + 1, :]
    z = acc + b_ref[...]
    mu = jnp.mean(z, axis=-1, keepdims=True)
    zc = z - mu
    var = jnp.mean(zc * zc, axis=-1, keepdims=True)
    y = zc * lax.rsqrt(var + NORM_EPS) * g_ref[...] + beta_ref[...]
    o_ref[...] = (y * jax.nn.sigmoid(y)).astype(BF16)


def _conv(z, lw):
    B, S, _ = z.shape
    T = min(512, S)
    H = CONV_HALO
    per = T // H
    last = S // H - 1
    full = lambda a: pl.BlockSpec(a.shape, lambda b, i: (0,) * a.ndim)
    consts = (lw['conv_w'], lw['conv_b'], lw['ln_g'], lw['ln_b'])
    return pl.pallas_call(
        _conv_kernel,
        grid=(B, S // T),
        in_specs=[pl.BlockSpec((None, H, CONV_CH), lambda b, i: (b, jnp.maximum(i * per - 1, 0), 0)),
                  pl.BlockSpec((None, T, CONV_CH), lambda b, i: (b, i, 0)),
                  pl.BlockSpec((None, H, CONV_CH), lambda b, i: (b, jnp.minimum((i + 1) * per, last), 0))]
                 + [full(a) for a in consts],
        out_specs=pl.BlockSpec((None, T, CONV_CH), lambda b, i: (b, i, 0)),
        out_shape=jax.ShapeDtypeStruct((B, S, CONV_CH), BF16),
        scratch_shapes=[pltpu.VMEM((T + 2 * H, CONV_CH), F32),
                        pltpu.VMEM((SUBLANES, T + 2 * H - SUBLANES, CONV_CH), F32)],
        compiler_params=_cparams("parallel", "parallel"),
        name="conv",
    )(z, z, z, *consts)


def _gla_block(d_ref, zl_ref, w2, b2, tri, att_mask, st_ref, reverse):
    TB = d_ref.shape[0]
    L = GLA_CHUNK
    nch = TB // L

    pre = _dot(zl_ref[...], w2) + b2
    la = (jnp.minimum(pre, 0.0) - jnp.log(1.0 + jnp.exp(-jnp.abs(pre)))) * (1.0 / GLA_TAU)

    hi = la.astype(BF16)
    r1 = la - hi.astype(F32)
    mid = r1.astype(BF16)
    lo = (r1 - mid.astype(F32)).astype(BF16)
    cums = _dot(tri, jnp.concatenate([hi, mid, lo], axis=1))
    b = cums[:, :256] + cums[:, 256:512] + cums[:, 512:]
    mid_row = L // 2 if reverse else L // 2 - 1
    last_row = 0 if reverse else L - 1
    per_chunk = lambda row: jnp.concatenate(
        [jnp.broadcast_to(b[ci * L + row:ci * L + row + 1], (L, 256)) for ci in range(nch)], axis=0)
    bmid, blast = per_chunk(mid_row), per_chunk(last_row)

    q = d_ref[:, 0:256].astype(F32)
    k = d_ref[:, 256:512].astype(F32)
    v = d_ref[:, 512:768]
    qt = (q * jnp.exp(b - bmid)).astype(BF16)
    kt = (k * jnp.exp(bmid - b)).astype(BF16)
    qe = (q * jnp.exp(b)).astype(BF16)
    kl = (k * jnp.exp(blast - b)).astype(BF16)
    dec = jnp.exp(blast)

    rr = lax.broadcasted_iota(jnp.int32, (LANES, LANES), 0) // HEAD_DIM
    cc = lax.broadcasted_iota(jnp.int32, (LANES, LANES), 1) // HEAD_DIM
    head_diag = rr == cc
    rowid = lax.broadcasted_iota(jnp.int32, (TB, LANES), 0) // L
    order = range(nch - 1, -1, -1) if reverse else range(nch)

    out = []
    for p in range(2):
        ls = slice(p * LANES, (p + 1) * LANES)
        vb = v[:, ls]
        att = _dot_nt(_stack_heads(qt[:, ls]), kt[:, ls]) * att_mask
        o_intra = _unstack_heads(_dot(att.astype(BF16), vb), TB)
        vt = vb.astype(F32).T.astype(BF16)
        klb = kl[:, ls]
        kv_t = [jnp.where(head_diag, _dot(vt, jnp.where(rowid == ci, klb, jnp.zeros_like(klb))), 0.0)
                for ci in range(nch)]
        st = st_ref[p]
        o_inter = [None] * nch
        for ci in order:
            o_inter[ci] = _dot_nt(qe[ci * L:(ci + 1) * L, ls], st.astype(BF16))
            st = st * dec[ci * L:ci * L + 1, ls] + kv_t[ci]
        st_ref[p] = st
        out.append(o_intra + jnp.concatenate(o_inter, axis=0))
    return out


def _gla_kernel(df_ref, zf_ref, db_ref, zb_ref, w2_ref, b2_ref, tri_ref, mask_ref, of_ref, ob_ref, st_ref):
    @pl.when(pl.program_id(1) == 0)
    def _():
        st_ref[...] = jnp.zeros(st_ref.shape, F32)

    for r in range(df_ref.shape[0]):
        fwd = _gla_block(df_ref.at[r], zf_ref.at[r], w2_ref[0], b2_ref[0], tri_ref[0], mask_ref[0],
                         st_ref.at[r, 0], False)
        bwd = _gla_block(db_ref.at[r], zb_ref.at[r], w2_ref[1], b2_ref[1], tri_ref[1], mask_ref[1],
                         st_ref.at[r, 1], True)
        for p in range(2):
            of_ref[r, :, p * LANES:(p + 1) * LANES] = fwd[p]
            ob_ref[r, :, p * LANES:(p + 1) * LANES] = bwd[p]


def _gla_masks(TB):
    L = GLA_CHUNK
    r = np.arange(TB)[:, None]
    c = np.arange(TB)[None, :]
    same = (r // L) == (c // L)
    tri = np.stack([same & (c <= r), same & (c >= r)]).astype(np.float32)
    return jnp.asarray(tri, BF16), jnp.asarray(np.concatenate([tri, tri], axis=1), F32)


def _gla(d, zl, w2, b2):
    B, S, _ = d.shape
    TB = min(GLA_BLOCK, S)
    nb = S // TB
    R = GLA_ROWS if B % GLA_ROWS == 0 else 1
    fblk = lambda b, i: (b, i, 0)
    bblk = lambda b, i: (b, nb - 1 - i, 0)
    full = lambda a: pl.BlockSpec(a.shape, lambda b, i: (0,) * a.ndim)
    tri, mask = _gla_masks(TB)
    out = jax.ShapeDtypeStruct((B, S, 256), F32)
    return pl.pallas_call(
        _gla_kernel,
        grid=(B // R, nb),
        in_specs=[pl.BlockSpec((R, TB, 1024), fblk), pl.BlockSpec((R, TB, LANES), fblk),
                  pl.BlockSpec((R, TB, 1024), bblk), pl.BlockSpec((R, TB, LANES), bblk),
                  full(w2), full(b2), full(tri), full(mask)],
        out_specs=[pl.BlockSpec((R, TB, 256), fblk), pl.BlockSpec((R, TB, 256), bblk)],
        out_shape=[out, out],
        scratch_shapes=[pltpu.VMEM((R, 2, 2, LANES, LANES), F32)],
        compiler_params=_cparams("parallel", "arbitrary"),
        name="gla",
    )(d, zl, d, zl, w2, b2, tri, mask)


def _merge_kernel(x_ref, oa_ref, ob_ref, oc_ref, ogf_ref, ogb_ref, rd_ref, gmix_ref, wg_ref, wbr_ref, wout_ref,
                  gn_ref, gffn_ref, wrh_ref, wrl_ref, x1_ref, h2_ref, aff_ref):
    x = x_ref[...]
    h = _rms(x, gmix_ref[...]).astype(BF16)
    og = ogf_ref[...] + ogb_ref[...]
    od = og * lax.rsqrt(_group_meansq(og, HEAD_DIM) + NORM_EPS) * gn_ref[...]
    rd = rd_ref[...].astype(F32)
    od = (od * (rd * jax.nn.sigmoid(rd))).astype(BF16)
    branches = (oa_ref[...], ob_ref[...], oc_ref[...], od)
    merged = jnp.zeros(x.shape, F32)
    for i, o in enumerate(branches):
        gate = jax.nn.sigmoid(_dot(h, wg_ref[:, i * D_MODEL:(i + 1) * D_MODEL]))
        merged = merged + gate * _dot(o, wbr_ref[i])
    x1 = x + _dot(merged.astype(BF16), wout_ref[...])
    x1_ref[...] = x1
    h2 = _rms(x1, gffn_ref[...])
    h2_ref[...] = h2
    hh, hl = _split2(h2)
    logits = _dot(hh, wrh_ref[...]) + _dot(hh, wrl_ref[...]) + _dot(hl, wrh_ref[...])
    lt = logits.T[:N_EXPERTS]
    e = jnp.exp(lt - jnp.max(lt, axis=0, keepdims=True))
    aff_ref[...] = e / jnp.sum(e, axis=0, keepdims=True)


def _merge(x, oa, ob, oc, ogf, ogb, d, lw):
    N = x.shape[0]
    T = min(256, N)
    tok = lambda w: pl.BlockSpec((T, w), lambda i: (i, 0))
    full = lambda a: pl.BlockSpec(a.shape, lambda i: (0,) * a.ndim)
    consts = (lw['g_mix'], lw['wg'], lw['wbr'], lw['wout'], lw['gn'], lw['g_ffn'], lw['wr_hi'], lw['wr_lo'])
    return pl.pallas_call(
        _merge_kernel,
        grid=(N // T,),
        in_specs=[tok(D_MODEL), tok(256), tok(256), tok(256), tok(256), tok(256),
                  pl.BlockSpec((T, 256), lambda i: (i, 3))] + [full(a) for a in consts],
        out_specs=[tok(D_MODEL), tok(D_MODEL), pl.BlockSpec((N_EXPERTS, T), lambda i: (0, i))],
        out_shape=[jax.ShapeDtypeStruct((N, D_MODEL), F32), jax.ShapeDtypeStruct((N, D_MODEL), F32),
                   jax.ShapeDtypeStruct((N_EXPERTS, N), F32)],
        compiler_params=_cparams("parallel"),
        name="merge",
    )(x, oa, ob, oc, ogf, ogb, d, *consts)


def _select_kernel(aff_ref, rank_ref, incl_ref, *, cap):
    E, N = aff_ref.shape
    CH = min(N, 4096)
    SC = min(N, SEL_CHUNK)

    def count(pred):
        def body(c, acc):
            start = pl.multiple_of(c * CH, CH)
            bits = lax.bitcast_convert_type(aff_ref[:, pl.ds(start, CH)], jnp.int32)
            tok = start + lax.broadcasted_iota(jnp.int32, (E, CH), 1)
            return acc + jnp.sum(pred(bits, tok), axis=1, keepdims=True)
        return lax.fori_loop(0, N // CH, body, jnp.zeros((E, 1), F32))

    def value_step(_, lohi):
        lo, hi = lohi
        mid = lo + ((hi - lo) >> 1)
        ok = count(lambda b, t: jnp.where(b >= mid, 1.0, 0.0)) >= cap
        return jnp.where(ok, mid, lo), jnp.where(ok, hi, mid)

    thr, _ = lax.fori_loop(0, 32, value_step,
                           (jnp.zeros((E, 1), jnp.int32), jnp.full((E, 1), 0x7F800000, jnp.int32)))
    need = cap - count(lambda b, t: jnp.where(b > thr, 1.0, 0.0))

    def tie_step(_, lohi):
        lo, hi = lohi
        mid = lo + ((hi - lo) >> 1)
        ok = count(lambda b, t: jnp.where(b == thr, jnp.where(t <= mid, 1.0, 0.0), 0.0)) >= need
        return jnp.where(ok, lo, mid), jnp.where(ok, mid, hi)

    on_thr = count(lambda b, t: jnp.where(b == thr, 1.0, 0.0))
    all_in = jnp.full((E, 1), N - 1, jnp.int32)
    cut = lax.cond(
        jnp.max(on_thr - need) <= 0.0,
        lambda: all_in,
        lambda: lax.fori_loop(0, int(np.ceil(np.log2(N))) + 1, tie_step,
                              (jnp.full((E, 1), -1, jnp.int32), all_in))[1])

    r = lax.broadcasted_iota(jnp.int32, (SC, SC), 0)
    c = lax.broadcasted_iota(jnp.int32, (SC, SC), 1)
    tri = jnp.where(r <= c, 1.0, 0.0).astype(BF16)

    def emit(ci, carry):
        start = pl.multiple_of(ci * SC, SC)
        bits = lax.bitcast_convert_type(aff_ref[:, pl.ds(start, SC)], jnp.int32)
        tok = start + lax.broadcasted_iota(jnp.int32, (E, SC), 1)
        picked = jnp.where(bits > thr, 1.0, jnp.where(bits == thr, jnp.where(tok <= cut, 1.0, 0.0), 0.0))
        inc = carry + _dot(picked.astype(BF16), tri)
        incl_ref[:, pl.ds(start, SC)] = inc
        rank_ref[:, pl.ds(start, SC)] = jnp.where(picked > 0.0, inc - 1.0, -1.0).astype(jnp.int32)
        return inc[:, SC - 1:SC]

    lax.fori_loop(0, N // SC, emit, jnp.zeros((E, 1), F32))


def _select(aff_t, cap):
    E, N = aff_t.shape
    full = pl.BlockSpec((E, N), lambda i: (0, 0))
    return pl.pallas_call(
        functools.partial(_select_kernel, cap=cap),
        grid=(1,),
        in_specs=[full],
        out_specs=[full, full],
        out_shape=[jax.ShapeDtypeStruct((E, N), jnp.int32), jax.ShapeDtypeStruct((E, N), F32)],
        compiler_params=_cparams("arbitrary"),
        name="select",
    )(aff_t)


def _sc_params():
    cp = pltpu.CompilerParams()
    if "needs_layout_passes" in pltpu.CompilerParams.__dataclass_fields__:
        cp = dataclasses.replace(cp, needs_layout_passes=False)
    return cp


def _compact(rank, aff_t, cap):
    E, N = rank.shape
    CH = min(N, 4096)
    mesh = plsc.VectorSubcoreMesh(core_axis_name="c", subcore_axis_name="s")

    @pl.kernel(out_type=(jax.ShapeDtypeStruct((E * cap,), jnp.int32), jax.ShapeDtypeStruct((E * cap,), F32)),
               mesh=mesh,
               scratch_types=[pltpu.VMEM((CH,), jnp.int32), pltpu.VMEM((CH,), F32),
                              pltpu.VMEM((cap,), jnp.int32), pltpu.VMEM((cap,), F32)],
               compiler_params=_sc_params())
    def compact(rank_hbm, aff_hbm, idx_hbm, gate_hbm, rbuf, abuf, ibuf, gbuf):
        wid = lax.axis_index("s") * mesh.num_cores + lax.axis_index("c")

        @pl.when(wid < E)
        def _():
            @pl.loop(0, N // CH)
            def _(c):
                base = wid * N + c * CH
                pltpu.sync_copy(rank_hbm.at[pl.ds(base, CH)], rbuf)
                pltpu.sync_copy(aff_hbm.at[pl.ds(base, CH)], abuf)

                @pl.loop(0, CH, step=SC_LANES)
                def _(i):
                    r = rbuf[pl.ds(i, SC_LANES)]
                    picked = r >= 0
                    slot = jnp.where(picked, r, 0)
                    tok = c * CH + i + lax.iota(jnp.int32, SC_LANES)
                    plsc.store_scatter(ibuf, [slot], tok, mask=picked)
                    plsc.store_scatter(gbuf, [slot], abuf[pl.ds(i, SC_LANES)], mask=picked)

            pltpu.sync_copy(ibuf, idx_hbm.at[pl.ds(wid * cap, cap)])
            pltpu.sync_copy(gbuf, gate_hbm.at[pl.ds(wid * cap, cap)])

    return compact(rank.reshape(E * N), aff_t.reshape(E * N))


def _gather_rows(x, idx):
    R = idx.shape[0]
    win = GATHER_WINDOW
    mesh = plsc.VectorSubcoreMesh(core_axis_name="c", subcore_axis_name="s")

    @pl.kernel(out_type=jax.ShapeDtypeStruct((R, LANES), x.dtype), mesh=mesh)
    def gather(x_hbm, i_hbm, o_hbm):
        def body(i_vmem, o_vmem):
            pltpu.sync_copy(x_hbm.at[i_vmem.at[0]], o_vmem)

        pltpu.emit_pipeline(
            body,
            grid=(R // win,),
            in_specs=[pl.BlockSpec((1, win), lambda i: (0, i))],
            out_specs=[pl.BlockSpec((win, LANES), lambda i: (i, 0))],
            core_axis_name=("c", "s"),
            dimension_semantics=(pltpu.PARALLEL,),
            trace_scopes=False,
        )(i_hbm, o_hbm)

    return gather(x, idx.reshape(1, R))


def _ffn_kernel(x_ref, wg_ref, wu_ref, wd_ref, gate_ref, y_ref, w_ref):
    @pl.when(pl.program_id(1) == 0)
    def _():
        w_ref[0] = wg_ref[...].astype(BF16)
        w_ref[1] = wu_ref[...].astype(BF16)
        w_ref[2] = wd_ref[...].astype(BF16)

    x = jnp.concatenate([x_ref[k] for k in range(D_MODEL // LANES)], axis=1).astype(BF16)
    g = _dot(x, w_ref[0])
    hid = (g * jax.nn.sigmoid(g)) * _dot(x, w_ref[1])
    y_ref[...] = (_dot(hid.astype(BF16), w_ref[2]) * gate_ref[...]).astype(BF16)


def _ffn(xe, gate, lw, cap, first):
    E = N_EXPERTS // FFN_GROUPS
    M = min(512, cap)
    per = cap // M
    layer = lw['layer']
    wspec = pl.BlockSpec((None, None, D_MODEL, D_MODEL), lambda e, s: (layer, first + e, 0, 0))
    return pl.pallas_call(
        _ffn_kernel,
        grid=(E, per),
        in_specs=[pl.BlockSpec((D_MODEL // LANES, M, LANES), lambda e, s: (0, e * per + s, 0)), wspec, wspec, wspec,
                  pl.BlockSpec((M, 1), lambda e, s: (e * per + s, 0))],
        out_specs=pl.BlockSpec((M, D_MODEL), lambda e, s: (e * per + s, 0)),
        out_shape=jax.ShapeDtypeStruct((E * cap, D_MODEL), BF16),
        scratch_shapes=[pltpu.VMEM((3, D_MODEL, D_MODEL), BF16)],
        compiler_params=_cparams("parallel", "arbitrary"),
        name="ffn",
    )(xe, lw['we_g'], lw['we_u'], lw['we_d'], gate)


def _combine_kernel(starts_ref, rounds_ref, x_ref, p_ref, rank_ref, *rest, cap, final):
    ye_hbm = rest[:FFN_GROUPS]
    g_ref, wgate_ref, wproj_ref, gfin_ref, spread_ref, o_ref, stage_ref, sem_ref = rest[FFN_GROUPS:]
    j = pl.program_id(0)
    nj = pl.num_programs(0)
    T = x_ref.shape[0]
    E, W = N_EXPERTS, COMB_WIN
    EG = E // FFN_GROUPS
    slot = j % 2

    def window_start(tile, rnd, e):
        return pl.multiple_of(jnp.minimum(starts_ref[tile * E + e] + rnd * W, EG * cap - W), 16)

    def window_copy(tile, rnd, e, sl):
        return pltpu.make_async_copy(ye_hbm[e // EG].at[pl.ds(window_start(tile, rnd, e), W), :],
                                     stage_ref.at[sl, pl.ds(e * W, W), :], sem_ref.at[sl])

    def start_all(tile, rnd, sl):
        for e in range(E):
            window_copy(tile, rnd, e, sl).start()

    def wait_all(tile, rnd, sl):
        for e in range(E):
            window_copy(tile, rnd, e, sl).wait()

    @pl.when(j == 0)
    def _():
        start_all(0, 0, 0)

    @pl.when(j + 1 < nj)
    def _():
        start_all(j + 1, 0, 1 - slot)

    rk = rank_ref[...]
    erow = lax.broadcasted_iota(jnp.int32, (E, 1), 0)
    tgt = rk + (erow % EG) * cap

    def per_expert(fn):
        v = jnp.zeros((E, 1), jnp.int32)
        for e in range(E):
            v = jnp.where(erow == e, fn(e), v)
        return v

    def placed(rnd):
        staged_from = per_expert(lambda e: window_start(j, rnd, e))
        fresh = per_expert(lambda e: starts_ref[j * E + e] + rnd * W)
        loc = jnp.where(rk >= 0, jnp.where(tgt >= fresh, tgt - staged_from, -1), -1)
        loc = jnp.where(loc < W, loc, -1).astype(F32)
        lhs = jnp.concatenate([loc, jnp.ones((1, T), F32), jnp.zeros((LANES - E - 1, T), F32)], axis=0).T
        spread = _dot(lhs.astype(BF16), spread_ref[...])
        return _dot(jnp.where(spread == 0.0, 1.0, 0.0).astype(BF16), stage_ref[slot])

    wait_all(j, 0, slot)
    moe = placed(0)

    def extra(rnd, acc):
        start_all(j, rnd, slot)
        wait_all(j, rnd, slot)
        return acc + placed(rnd)

    moe = lax.fori_loop(1, rounds_ref[j], extra, moe)

    x = x_ref[...] + moe
    h = _rms(x, g_ref[...]).astype(BF16)
    gate = jax.nn.sigmoid(_dot(h, wgate_ref[...]))
    y = x + gate * _dot(p_ref[...].astype(BF16), wproj_ref[...])
    if final:
        y = _rms(y, gfin_ref[...])
    o_ref[...] = y


def _combine(x1, p, rank, incl, yes, lw, gfin, cap, final):
    N = x1.shape[0]
    E, W = N_EXPERTS, COMB_WIN
    T = min(256, N)
    nt = N // T
    ends = incl[:, T - 1::T].astype(jnp.int32)
    begins = jnp.concatenate([jnp.zeros((E, 1), jnp.int32), ends[:, :-1]], axis=1)
    base = (jnp.arange(E, dtype=jnp.int32) % (E // FFN_GROUPS))[:, None] * cap
    aligned = (base + begins) // 16 * 16
    rounds = jnp.maximum(1, jnp.max((base + ends - aligned + W - 1) // W, axis=0)).astype(jnp.int32)
    starts = aligned.T.reshape(-1)

    tok = lambda w: pl.BlockSpec((T, w), lambda i, *_: (i, 0))
    full = lambda a: pl.BlockSpec(a.shape, lambda i, *_: (0,) * a.ndim)
    col = np.arange(E * W)[None, :]
    row = np.arange(LANES)[:, None]
    spread = np.where(row == col // W, 1.0, 0.0) + np.where(row == E, -(col % W), 0.0)
    consts = (lw['g_ple'], lw['w_pg'], lw['w_pp'], gfin, jnp.asarray(spread, BF16))
    grid_spec = pltpu.PrefetchScalarGridSpec(
        num_scalar_prefetch=2,
        grid=(nt,),
        in_specs=[tok(D_MODEL), tok(PLE_DIM), pl.BlockSpec((E, T), lambda i, *_: (0, i))]
                 + [pl.BlockSpec(memory_space=pl.ANY)] * FFN_GROUPS + [full(a) for a in consts],
        out_specs=tok(D_MODEL),
        scratch_shapes=[pltpu.VMEM((2, E * W, D_MODEL), BF16), pltpu.SemaphoreType.DMA((2,))],
    )
    return pl.pallas_call(
        functools.partial(_combine_kernel, cap=cap, final=final),
        grid_spec=grid_spec,
        out_shape=jax.ShapeDtypeStruct((N, D_MODEL), F32),
        compiler_params=_cparams("arbitrary"),
        name="combine",
    )(starts, rounds, x1, p, rank, *yes, *consts)


def _rope_tables(S):
    lane = np.arange(LANES)
    d = lane % HEAD_DIM
    t = jnp.arange(S)
    inv_a = ROPE_THETA ** (-jnp.arange(0, HEAD_DIM // 2, 2, dtype=F32) / (HEAD_DIM // 2))
    pos_a = jnp.where((d // 32 == 0)[None, :], (t // GRID_W)[:, None], (t % GRID_W)[:, None]).astype(F32)
    ang_a = pos_a * inv_a[d % 16][None, :]
    sign_a = jnp.where(d % 32 < 16, -1.0, 1.0)[None, :]
    inv_c = ROPE_THETA ** (-jnp.arange(0, HEAD_DIM, 2, dtype=F32) / HEAD_DIM)
    ang_c = t.astype(F32)[:, None] * inv_c[d % 32][None, :]
    sign_c = jnp.where(d < 32, -1.0, 1.0)[None, :]
    return (jnp.cos(ang_a), jnp.sin(ang_a) * sign_a, jnp.cos(ang_c), jnp.sin(ang_c) * sign_c)


def _layer_weights(i, w):
    w_in = w['w_in'][i]
    cols = lambda r: w_in[:, r[0]:r[1]]
    bf = lambda a: a.astype(BF16)
    row = lambda a: a.reshape(1, -1).astype(F32)
    wz = jnp.pad(cols(_ZL), ((0, 0), (0, LANES - (_ZL[1] - _ZL[0]))))
    w2 = w['gla_w2'][i]
    w2p = jnp.zeros((2, LANES, 256), F32).at[0, 0:16].set(w2[0]).at[1, 16:32].set(w2[1])
    wr = jnp.pad(w['w_router'][i], ((0, 0), (0, LANES - N_EXPERTS)))
    wr_hi = wr.astype(BF16)
    wbr = w['w_branch'][i]
    wbr = jnp.stack([wbr[0][_QPERM], wbr[1], wbr[2][_QPERM], wbr[3]])
    return dict(
        g_mix=row(w['norm_mix'][i]),
        wa=bf(jnp.concatenate([cols(_QA)[:, _QPERM], cols(_KA), cols(_VA)], axis=1)),
        wb=bf(cols(_UB)),
        wc=bf(jnp.concatenate([cols(_QC)[:, _QPERM], cols(_KC), cols(_VC)], axis=1)),
        wd=bf(cols(_DD)),
        wz=bf(wz),
        gq=row(jnp.tile(w['qk_norm'][i, 0], 4)),
        gk=row(jnp.tile(w['qk_norm'][i, 1], 2)),
        sink=w['sink_logit'][i][_HEAD_PERM].astype(F32) * LOG2E,
        conv_w=w['conv_dw'][i].astype(F32),
        conv_b=row(w['conv_dw_b'][i]), ln_g=row(w['conv_ln_g'][i]), ln_b=row(w['conv_ln_b'][i]),
        w2=bf(w2p), b2=w['gla_b2'][i].reshape(2, 1, 256).astype(F32),
        gn=row(jnp.tile(w['gla_norm'][i], 4)),
        wg=bf(w_in[:, _GATES:]), wbr=bf(wbr), wout=bf(w['w_out'][i]),
        g_ffn=row(w['norm_ffn'][i]),
        wr_hi=wr_hi, wr_lo=(wr - wr_hi.astype(F32)).astype(BF16),
        layer=i, we_g=w['w_gate_e'], we_u=w['w_up_e'], we_d=w['w_down_e'],
        g_ple=row(w['norm_ple'][i]), w_pg=bf(w['w_ple_gate'][i]), w_pp=bf(w['w_ple_proj'][i]),
    )


def _trunk(x3, p4, layers, gfin):
    B, S, _ = x3.shape
    N = B * S
    tabs = _rope_tables(S)
    x = x3.reshape(N, D_MODEL)
    cap = max(1, EC_CAPACITY * N // N_EXPERTS)
    for i, lw in enumerate(layers):
        qa, kat, va, zb, qc, kc, vc, d, zl = _proj(x, lw, tabs, S)
        b3 = lambda a: a.reshape(B, S, a.shape[-1])
        oa = _gattn(b3(qa), kat, b3(va))
        ob = _conv(b3(zb), lw)
        oc = _wattn(b3(qc), b3(kc), b3(vc), lw['sink'])
        ogf, ogb = _gla(b3(d), b3(zl), lw['w2'], lw['b2'])
        flat = lambda a: a.reshape(N, a.shape[-1])
        x1, h2, aff_t = _merge(x, flat(oa), flat(ob), flat(oc), flat(ogf), flat(ogb), d, lw)
        rank, incl = _select(aff_t, cap)
        idx, gate = _compact(rank, aff_t, cap)
        sub = D_MODEL // LANES
        h2_rows = h2.reshape(N // SUBLANES, SUBLANES, sub, LANES).transpose(0, 2, 1, 3).reshape(N * sub, LANES)
        piece = jnp.arange(sub, dtype=jnp.int32)[:, None]
        rows = ((idx // SUBLANES)[None, :] * sub + piece) * SUBLANES + (idx % SUBLANES)[None, :]
        group = N_EXPERTS // FFN_GROUPS * cap
        yes = []
        for g in range(FFN_GROUPS):
            sl = slice(g * group, (g + 1) * group)
            xe = _gather_rows(h2_rows, rows[:, sl].reshape(-1)).reshape(sub, -1, LANES)
            yes.append(_ffn(xe, gate[sl].reshape(-1, 1), lw, cap, g * (N_EXPERTS // FFN_GROUPS)))
        x = _combine(x1, p4[i].reshape(N, PLE_DIM), rank, incl, yes, lw, gfin, cap, i == len(layers) - 1)
    return x.reshape(B, S, D_MODEL)


def kernel(x_prompt, x_sample, p_prompt, p_sample, norm_mix, w_in, qk_norm, sink_logit, conv_dw, conv_dw_b,
           conv_ln_g, conv_ln_b, gla_w2, gla_b2, gla_norm, w_branch, w_out, norm_ffn, w_router, w_gate_e,
           w_up_e, w_down_e, norm_ple, w_ple_gate, w_ple_proj, norm_final):
    w = dict(norm_mix=norm_mix, w_in=w_in, qk_norm=qk_norm, sink_logit=sink_logit, conv_dw=conv_dw,
             conv_dw_b=conv_dw_b, conv_ln_g=conv_ln_g, conv_ln_b=conv_ln_b, gla_w2=gla_w2, gla_b2=gla_b2,
             gla_norm=gla_norm, w_branch=w_branch, w_out=w_out, norm_ffn=norm_ffn, w_router=w_router,
             w_gate_e=w_gate_e, w_up_e=w_up_e, w_down_e=w_down_e, norm_ple=norm_ple,
             w_ple_gate=w_ple_gate, w_ple_proj=w_ple_proj)
    layers = [_layer_weights(i, w) for i in range(norm_mix.shape[0])]
    gfin = norm_final.reshape(1, -1).astype(F32)
    return (_trunk(x_prompt, p_prompt, layers, gfin), _trunk(x_sample, p_sample, layers, gfin))
```

```python
import dataclasses
import functools

import jax
import jax.numpy as jnp
import numpy as np
from jax import lax
from jax.experimental import pallas as pl
from jax.experimental.pallas import tpu as pltpu
from jax.experimental.pallas import tpu_sc as plsc

F32 = jnp.float32
BF16 = jnp.bfloat16

D_MODEL = 1024
DEPTH = 4
GRID_W = 64
HEAD_DIM = 64
ROPE_THETA = 10000.0
NORM_EPS = 1e-6
CONV_CH = 256
CONV_WIDTH = 31
CONV_HALO = 16
WINDOW = 128
GLA_TAU = 16.0
GLA_CHUNK = 64
GLA_BLOCK = 256
GLA_ROWS = 2
N_EXPERTS = 16
EC_CAPACITY = 2
PLE_DIM = 256
LANES = 128
SUBLANES = 8
VMEM_LIMIT = 56 * 1024 * 1024
LOG2E = 1.4426950408889634
SEL_CHUNK = 512
COMB_WIN = 64
FFN_GROUPS = 2
SC_LANES = 16
GATHER_WINDOW = 128

_QA, _KA, _VA = (0, 256), (256, 384), (384, 512)
_UB = (512, 1024)
_QC, _KC, _VC = (1024, 1280), (1280, 1408), (1408, 1536)
_DD = (1536, 2560)
_ZL = (2560, 2592)
_GATES = 2592
_QPERM = np.concatenate([np.arange(0, 64), np.arange(128, 192), np.arange(64, 128), np.arange(192, 256)])
_HEAD_PERM = np.array([0, 2, 1, 3])


def _cparams(*sem):
    return pltpu.CompilerParams(dimension_semantics=sem, vmem_limit_bytes=VMEM_LIMIT)


def _dot(a, b):
    return jnp.dot(a, b, preferred_element_type=F32)


def _dot_nt(a, b):
    return lax.dot_general(a, b, (((1,), (1,)), ((), ())), preferred_element_type=F32)


def _lockstep(chains):
    results = [None] * len(chains)
    active = list(range(len(chains)))
    while active:
        for i in list(active):
            try:
                next(chains[i])
            except StopIteration as done:
                results[i] = done.value
                active.remove(i)
    return results


def _rms(x, g):
    return x * lax.rsqrt(jnp.mean(x * x, axis=-1, keepdims=True) + NORM_EPS) * g


def _split2(x):
    hi = x.astype(BF16)
    lo = (x - hi.astype(F32)).astype(BF16)
    return hi, lo


def _group_ones(width, group):
    r = lax.broadcasted_iota(jnp.int32, (width, width), 0) // group
    c = lax.broadcasted_iota(jnp.int32, (width, width), 1) // group
    return jnp.where(r == c, 1.0, 0.0).astype(BF16)


def _group_meansq(x, group):
    hi, lo = _split2(x * x)
    ones = _group_ones(x.shape[1], group)
    return (_dot(hi, ones) + _dot(lo, ones)) * (1.0 / group)


def _rope(x, cos, sin_signed, half):
    width = x.shape[1]
    lane = lax.broadcasted_iota(jnp.int32, x.shape, 1)
    from_lo = pltpu.roll(x, half, 1)
    from_hi = pltpu.roll(x, width - half, 1)
    partner = jnp.where((lane & half) != 0, from_lo, from_hi)
    return x * cos + partner * sin_signed


def _proj_kernel(x_ref, g_ref, wa_ref, wb_ref, wc_ref, wd_ref, wz_ref, gq_ref, gk_ref,
                 ca_ref, sa_ref, cc_ref, sc_ref,
                 qa_ref, kat_ref, va_ref, zb_ref, qc_ref, kc_ref, vc_ref, d_ref, zl_ref):
    h = _rms(x_ref[...], g_ref[...]).astype(BF16)
    scale = HEAD_DIM ** -0.5

    ua = _dot(h, wa_ref[...])
    ub = _dot(h, wb_ref[...])
    uc = _dot(h, wc_ref[...])
    ud = _dot(h, wd_ref[...])
    uz = _dot(h, wz_ref[...])

    ca, sa = ca_ref[...], sa_ref[...]
    q = ua[:, :256]
    q = q * lax.rsqrt(_group_meansq(q, HEAD_DIM) + NORM_EPS) * gq_ref[...]
    q = _rope(q, jnp.concatenate([ca, ca], axis=1), jnp.concatenate([sa, sa], axis=1), 16)
    qa_ref[...] = (q * (scale * LOG2E)).astype(BF16)
    k = ua[:, 256:384]
    k = k * lax.rsqrt(_group_meansq(k, HEAD_DIM) + NORM_EPS) * gk_ref[...]
    kat_ref[...] = _rope(k, ca, sa, 16).T.astype(BF16)
    va_ref[...] = ua[:, 384:].astype(BF16)

    zb_ref[...] = (ub[:, :CONV_CH] * jax.nn.sigmoid(ub[:, CONV_CH:])).astype(BF16)

    cc, sc = cc_ref[...], sc_ref[...]
    qc = _rope(uc[:, :256], jnp.concatenate([cc, cc], axis=1), jnp.concatenate([sc, sc], axis=1), 32)
    qc_ref[...] = (qc * (scale * LOG2E)).astype(BF16)
    kc_ref[...] = _rope(uc[:, 256:384], cc, sc, 32).astype(BF16)
    vc_ref[...] = uc[:, 384:].astype(BF16)

    d_ref[:, :256] = (ud[:, :256] * scale).astype(BF16)
    d_ref[:, 256:] = ud[:, 256:].astype(BF16)
    zl_ref[...] = uz.astype(BF16)


def _proj(x, lw, tabs, S):
    N = x.shape[0]
    T = min(512, S)
    per_row = S // T
    tok = lambda w: pl.BlockSpec((T, w), lambda i: (i, 0))
    full = lambda a: pl.BlockSpec(a.shape, lambda i: (0,) * a.ndim)
    tab = pl.BlockSpec((T, LANES), lambda i: (i % per_row, 0))
    widths = (256, 128, 128, 256, 256, 128, 128, 1024, 128)
    consts = (lw['g_mix'], lw['wa'], lw['wb'], lw['wc'], lw['wd'], lw['wz'], lw['gq'], lw['gk'])
    return pl.pallas_call(
        _proj_kernel,
        grid=(N // T,),
        in_specs=[tok(D_MODEL)] + [full(a) for a in consts] + [tab] * 4,
        out_specs=[tok(256), pl.BlockSpec((LANES, T), lambda i: (0, i))] + [tok(w) for w in widths[2:]],
        out_shape=[jax.ShapeDtypeStruct((N, 256), BF16), jax.ShapeDtypeStruct((LANES, N), BF16)]
                  + [jax.ShapeDtypeStruct((N, w), BF16) for w in widths[2:]],
        compiler_params=_cparams("parallel"),
        name="proj",
    )(x, *consts, *tabs)


def _stack_heads(qb):
    lane = lax.broadcasted_iota(jnp.int32, qb.shape, 1)
    zero = jnp.zeros_like(qb)
    return jnp.concatenate([jnp.where(lane < HEAD_DIM, qb, zero), jnp.where(lane < HEAD_DIM, zero, qb)], axis=0)


def _unstack_heads(o, T):
    lane = lax.broadcasted_iota(jnp.int32, (T, LANES), 1)
    return jnp.where(lane < HEAD_DIM, o[:T], o[T:])


def _gattn_kernel(q_ref, kt_ref, v_ref, o_ref, *, tk):
    T = q_ref.shape[0]
    S = kt_ref.shape[1]
    lane_v = lax.broadcasted_iota(jnp.int32, (tk, LANES), 1)
    one = jnp.ones((tk, LANES), BF16)
    qs = [_stack_heads(q_ref[:, j * LANES:(j + 1) * LANES]) for j in range(2)]
    m = [jnp.full((2 * T, 1), -jnp.inf, F32) for _ in range(2)]
    acc = [jnp.zeros((2 * T, LANES), F32) for _ in range(2)]
    for c in range(S // tk):
        kt = kt_ref[:, c * tk:(c + 1) * tk]
        v = v_ref[c * tk:(c + 1) * tk, :]
        va = jnp.where(lane_v < HEAD_DIM, v, one)
        vb = jnp.where(lane_v < HEAD_DIM, one, v)
        for j in range(2):
            s = _dot(qs[j], kt)
            m_new = jnp.maximum(m[j], jnp.max(s, axis=1, keepdims=True))
            alpha = jnp.exp2(m[j] - m_new)
            p = jnp.exp2(s - m_new).astype(BF16)
            pv = jnp.concatenate([_dot(p[:T], va), _dot(p[T:], vb)], axis=0)
            acc[j] = alpha * acc[j] + pv
            m[j] = m_new
    lane_o = lax.broadcasted_iota(jnp.int32, (T, LANES), 1)
    for j in range(2):
        a, b = acc[j][:T], acc[j][T:]
        o = jnp.where(lane_o < HEAD_DIM, a / pltpu.roll(a, HEAD_DIM, 1), b / pltpu.roll(b, HEAD_DIM, 1))
        o_ref[:, j * LANES:(j + 1) * LANES] = o.astype(BF16)


def _gattn(q, kt, v):
    B, S, _ = q.shape
    T = min(512, S)
    tk = min(1024, S)
    return pl.pallas_call(
        functools.partial(_gattn_kernel, tk=tk),
        grid=(B, S // T),
        in_specs=[pl.BlockSpec((None, T, 256), lambda b, i: (b, i, 0)),
                  pl.BlockSpec((LANES, S), lambda b, i: (0, b)),
                  pl.BlockSpec((None, S, LANES), lambda b, i: (b, 0, 0))],
        out_specs=pl.BlockSpec((None, T, 256), lambda b, i: (b, i, 0)),
        out_shape=jax.ShapeDtypeStruct((B, S, 256), BF16),
        compiler_params=_cparams("parallel", "parallel"),
        name="gattn",
    )(q, kt, v)


def _wattn_kernel(sink_ref, q_ref, k_ref, v_ref, bias_ref, o_ref, *, kw):
    T = q_ref.shape[0]
    S = k_ref.shape[0]
    i = pl.program_id(1)
    start = pl.multiple_of(jnp.clip(i * T - WINDOW, 0, S - kw), WINDOW)
    kwin = k_ref[pl.ds(start, kw), :]
    vwin = v_ref[pl.ds(start, kw), :]
    band = bias_ref[(i * T - start) // WINDOW]
    bias = jnp.concatenate([band, band], axis=0)
    lane_v = lax.broadcasted_iota(jnp.int32, (kw, LANES), 1)
    one = jnp.ones((kw, LANES), BF16)
    va = jnp.where(lane_v < HEAD_DIM, vwin, one)
    vb = jnp.where(lane_v < HEAD_DIM, one, vwin)
    first = lax.broadcasted_iota(jnp.int32, (2 * T, 1), 0) < T
    lane_o = lax.broadcasted_iota(jnp.int32, (T, LANES), 1)
    def head_block(j):
        qst = _stack_heads(q_ref[:, j * LANES:(j + 1) * LANES])
        s = _dot_nt(qst, kwin) + bias
        yield
        sk = jnp.where(first, sink_ref[2 * j], sink_ref[2 * j + 1])
        m = jnp.maximum(jnp.max(s, axis=1, keepdims=True), sk)
        p = jnp.exp2(s - m).astype(BF16)
        sunk = jnp.exp2(sk - m)
        yield
        a = _dot(p[:T], va)
        b = _dot(p[T:], vb)
        yield
        o = jnp.where(lane_o < HEAD_DIM, a / (pltpu.roll(a, HEAD_DIM, 1) + sunk[:T]),
                      b / (pltpu.roll(b, HEAD_DIM, 1) + sunk[T:]))
        o_ref[:, j * LANES:(j + 1) * LANES] = o.astype(BF16)

    _lockstep([head_block(j) for j in range(2)])


def _band_bias(T, kw):
    r = np.arange(T)[None, :, None]
    c = np.arange(kw)[None, None, :]
    off = np.arange(3)[:, None, None] * WINDOW
    return jnp.asarray(np.where(np.abs(c - off - r) <= WINDOW, 0.0, -np.inf), F32)


def _wattn(q, k, v, sink):
    B, S, _ = q.shape
    T = min(256, S - 2 * WINDOW) if S > 2 * WINDOW else S
    kw = min(T + 2 * WINDOW, S)
    bias = _band_bias(T, kw)
    return pl.pallas_call(
        functools.partial(_wattn_kernel, kw=kw),
        grid=(B, S // T),
        in_specs=[pl.BlockSpec(memory_space=pltpu.SMEM),
                  pl.BlockSpec((None, T, 256), lambda b, i: (b, i, 0)),
                  pl.BlockSpec((None, S, LANES), lambda b, i: (b, 0, 0)),
                  pl.BlockSpec((None, S, LANES), lambda b, i: (b, 0, 0)),
                  pl.BlockSpec(bias.shape, lambda b, i: (0, 0, 0))],
        out_specs=pl.BlockSpec((None, T, 256), lambda b, i: (b, i, 0)),
        out_shape=jax.ShapeDtypeStruct((B, S, 256), BF16),
        compiler_params=_cparams("parallel", "parallel"),
        name="wattn",
    )(sink, q, k, v, bias)


def _conv_kernel(zp_ref, zc_ref, zn_ref, w_ref, b_ref, g_ref, beta_ref, o_ref, buf_ref, sh_ref):
    T = zc_ref.shape[0]
    i = pl.program_id(1)
    H = CONV_HALO
    keep_prev = jnp.where(i > 0, 1.0, 0.0)
    keep_next = jnp.where(i < pl.num_programs(1) - 1, 1.0, 0.0)
    buf_ref[0:H, :] = zp_ref[...].astype(F32) * keep_prev
    buf_ref[H:H + T, :] = zc_ref[...].astype(F32)
    buf_ref[H + T:, :] = zn_ref[...].astype(F32) * keep_next
    span = sh_ref.shape[1]
    for r in range(1, SUBLANES):
        sh_ref[r] = buf_ref[r:r + span, :]
    acc = jnp.zeros((T, CONV_CH), F32)
    off = H - CONV_WIDTH // 2
    for tap in range(CONV_WIDTH):
        r = (off + tap) % SUBLANES
        a = off + tap - r
        rows = buf_ref[a:a + T, :] if r == 0 else sh_ref[r, a:a + T, :]
        acc = acc + rows * w_ref[tap:tap + 1, :]
    z = acc + b_ref[...]
    mu = jnp.mean(z, axis=-1, keepdims=True)
    zc = z - mu
    var = jnp.mean(zc * zc, axis=-1, keepdims=True)
    y = zc * lax.rsqrt(var + NORM_EPS) * g_ref[...] + beta_ref[...]
    o_ref[...] = (y * jax.nn.sigmoid(y)).astype(BF16)


def _conv(z, lw):
    B, S, _ = z.shape
    T = min(512, S)
    H = CONV_HALO
    per = T // H
    last = S // H - 1
    full = lambda a: pl.BlockSpec(a.shape, lambda b, i: (0,) * a.ndim)
    consts = (lw['conv_w'], lw['conv_b'], lw['ln_g'], lw['ln_b'])
    return pl.pallas_call(
        _conv_kernel,
        grid=(B, S // T),
        in_specs=[pl.BlockSpec((None, H, CONV_CH), lambda b, i: (b, jnp.maximum(i * per - 1, 0), 0)),
                  pl.BlockSpec((None, T, CONV_CH), lambda b, i: (b, i, 0)),
                  pl.BlockSpec((None, H, CONV_CH), lambda b, i: (b, jnp.minimum((i + 1) * per, last), 0))]
                 + [full(a) for a in consts],
        out_specs=pl.BlockSpec((None, T, CONV_CH), lambda b, i: (b, i, 0)),
        out_shape=jax.ShapeDtypeStruct((B, S, CONV_CH), BF16),
        scratch_shapes=[pltpu.VMEM((T + 2 * H, CONV_CH), F32),
                        pltpu.VMEM((SUBLANES, T + 2 * H - SUBLANES, CONV_CH), F32)],
        compiler_params=_cparams("parallel", "parallel"),
        name="conv",
    )(z, z, z, *consts)


def _gla_block(d_ref, zl_ref, w2, b2, tri, att_mask, st_ref, reverse):
    TB = d_ref.shape[0]
    L = GLA_CHUNK
    nch = TB // L

    pre = _dot(zl_ref[...], w2) + b2
    la = (jnp.minimum(pre, 0.0) - jnp.log(1.0 + jnp.exp(-jnp.abs(pre)))) * (1.0 / GLA_TAU)
    yield

    hi = la.astype(BF16)
    r1 = la - hi.astype(F32)
    mid = r1.astype(BF16)
    lo = (r1 - mid.astype(F32)).astype(BF16)
    cums = _dot(tri, jnp.concatenate([hi, mid, lo], axis=1))
    b = cums[:, :256] + cums[:, 256:512] + cums[:, 512:]
    mid_row = L // 2 if reverse else L // 2 - 1
    last_row = 0 if reverse else L - 1
    per_chunk = lambda row: jnp.concatenate(
        [jnp.broadcast_to(b[ci * L + row:ci * L + row + 1], (L, 256)) for ci in range(nch)], axis=0)
    bmid, blast = per_chunk(mid_row), per_chunk(last_row)
    yield

    q = d_ref[:, 0:256].astype(F32)
    k = d_ref[:, 256:512].astype(F32)
    v = d_ref[:, 512:768]
    qt = (q * jnp.exp(b - bmid)).astype(BF16)
    kt = (k * jnp.exp(bmid - b)).astype(BF16)
    qe = (q * jnp.exp(b)).astype(BF16)
    kl = (k * jnp.exp(blast - b)).astype(BF16)
    dec = jnp.exp(blast)
    yield

    rr = lax.broadcasted_iota(jnp.int32, (LANES, LANES), 0) // HEAD_DIM
    cc = lax.broadcasted_iota(jnp.int32, (LANES, LANES), 1) // HEAD_DIM
    head_diag = rr == cc
    rowid = lax.broadcasted_iota(jnp.int32, (TB, LANES), 0) // L
    order = range(nch - 1, -1, -1) if reverse else range(nch)

    out = []
    for p in range(2):
        ls = slice(p * LANES, (p + 1) * LANES)
        vb = v[:, ls]
        att = _dot_nt(_stack_heads(qt[:, ls]), kt[:, ls]) * att_mask
        yield
        o_intra = _unstack_heads(_dot(att.astype(BF16), vb), TB)
        vt = vb.astype(F32).T.astype(BF16)
        klb = kl[:, ls]
        yield
        kv_t = [jnp.where(head_diag, _dot(vt, jnp.where(rowid == ci, klb, jnp.zeros_like(klb))), 0.0)
                for ci in range(nch)]
        yield
        st = st_ref[p]
        o_inter = [None] * nch
        for ci in order:
            o_inter[ci] = _dot_nt(qe[ci * L:(ci + 1) * L, ls], st.astype(BF16))
            st = st * dec[ci * L:ci * L + 1, ls] + kv_t[ci]
        st_ref[p] = st
        out.append(o_intra + jnp.concatenate(o_inter, axis=0))
        yield
    return out


def _gla_kernel(df_ref, zf_ref, db_ref, zb_ref, w2_ref, b2_ref, tri_ref, mask_ref, of_ref, ob_ref, st_ref):
    @pl.when(pl.program_id(1) == 0)
    def _():
        st_ref[...] = jnp.zeros(st_ref.shape, F32)

    rows = df_ref.shape[0]
    chains = []
    for r in range(rows):
        chains.append(_gla_block(df_ref.at[r], zf_ref.at[r], w2_ref[0], b2_ref[0], tri_ref[0], mask_ref[0],
                                 st_ref.at[r, 0], False))
        chains.append(_gla_block(db_ref.at[r], zb_ref.at[r], w2_ref[1], b2_ref[1], tri_ref[1], mask_ref[1],
                                 st_ref.at[r, 1], True))
    done = _lockstep(chains)
    for r in range(rows):
        for p in range(2):
            of_ref[r, :, p * LANES:(p + 1) * LANES] = done[2 * r][p]
            ob_ref[r, :, p * LANES:(p + 1) * LANES] = done[2 * r + 1][p]


def _gla_masks(TB):
    L = GLA_CHUNK
    r = np.arange(TB)[:, None]
    c = np.arange(TB)[None, :]
    same = (r // L) == (c // L)
    tri = np.stack([same & (c <= r), same & (c >= r)]).astype(np.float32)
    return jnp.asarray(tri, BF16), jnp.asarray(np.concatenate([tri, tri], axis=1), F32)


def _gla(d, zl, w2, b2):
    B, S, _ = d.shape
    TB = min(GLA_BLOCK, S)
    nb = S // TB
    R = GLA_ROWS if B % GLA_ROWS == 0 else 1
    fblk = lambda b, i: (b, i, 0)
    bblk = lambda b, i: (b, nb - 1 - i, 0)
    full = lambda a: pl.BlockSpec(a.shape, lambda b, i: (0,) * a.ndim)
    tri, mask = _gla_masks(TB)
    out = jax.ShapeDtypeStruct((B, S, 256), F32)
    return pl.pallas_call(
        _gla_kernel,
        grid=(B // R, nb),
        in_specs=[pl.BlockSpec((R, TB, 1024), fblk), pl.BlockSpec((R, TB, LANES), fblk),
                  pl.BlockSpec((R, TB, 1024), bblk), pl.BlockSpec((R, TB, LANES), bblk),
                  full(w2), full(b2), full(tri), full(mask)],
        out_specs=[pl.BlockSpec((R, TB, 256), fblk), pl.BlockSpec((R, TB, 256), bblk)],
        out_shape=[out, out],
        scratch_shapes=[pltpu.VMEM((R, 2, 2, LANES, LANES), F32)],
        compiler_params=_cparams("parallel", "arbitrary"),
        name="gla",
    )(d, zl, d, zl, w2, b2, tri, mask)


def _merge_kernel(x_ref, oa_ref, ob_ref, oc_ref, ogf_ref, ogb_ref, rd_ref, gmix_ref, wg_ref, wbr_ref, wout_ref,
                  gn_ref, gffn_ref, wrh_ref, wrl_ref, x1_ref, h2_ref, aff_ref):
    x = x_ref[...]
    h = _rms(x, gmix_ref[...]).astype(BF16)
    og = ogf_ref[...] + ogb_ref[...]
    od = og * lax.rsqrt(_group_meansq(og, HEAD_DIM) + NORM_EPS) * gn_ref[...]
    rd = rd_ref[...].astype(F32)
    od = (od * (rd * jax.nn.sigmoid(rd))).astype(BF16)
    branches = (oa_ref[...], ob_ref[...], oc_ref[...], od)
    merged = jnp.zeros(x.shape, F32)
    for i, o in enumerate(branches):
        gate = jax.nn.sigmoid(_dot(h, wg_ref[:, i * D_MODEL:(i + 1) * D_MODEL]))
        merged = merged + gate * _dot(o, wbr_ref[i])
    x1 = x + _dot(merged.astype(BF16), wout_ref[...])
    x1_ref[...] = x1
    h2 = _rms(x1, gffn_ref[...])
    h2_ref[...] = h2
    hh, hl = _split2(h2)
    logits = _dot(hh, wrh_ref[...]) + _dot(hh, wrl_ref[...]) + _dot(hl, wrh_ref[...])
    lt = logits.T[:N_EXPERTS]
    e = jnp.exp(lt - jnp.max(lt, axis=0, keepdims=True))
    aff_ref[...] = e / jnp.sum(e, axis=0, keepdims=True)


def _merge(x, oa, ob, oc, ogf, ogb, d, lw):
    N = x.shape[0]
    T = min(256, N)
    tok = lambda w: pl.BlockSpec((T, w), lambda i: (i, 0))
    full = lambda a: pl.BlockSpec(a.shape, lambda i: (0,) * a.ndim)
    consts = (lw['g_mix'], lw['wg'], lw['wbr'], lw['wout'], lw['gn'], lw['g_ffn'], lw['wr_hi'], lw['wr_lo'])
    return pl.pallas_call(
        _merge_kernel,
        grid=(N // T,),
        in_specs=[tok(D_MODEL), tok(256), tok(256), tok(256), tok(256), tok(256),
                  pl.BlockSpec((T, 256), lambda i: (i, 3))] + [full(a) for a in consts],
        out_specs=[tok(D_MODEL), tok(D_MODEL), pl.BlockSpec((N_EXPERTS, T), lambda i: (0, i))],
        out_shape=[jax.ShapeDtypeStruct((N, D_MODEL), F32), jax.ShapeDtypeStruct((N, D_MODEL), F32),
                   jax.ShapeDtypeStruct((N_EXPERTS, N), F32)],
        compiler_params=_cparams("parallel"),
        name="merge",
    )(x, oa, ob, oc, ogf, ogb, d, *consts)


def _select_kernel(aff_ref, rank_ref, incl_ref, *, cap):
    E, N = aff_ref.shape
    CH = min(N, 4096)
    SC = min(N, SEL_CHUNK)

    def count(pred):
        def body(c, acc):
            start = pl.multiple_of(c * CH, CH)
            bits = lax.bitcast_convert_type(aff_ref[:, pl.ds(start, CH)], jnp.int32)
            tok = start + lax.broadcasted_iota(jnp.int32, (E, CH), 1)
            return acc + jnp.sum(pred(bits, tok), axis=1, keepdims=True)
        return lax.fori_loop(0, N // CH, body, jnp.zeros((E, 1), F32))

    def value_step(_, lohi):
        lo, hi = lohi
        mid = lo + ((hi - lo) >> 1)
        ok = count(lambda b, t: jnp.where(b >= mid, 1.0, 0.0)) >= cap
        return jnp.where(ok, mid, lo), jnp.where(ok, hi, mid)

    thr, _ = lax.fori_loop(0, 32, value_step,
                           (jnp.zeros((E, 1), jnp.int32), jnp.full((E, 1), 0x7F800000, jnp.int32)))
    need = cap - count(lambda b, t: jnp.where(b > thr, 1.0, 0.0))

    def tie_step(_, lohi):
        lo, hi = lohi
        mid = lo + ((hi - lo) >> 1)
        ok = count(lambda b, t: jnp.where(b == thr, jnp.where(t <= mid, 1.0, 0.0), 0.0)) >= need
        return jnp.where(ok, lo, mid), jnp.where(ok, mid, hi)

    on_thr = count(lambda b, t: jnp.where(b == thr, 1.0, 0.0))
    all_in = jnp.full((E, 1), N - 1, jnp.int32)
    cut = lax.cond(
        jnp.max(on_thr - need) <= 0.0,
        lambda: all_in,
        lambda: lax.fori_loop(0, int(np.ceil(np.log2(N))) + 1, tie_step,
                              (jnp.full((E, 1), -1, jnp.int32), all_in))[1])

    r = lax.broadcasted_iota(jnp.int32, (SC, SC), 0)
    c = lax.broadcasted_iota(jnp.int32, (SC, SC), 1)
    tri = jnp.where(r <= c, 1.0, 0.0).astype(BF16)

    def emit(ci, carry):
        start = pl.multiple_of(ci * SC, SC)
        bits = lax.bitcast_convert_type(aff_ref[:, pl.ds(start, SC)], jnp.int32)
        tok = start + lax.broadcasted_iota(jnp.int32, (E, SC), 1)
        picked = jnp.where(bits > thr, 1.0, jnp.where(bits == thr, jnp.where(tok <= cut, 1.0, 0.0), 0.0))
        inc = carry + _dot(picked.astype(BF16), tri)
        incl_ref[:, pl.ds(start, SC)] = inc
        rank_ref[:, pl.ds(start, SC)] = jnp.where(picked > 0.0, inc - 1.0, -1.0).astype(jnp.int32)
        return inc[:, SC - 1:SC]

    lax.fori_loop(0, N // SC, emit, jnp.zeros((E, 1), F32))


def _select(aff_t, cap):
    E, N = aff_t.shape
    full = pl.BlockSpec((E, N), lambda i: (0, 0))
    return pl.pallas_call(
        functools.partial(_select_kernel, cap=cap),
        grid=(1,),
        in_specs=[full],
        out_specs=[full, full],
        out_shape=[jax.ShapeDtypeStruct((E, N), jnp.int32), jax.ShapeDtypeStruct((E, N), F32)],
        compiler_params=_cparams("arbitrary"),
        name="select",
    )(aff_t)


def _sc_params():
    cp = pltpu.CompilerParams()
    if "needs_layout_passes" in pltpu.CompilerParams.__dataclass_fields__:
        cp = dataclasses.replace(cp, needs_layout_passes=False)
    return cp


def _compact(rank, aff_t, cap):
    E, N = rank.shape
    CH = min(N, 4096)
    mesh = plsc.VectorSubcoreMesh(core_axis_name="c", subcore_axis_name="s")

    @pl.kernel(out_type=(jax.ShapeDtypeStruct((E * cap,), jnp.int32), jax.ShapeDtypeStruct((E * cap,), F32)),
               mesh=mesh,
               scratch_types=[pltpu.VMEM((CH,), jnp.int32), pltpu.VMEM((CH,), F32),
                              pltpu.VMEM((cap,), jnp.int32), pltpu.VMEM((cap,), F32)],
               compiler_params=_sc_params())
    def compact(rank_hbm, aff_hbm, idx_hbm, gate_hbm, rbuf, abuf, ibuf, gbuf):
        wid = lax.axis_index("s") * mesh.num_cores + lax.axis_index("c")

        @pl.when(wid < E)
        def _():
            @pl.loop(0, N // CH)
            def _(c):
                base = wid * N + c * CH
                pltpu.sync_copy(rank_hbm.at[pl.ds(base, CH)], rbuf)
                pltpu.sync_copy(aff_hbm.at[pl.ds(base, CH)], abuf)

                @pl.loop(0, CH, step=SC_LANES)
                def _(i):
                    r = rbuf[pl.ds(i, SC_LANES)]
                    picked = r >= 0
                    slot = jnp.where(picked, r, 0)
                    tok = c * CH + i + lax.iota(jnp.int32, SC_LANES)
                    plsc.store_scatter(ibuf, [slot], tok, mask=picked)
                    plsc.store_scatter(gbuf, [slot], abuf[pl.ds(i, SC_LANES)], mask=picked)

            pltpu.sync_copy(ibuf, idx_hbm.at[pl.ds(wid * cap, cap)])
            pltpu.sync_copy(gbuf, gate_hbm.at[pl.ds(wid * cap, cap)])

    return compact(rank.reshape(E * N), aff_t.reshape(E * N))


def _gather_rows(x, idx):
    R = idx.shape[0]
    win = GATHER_WINDOW
    mesh = plsc.VectorSubcoreMesh(core_axis_name="c", subcore_axis_name="s")

    @pl.kernel(out_type=jax.ShapeDtypeStruct((R, LANES), x.dtype), mesh=mesh)
    def gather(x_hbm, i_hbm, o_hbm):
        def body(i_vmem, o_vmem):
            pltpu.sync_copy(x_hbm.at[i_vmem.at[0]], o_vmem)

        pltpu.emit_pipeline(
            body,
            grid=(R // win,),
            in_specs=[pl.BlockSpec((1, win), lambda i: (0, i))],
            out_specs=[pl.BlockSpec((win, LANES), lambda i: (i, 0))],
            core_axis_name=("c", "s"),
            dimension_semantics=(pltpu.PARALLEL,),
            trace_scopes=False,
        )(i_hbm, o_hbm)

    return gather(x, idx.reshape(1, R))


def _ffn_kernel(x_ref, wg_ref, wu_ref, wd_ref, gate_ref, y_ref, w_ref):
    @pl.when(pl.program_id(1) == 0)
    def _():
        w_ref[0] = wg_ref[...].astype(BF16)
        w_ref[1] = wu_ref[...].astype(BF16)
        w_ref[2] = wd_ref[...].astype(BF16)

    x = jnp.concatenate([x_ref[k] for k in range(D_MODEL // LANES)], axis=1).astype(BF16)
    g = _dot(x, w_ref[0])
    hid = (g * jax.nn.sigmoid(g)) * _dot(x, w_ref[1])
    y_ref[...] = (_dot(hid.astype(BF16), w_ref[2]) * gate_ref[...]).astype(BF16)


def _ffn(xe, gate, lw, cap, first):
    E = N_EXPERTS // FFN_GROUPS
    M = min(512, cap)
    per = cap // M
    layer = lw['layer']
    wspec = pl.BlockSpec((None, None, D_MODEL, D_MODEL), lambda e, s: (layer, first + e, 0, 0))
    return pl.pallas_call(
        _ffn_kernel,
        grid=(E, per),
        in_specs=[pl.BlockSpec((D_MODEL // LANES, M, LANES), lambda e, s: (0, e * per + s, 0)), wspec, wspec, wspec,
                  pl.BlockSpec((M, 1), lambda e, s: (e * per + s, 0))],
        out_specs=pl.BlockSpec((M, D_MODEL), lambda e, s: (e * per + s, 0)),
        out_shape=jax.ShapeDtypeStruct((E * cap, D_MODEL), BF16),
        scratch_shapes=[pltpu.VMEM((3, D_MODEL, D_MODEL), BF16)],
        compiler_params=_cparams("parallel", "arbitrary"),
        name="ffn",
    )(xe, lw['we_g'], lw['we_u'], lw['we_d'], gate)


def _combine_kernel(starts_ref, rounds_ref, x_ref, p_ref, rank_ref, *rest, cap, final):
    ye_hbm = rest[:FFN_GROUPS]
    g_ref, wgate_ref, wproj_ref, gfin_ref, spread_ref, o_ref, stage_ref, sem_ref = rest[FFN_GROUPS:]
    j = pl.program_id(0)
    nj = pl.num_programs(0)
    T = x_ref.shape[0]
    E, W = N_EXPERTS, COMB_WIN
    EG = E // FFN_GROUPS
    slot = j % 2

    def window_start(tile, rnd, e):
        return pl.multiple_of(jnp.minimum(starts_ref[tile * E + e] + rnd * W, EG * cap - W), 16)

    def window_copy(tile, rnd, e, sl):
        return pltpu.make_async_copy(ye_hbm[e // EG].at[pl.ds(window_start(tile, rnd, e), W), :],
                                     stage_ref.at[sl, pl.ds(e * W, W), :], sem_ref.at[sl])

    def start_all(tile, rnd, sl):
        for e in range(E):
            window_copy(tile, rnd, e, sl).start()

    def wait_all(tile, rnd, sl):
        for e in range(E):
            window_copy(tile, rnd, e, sl).wait()

    @pl.when(j == 0)
    def _():
        start_all(0, 0, 0)

    @pl.when(j + 1 < nj)
    def _():
        start_all(j + 1, 0, 1 - slot)

    rk = rank_ref[...]
    erow = lax.broadcasted_iota(jnp.int32, (E, 1), 0)
    tgt = rk + (erow % EG) * cap

    def per_expert(fn):
        v = jnp.zeros((E, 1), jnp.int32)
        for e in range(E):
            v = jnp.where(erow == e, fn(e), v)
        return v

    def placed(rnd):
        staged_from = per_expert(lambda e: window_start(j, rnd, e))
        fresh = per_expert(lambda e: starts_ref[j * E + e] + rnd * W)
        loc = jnp.where(rk >= 0, jnp.where(tgt >= fresh, tgt - staged_from, -1), -1)
        loc = jnp.where(loc < W, loc, -1).astype(F32)
        lhs = jnp.concatenate([loc, jnp.ones((1, T), F32), jnp.zeros((LANES - E - 1, T), F32)], axis=0).T
        spread = _dot(lhs.astype(BF16), spread_ref[...])
        return _dot(jnp.where(spread == 0.0, 1.0, 0.0).astype(BF16), stage_ref[slot])

    wait_all(j, 0, slot)
    moe = placed(0)

    def extra(rnd, acc):
        start_all(j, rnd, slot)
        wait_all(j, rnd, slot)
        return acc + placed(rnd)

    moe = lax.fori_loop(1, rounds_ref[j], extra, moe)

    x = x_ref[...] + moe
    h = _rms(x, g_ref[...]).astype(BF16)
    gate = jax.nn.sigmoid(_dot(h, wgate_ref[...]))
    y = x + gate * _dot(p_ref[...].astype(BF16), wproj_ref[...])
    if final:
        y = _rms(y, gfin_ref[...])
    o_ref[...] = y


def _combine(x1, p, rank, incl, yes, lw, gfin, cap, final):
    N = x1.shape[0]
    E, W = N_EXPERTS, COMB_WIN
    T = min(256, N)
    nt = N // T
    ends = incl[:, T - 1::T].astype(jnp.int32)
    begins = jnp.concatenate([jnp.zeros((E, 1), jnp.int32), ends[:, :-1]], axis=1)
    base = (jnp.arange(E, dtype=jnp.int32) % (E // FFN_GROUPS))[:, None] * cap
    aligned = (base + begins) // 16 * 16
    rounds = jnp.maximum(1, jnp.max((base + ends - aligned + W - 1) // W, axis=0)).astype(jnp.int32)
    starts = aligned.T.reshape(-1)

    tok = lambda w: pl.BlockSpec((T, w), lambda i, *_: (i, 0))
    full = lambda a: pl.BlockSpec(a.shape, lambda i, *_: (0,) * a.ndim)
    col = np.arange(E * W)[None, :]
    row = np.arange(LANES)[:, None]
    spread = np.where(row == col // W, 1.0, 0.0) + np.where(row == E, -(col % W), 0.0)
    consts = (lw['g_ple'], lw['w_pg'], lw['w_pp'], gfin, jnp.asarray(spread, BF16))
    grid_spec = pltpu.PrefetchScalarGridSpec(
        num_scalar_prefetch=2,
        grid=(nt,),
        in_specs=[tok(D_MODEL), tok(PLE_DIM), pl.BlockSpec((E, T), lambda i, *_: (0, i))]
                 + [pl.BlockSpec(memory_space=pl.ANY)] * FFN_GROUPS + [full(a) for a in consts],
        out_specs=tok(D_MODEL),
        scratch_shapes=[pltpu.VMEM((2, E * W, D_MODEL), BF16), pltpu.SemaphoreType.DMA((2,))],
    )
    return pl.pallas_call(
        functools.partial(_combine_kernel, cap=cap, final=final),
        grid_spec=grid_spec,
        out_shape=jax.ShapeDtypeStruct((N, D_MODEL), F32),
        compiler_params=_cparams("arbitrary"),
        name="combine",
    )(starts, rounds, x1, p, rank, *yes, *consts)


def _rope_tables(S):
    lane = np.arange(LANES)
    d = lane % HEAD_DIM
    t = jnp.arange(S)
    inv_a = ROPE_THETA ** (-jnp.arange(0, HEAD_DIM // 2, 2, dtype=F32) / (HEAD_DIM // 2))
    pos_a = jnp.where((d // 32 == 0)[None, :], (t // GRID_W)[:, None], (t % GRID_W)[:, None]).astype(F32)
    ang_a = pos_a * inv_a[d % 16][None, :]
    sign_a = jnp.where(d % 32 < 16, -1.0, 1.0)[None, :]
    inv_c = ROPE_THETA ** (-jnp.arange(0, HEAD_DIM, 2, dtype=F32) / HEAD_DIM)
    ang_c = t.astype(F32)[:, None] * inv_c[d % 32][None, :]
    sign_c = jnp.where(d < 32, -1.0, 1.0)[None, :]
    return (jnp.cos(ang_a), jnp.sin(ang_a) * sign_a, jnp.cos(ang_c), jnp.sin(ang_c) * sign_c)


def _layer_weights(i, w):
    w_in = w['w_in'][i]
    cols = lambda r: w_in[:, r[0]:r[1]]
    bf = lambda a: a.astype(BF16)
    row = lambda a: a.reshape(1, -1).astype(F32)
    wz = jnp.pad(cols(_ZL), ((0, 0), (0, LANES - (_ZL[1] - _ZL[0]))))
    w2 = w['gla_w2'][i]
    w2p = jnp.zeros((2, LANES, 256), F32).at[0, 0:16].set(w2[0]).at[1, 16:32].set(w2[1])
    wr = jnp.pad(w['w_router'][i], ((0, 0), (0, LANES - N_EXPERTS)))
    wr_hi = wr.astype(BF16)
    wbr = w['w_branch'][i]
    wbr = jnp.stack([wbr[0][_QPERM], wbr[1], wbr[2][_QPERM], wbr[3]])
    return dict(
        g_mix=row(w['norm_mix'][i]),
        wa=bf(jnp.concatenate([cols(_QA)[:, _QPERM], cols(_KA), cols(_VA)], axis=1)),
        wb=bf(cols(_UB)),
        wc=bf(jnp.concatenate([cols(_QC)[:, _QPERM], cols(_KC), cols(_VC)], axis=1)),
        wd=bf(cols(_DD)),
        wz=bf(wz),
        gq=row(jnp.tile(w['qk_norm'][i, 0], 4)),
        gk=row(jnp.tile(w['qk_norm'][i, 1], 2)),
        sink=w['sink_logit'][i][_HEAD_PERM].astype(F32) * LOG2E,
        conv_w=w['conv_dw'][i].astype(F32),
        conv_b=row(w['conv_dw_b'][i]), ln_g=row(w['conv_ln_g'][i]), ln_b=row(w['conv_ln_b'][i]),
        w2=bf(w2p), b2=w['gla_b2'][i].reshape(2, 1, 256).astype(F32),
        gn=row(jnp.tile(w['gla_norm'][i], 4)),
        wg=bf(w_in[:, _GATES:]), wbr=bf(wbr), wout=bf(w['w_out'][i]),
        g_ffn=row(w['norm_ffn'][i]),
        wr_hi=wr_hi, wr_lo=(wr - wr_hi.astype(F32)).astype(BF16),
        layer=i, we_g=w['w_gate_e'], we_u=w['w_up_e'], we_d=w['w_down_e'],
        g_ple=row(w['norm_ple'][i]), w_pg=bf(w['w_ple_gate'][i]), w_pp=bf(w['w_ple_proj'][i]),
    )


def _trunk(x3, p4, layers, gfin):
    B, S, _ = x3.shape
    N = B * S
    tabs = _rope_tables(S)
    x = x3.reshape(N, D_MODEL)
    cap = max(1, EC_CAPACITY * N // N_EXPERTS)
    for i, lw in enumerate(layers):
        qa, kat, va, zb, qc, kc, vc, d, zl = _proj(x, lw, tabs, S)
        b3 = lambda a: a.reshape(B, S, a.shape[-1])
        oa = _gattn(b3(qa), kat, b3(va))
        ob = _conv(b3(zb), lw)
        oc = _wattn(b3(qc), b3(kc), b3(vc), lw['sink'])
        ogf, ogb = _gla(b3(d), b3(zl), lw['w2'], lw['b2'])
        flat = lambda a: a.reshape(N, a.shape[-1])
        x1, h2, aff_t = _merge(x, flat(oa), flat(ob), flat(oc), flat(ogf), flat(ogb), d, lw)
        rank, incl = _select(aff_t, cap)
        idx, gate = _compact(rank, aff_t, cap)
        sub = D_MODEL // LANES
        h2_rows = h2.reshape(N // SUBLANES, SUBLANES, sub, LANES).transpose(0, 2, 1, 3).reshape(N * sub, LANES)
        piece = jnp.arange(sub, dtype=jnp.int32)[:, None]
        rows = ((idx // SUBLANES)[None, :] * sub + piece) * SUBLANES + (idx % SUBLANES)[None, :]
        group = N_EXPERTS // FFN_GROUPS * cap
        yes = []
        for g in range(FFN_GROUPS):
            sl = slice(g * group, (g + 1) * group)
            xe = _gather_rows(h2_rows, rows[:, sl].reshape(-1)).reshape(sub, -1, LANES)
            yes.append(_ffn(xe, gate[sl].reshape(-1, 1), lw, cap, g * (N_EXPERTS // FFN_GROUPS)))
        x = _combine(x1, p4[i].reshape(N, PLE_DIM), rank, incl, yes, lw, gfin, cap, i == len(layers) - 1)
    return x.reshape(B, S, D_MODEL)


def kernel(x_prompt, x_sample, p_prompt, p_sample, norm_mix, w_in, qk_norm, sink_logit, conv_dw, conv_dw_b,
           conv_ln_g, conv_ln_b, gla_w2, gla_b2, gla_norm, w_branch, w_out, norm_ffn, w_router, w_gate_e,
           w_up_e, w_down_e, norm_ple, w_ple_gate, w_ple_proj, norm_final):
    w = dict(norm_mix=norm_mix, w_in=w_in, qk_norm=qk_norm, sink_logit=sink_logit, conv_dw=conv_dw,
             conv_dw_b=conv_dw_b, conv_ln_g=conv_ln_g, conv_ln_b=conv_ln_b, gla_w2=gla_w2, gla_b2=gla_b2,
             gla_norm=gla_norm, w_branch=w_branch, w_out=w_out, norm_ffn=norm_ffn, w_router=w_router,
             w_gate_e=w_gate_e, w_up_e=w_up_e, w_down_e=w_down_e, norm_ple=norm_ple,
             w_ple_gate=w_ple_gate, w_ple_proj=w_ple_proj)
    layers = [_layer_weights(i, w) for i in range(norm_mix.shape[0])]
    gfin = norm_final.reshape(1, -1).astype(F32)
    return (_trunk(x_prompt, p_prompt, layers, gfin), _trunk(x_sample, p_sample, layers, gfin))
```

```python
import dataclasses
import functools

import jax
import jax.numpy as jnp
import numpy as np
from jax import lax
from jax.experimental import pallas as pl
from jax.experimental.pallas import tpu as pltpu
from jax.experimental.pallas import tpu_sc as plsc

F32 = jnp.float32
BF16 = jnp.bfloat16

D_MODEL = 1024
DEPTH = 4
GRID_W = 64
HEAD_DIM = 64
ROPE_THETA = 10000.0
NORM_EPS = 1e-6
CONV_CH = 256
CONV_WIDTH = 31
CONV_HALO = 16
WINDOW = 128
GLA_TAU = 16.0
GLA_CHUNK = 64
GLA_BLOCK = 256
GLA_ROWS = 4
N_EXPERTS = 16
EC_CAPACITY = 2
PLE_DIM = 256
LANES = 128
SUBLANES = 8
VMEM_LIMIT = 56 * 1024 * 1024
LOG2E = 1.4426950408889634
SEL_CHUNK = 512
COMB_WIN = 64
FFN_GROUPS = 2
SC_LANES = 16
GATHER_WINDOW = 128

_QA, _KA, _VA = (0, 256), (256, 384), (384, 512)
_UB = (512, 1024)
_QC, _KC, _VC = (1024, 1280), (1280, 1408), (1408, 1536)
_DD = (1536, 2560)
_ZL = (2560, 2592)
_GATES = 2592
_QPERM = np.concatenate([np.arange(0, 64), np.arange(128, 192), np.arange(64, 128), np.arange(192, 256)])
_HEAD_PERM = np.array([0, 2, 1, 3])


def _cparams(*sem):
    return pltpu.CompilerParams(dimension_semantics=sem, vmem_limit_bytes=VMEM_LIMIT)


def _dot(a, b):
    return jnp.dot(a, b, preferred_element_type=F32)


def _dot_nt(a, b):
    return lax.dot_general(a, b, (((1,), (1,)), ((), ())), preferred_element_type=F32)


def _lockstep(chains):
    results = [None] * len(chains)
    active = list(range(len(chains)))
    while active:
        for i in list(active):
            try:
                next(chains[i])
            except StopIteration as done:
                results[i] = done.value
                active.remove(i)
    return results


def _rms(x, g):
    return x * lax.rsqrt(jnp.mean(x * x, axis=-1, keepdims=True) + NORM_EPS) * g


def _split2(x):
    hi = x.astype(BF16)
    lo = (x - hi.astype(F32)).astype(BF16)
    return hi, lo


def _group_ones(width, group):
    r = lax.broadcasted_iota(jnp.int32, (width, width), 0) // group
    c = lax.broadcasted_iota(jnp.int32, (width, width), 1) // group
    return jnp.where(r == c, 1.0, 0.0).astype(BF16)


def _group_meansq(x, group):
    hi, lo = _split2(x * x)
    ones = _group_ones(x.shape[1], group)
    return (_dot(hi, ones) + _dot(lo, ones)) * (1.0 / group)


def _rope(x, cos, sin_signed, half):
    width = x.shape[1]
    lane = lax.broadcasted_iota(jnp.int32, x.shape, 1)
    from_lo = pltpu.roll(x, half, 1)
    from_hi = pltpu.roll(x, width - half, 1)
    partner = jnp.where((lane & half) != 0, from_lo, from_hi)
    return x * cos + partner * sin_signed


def _proj_kernel(x_ref, g_ref, wa_ref, wb_ref, wc_ref, wd_ref, wz_ref, gq_ref, gk_ref,
                 ca_ref, sa_ref, cc_ref, sc_ref,
                 qa_ref, kat_ref, va_ref, zb_ref, qc_ref, kc_ref, vc_ref, d_ref, zl_ref):
    h = _rms(x_ref[...], g_ref[...]).astype(BF16)
    scale = HEAD_DIM ** -0.5

    ua = _dot(h, wa_ref[...])
    ub = _dot(h, wb_ref[...])
    uc = _dot(h, wc_ref[...])
    ud = _dot(h, wd_ref[...])
    uz = _dot(h, wz_ref[...])

    ca, sa = ca_ref[...], sa_ref[...]
    q = ua[:, :256]
    q = q * lax.rsqrt(_group_meansq(q, HEAD_DIM) + NORM_EPS) * gq_ref[...]
    q = _rope(q, jnp.concatenate([ca, ca], axis=1), jnp.concatenate([sa, sa], axis=1), 16)
    qa_ref[...] = (q * (scale * LOG2E)).astype(BF16)
    k = ua[:, 256:384]
    k = k * lax.rsqrt(_group_meansq(k, HEAD_DIM) + NORM_EPS) * gk_ref[...]
    kat_ref[...] = _rope(k, ca, sa, 16).T.astype(BF16)
    va_ref[...] = ua[:, 384:].astype(BF16)

    zb_ref[...] = (ub[:, :CONV_CH] * jax.nn.sigmoid(ub[:, CONV_CH:])).astype(BF16)

    cc, sc = cc_ref[...], sc_ref[...]
    qc = _rope(uc[:, :256], jnp.concatenate([cc, cc], axis=1), jnp.concatenate([sc, sc], axis=1), 32)
    qc_ref[...] = (qc * (scale * LOG2E)).astype(BF16)
    kc_ref[...] = _rope(uc[:, 256:384], cc, sc, 32).astype(BF16)
    vc_ref[...] = uc[:, 384:].astype(BF16)

    d_ref[:, :256] = (ud[:, :256] * scale).astype(BF16)
    d_ref[:, 256:] = ud[:, 256:].astype(BF16)
    zl_ref[...] = uz.astype(BF16)


def _proj(x, lw, tabs, S):
    N = x.shape[0]
    T = min(512, S)
    per_row = S // T
    tok = lambda w: pl.BlockSpec((T, w), lambda i: (i, 0))
    full = lambda a: pl.BlockSpec(a.shape, lambda i: (0,) * a.ndim)
    tab = pl.BlockSpec((T, LANES), lambda i: (i % per_row, 0))
    widths = (256, 128, 128, 256, 256, 128, 128, 1024, 128)
    consts = (lw['g_mix'], lw['wa'], lw['wb'], lw['wc'], lw['wd'], lw['wz'], lw['gq'], lw['gk'])
    return pl.pallas_call(
        _proj_kernel,
        grid=(N // T,),
        in_specs=[tok(D_MODEL)] + [full(a) for a in consts] + [tab] * 4,
        out_specs=[tok(256), pl.BlockSpec((LANES, T), lambda i: (0, i))] + [tok(w) for w in widths[2:]],
        out_shape=[jax.ShapeDtypeStruct((N, 256), BF16), jax.ShapeDtypeStruct((LANES, N), BF16)]
                  + [jax.ShapeDtypeStruct((N, w), BF16) for w in widths[2:]],
        compiler_params=_cparams("parallel"),
        name="proj",
    )(x, *consts, *tabs)


def _stack_heads(qb):
    lane = lax.broadcasted_iota(jnp.int32, qb.shape, 1)
    zero = jnp.zeros_like(qb)
    return jnp.concatenate([jnp.where(lane < HEAD_DIM, qb, zero), jnp.where(lane < HEAD_DIM, zero, qb)], axis=0)


def _unstack_heads(o, T):
    lane = lax.broadcasted_iota(jnp.int32, (T, LANES), 1)
    return jnp.where(lane < HEAD_DIM, o[:T], o[T:])


def _gattn_kernel(q_ref, kt_ref, v_ref, o_ref, *, tk):
    T = q_ref.shape[0]
    S = kt_ref.shape[1]
    lane_v = lax.broadcasted_iota(jnp.int32, (tk, LANES), 1)
    one = jnp.ones((tk, LANES), BF16)
    qs = [_stack_heads(q_ref[:, j * LANES:(j + 1) * LANES]) for j in range(2)]
    m = [jnp.full((2 * T, 1), -jnp.inf, F32) for _ in range(2)]
    acc = [jnp.zeros((2 * T, LANES), F32) for _ in range(2)]
    for c in range(S // tk):
        kt = kt_ref[:, c * tk:(c + 1) * tk]
        v = v_ref[c * tk:(c + 1) * tk, :]
        va = jnp.where(lane_v < HEAD_DIM, v, one)
        vb = jnp.where(lane_v < HEAD_DIM, one, v)
        for j in range(2):
            s = _dot(qs[j], kt)
            m_new = jnp.maximum(m[j], jnp.max(s, axis=1, keepdims=True))
            alpha = jnp.exp2(m[j] - m_new)
            p = jnp.exp2(s - m_new).astype(BF16)
            pv = jnp.concatenate([_dot(p[:T], va), _dot(p[T:], vb)], axis=0)
            acc[j] = alpha * acc[j] + pv
            m[j] = m_new
    lane_o = lax.broadcasted_iota(jnp.int32, (T, LANES), 1)
    for j in range(2):
        a, b = acc[j][:T], acc[j][T:]
        o = jnp.where(lane_o < HEAD_DIM, a / pltpu.roll(a, HEAD_DIM, 1), b / pltpu.roll(b, HEAD_DIM, 1))
        o_ref[:, j * LANES:(j + 1) * LANES] = o.astype(BF16)


def _gattn(q, kt, v):
    B, S, _ = q.shape
    T = min(512, S)
    tk = min(2048, S)
    return pl.pallas_call(
        functools.partial(_gattn_kernel, tk=tk),
        grid=(B, S // T),
        in_specs=[pl.BlockSpec((None, T, 256), lambda b, i: (b, i, 0)),
                  pl.BlockSpec((LANES, S), lambda b, i: (0, b)),
                  pl.BlockSpec((None, S, LANES), lambda b, i: (b, 0, 0))],
        out_specs=pl.BlockSpec((None, T, 256), lambda b, i: (b, i, 0)),
        out_shape=jax.ShapeDtypeStruct((B, S, 256), BF16),
        compiler_params=_cparams("parallel", "parallel"),
        name="gattn",
    )(q, kt, v)


def _wattn_kernel(sink_ref, q_ref, k_ref, v_ref, bias_ref, o_ref, *, kw):
    T = q_ref.shape[0]
    S = k_ref.shape[0]
    i = pl.program_id(1)
    start = pl.multiple_of(jnp.clip(i * T - WINDOW, 0, S - kw), WINDOW)
    kwin = k_ref[pl.ds(start, kw), :]
    vwin = v_ref[pl.ds(start, kw), :]
    band = bias_ref[(i * T - start) // WINDOW]
    bias = jnp.concatenate([band, band], axis=0)
    lane_v = lax.broadcasted_iota(jnp.int32, (kw, LANES), 1)
    one = jnp.ones((kw, LANES), BF16)
    va = jnp.where(lane_v < HEAD_DIM, vwin, one)
    vb = jnp.where(lane_v < HEAD_DIM, one, vwin)
    first = lax.broadcasted_iota(jnp.int32, (2 * T, 1), 0) < T
    lane_o = lax.broadcasted_iota(jnp.int32, (T, LANES), 1)
    def head_block(j):
        qst = _stack_heads(q_ref[:, j * LANES:(j + 1) * LANES])
        s = _dot_nt(qst, kwin) + bias
        yield
        sk = jnp.where(first, sink_ref[2 * j], sink_ref[2 * j + 1])
        m = jnp.maximum(jnp.max(s, axis=1, keepdims=True), sk)
        p = jnp.exp2(s - m).astype(BF16)
        sunk = jnp.exp2(sk - m)
        yield
        a = _dot(p[:T], va)
        b = _dot(p[T:], vb)
        yield
        o = jnp.where(lane_o < HEAD_DIM, a / (pltpu.roll(a, HEAD_DIM, 1) + sunk[:T]),
                      b / (pltpu.roll(b, HEAD_DIM, 1) + sunk[T:]))
        o_ref[:, j * LANES:(j + 1) * LANES] = o.astype(BF16)

    _lockstep([head_block(j) for j in range(2)])


def _band_bias(T, kw):
    r = np.arange(T)[None, :, None]
    c = np.arange(kw)[None, None, :]
    off = np.arange(3)[:, None, None] * WINDOW
    return jnp.asarray(np.where(np.abs(c - off - r) <= WINDOW, 0.0, -np.inf), F32)


def _wattn(q, k, v, sink):
    B, S, _ = q.shape
    T = min(256, S - 2 * WINDOW) if S > 2 * WINDOW else S
    kw = min(T + 2 * WINDOW, S)
    bias = _band_bias(T, kw)
    return pl.pallas_call(
        functools.partial(_wattn_kernel, kw=kw),
        grid=(B, S // T),
        in_specs=[pl.BlockSpec(memory_space=pltpu.SMEM),
                  pl.BlockSpec((None, T, 256), lambda b, i: (b, i, 0)),
                  pl.BlockSpec((None, S, LANES), lambda b, i: (b, 0, 0)),
                  pl.BlockSpec((None, S, LANES), lambda b, i: (b, 0, 0)),
                  pl.BlockSpec(bias.shape, lambda b, i: (0, 0, 0))],
        out_specs=pl.BlockSpec((None, T, 256), lambda b, i: (b, i, 0)),
        out_shape=jax.ShapeDtypeStruct((B, S, 256), BF16),
        compiler_params=_cparams("parallel", "parallel"),
        name="wattn",
    )(sink, q, k, v, bias)


def _conv_kernel(zp_ref, zc_ref, zn_ref, w_ref, b_ref, g_ref, beta_ref, o_ref, buf_ref, sh_ref):
    T = zc_ref.shape[0]
    i = pl.program_id(1)
    H = CONV_HALO
    keep_prev = jnp.where(i > 0, 1.0, 0.0)
    keep_next = jnp.where(i < pl.num_programs(1) - 1, 1.0, 0.0)
    buf_ref[0:H, :] = zp_ref[...].astype(F32) * keep_prev
    buf_ref[H:H + T, :] = zc_ref[...].astype(F32)
    buf_ref[H + T:, :] = zn_ref[...].astype(F32) * keep_next
    span = sh_ref.shape[1]
    for r in range(1, SUBLANES):
        sh_ref[r] = buf_ref[r:r + span, :]
    acc = jnp.zeros((T, CONV_CH), F32)
    off = H - CONV_WIDTH // 2
    for tap in range(CONV_WIDTH):
        r = (off + tap) % SUBLANES
        a = off + tap - r
        rows = buf_ref[a:a + T, :] if r == 0 else sh_ref[r, a:a + T, :]
        acc = acc + rows * w_ref[tap:tap + 1, :]
    z = acc + b_ref[...]
    mu = jnp.mean(z, axis=-1, keepdims=True)
    zc = z - mu
    var = jnp.mean(zc * zc, axis=-1, keepdims=True)
    y = zc * lax.rsqrt(var + NORM_EPS) * g_ref[...] + beta_ref[...]
    o_ref[...] = (y * jax.nn.sigmoid(y)).astype(BF16)


def _conv(z, lw):
    B, S, _ = z.shape
    T = min(512, S)
    H = CONV_HALO
    per = T // H
    last = S // H - 1
    full = lambda a: pl.BlockSpec(a.shape, lambda b, i: (0,) * a.ndim)
    consts = (lw['conv_w'], lw['conv_b'], lw['ln_g'], lw['ln_b'])
    return pl.pallas_call(
        _conv_kernel,
        grid=(B, S // T),
        in_specs=[pl.BlockSpec((None, H, CONV_CH), lambda b, i: (b, jnp.maximum(i * per - 1, 0), 0)),
                  pl.BlockSpec((None, T, CONV_CH), lambda b, i: (b, i, 0)),
                  pl.BlockSpec((None, H, CONV_CH), lambda b, i: (b, jnp.minimum((i + 1) * per, last), 0))]
                 + [full(a) for a in consts],
        out_specs=pl.BlockSpec((None, T, CONV_CH), lambda b, i: (b, i, 0)),
        out_shape=jax.ShapeDtypeStruct((B, S, CONV_CH), BF16),
        scratch_shapes=[pltpu.VMEM((T + 2 * H, CONV_CH), F32),
                        pltpu.VMEM((SUBLANES, T + 2 * H - SUBLANES, CONV_CH), F32)],
        compiler_params=_cparams("parallel", "parallel"),
        name="conv",
    )(z, z, z, *consts)


def _gla_block(d_ref, zl_ref, w2, b2, tri, att_mask, st_ref, reverse):
    TB = d_ref.shape[0]
    L = GLA_CHUNK
    nch = TB // L

    pre = _dot(zl_ref[...], w2) + b2
    la = (jnp.minimum(pre, 0.0) - jnp.log(1.0 + jnp.exp(-jnp.abs(pre)))) * (1.0 / GLA_TAU)
    yield

    hi = la.astype(BF16)
    r1 = la - hi.astype(F32)
    mid = r1.astype(BF16)
    lo = (r1 - mid.astype(F32)).astype(BF16)
    cums = _dot(tri, jnp.concatenate([hi, mid, lo], axis=1))
    b = cums[:, :256] + cums[:, 256:512] + cums[:, 512:]
    mid_row = L // 2 if reverse else L // 2 - 1
    last_row = 0 if reverse else L - 1
    per_chunk = lambda row: jnp.concatenate(
        [jnp.broadcast_to(b[ci * L + row:ci * L + row + 1], (L, 256)) for ci in range(nch)], axis=0)
    bmid, blast = per_chunk(mid_row), per_chunk(last_row)
    yield

    q = d_ref[:, 0:256].astype(F32)
    k = d_ref[:, 256:512].astype(F32)
    v = d_ref[:, 512:768]
    qt = (q * jnp.exp(b - bmid)).astype(BF16)
    kt = (k * jnp.exp(bmid - b)).astype(BF16)
    qe = (q * jnp.exp(b)).astype(BF16)
    kl = (k * jnp.exp(blast - b)).astype(BF16)
    dec = jnp.exp(blast)
    yield

    rr = lax.broadcasted_iota(jnp.int32, (LANES, LANES), 0) // HEAD_DIM
    cc = lax.broadcasted_iota(jnp.int32, (LANES, LANES), 1) // HEAD_DIM
    head_diag = rr == cc
    rowid = lax.broadcasted_iota(jnp.int32, (TB, LANES), 0) // L
    order = range(nch - 1, -1, -1) if reverse else range(nch)

    out = []
    for p in range(2):
        ls = slice(p * LANES, (p + 1) * LANES)
        vb = v[:, ls]
        att = _dot_nt(_stack_heads(qt[:, ls]), kt[:, ls]) * att_mask
        yield
        o_intra = _unstack_heads(_dot(att.astype(BF16), vb), TB)
        vt = vb.astype(F32).T.astype(BF16)
        klb = kl[:, ls]
        yield
        kv_t = [jnp.where(head_diag, _dot(vt, jnp.where(rowid == ci, klb, jnp.zeros_like(klb))), 0.0)
                for ci in range(nch)]
        yield
        st = st_ref[p]
        o_inter = [None] * nch
        for ci in order:
            o_inter[ci] = _dot_nt(qe[ci * L:(ci + 1) * L, ls], st.astype(BF16))
            st = st * dec[ci * L:ci * L + 1, ls] + kv_t[ci]
        st_ref[p] = st
        out.append(o_intra + jnp.concatenate(o_inter, axis=0))
        yield
    return out


def _gla_kernel(df_ref, zf_ref, db_ref, zb_ref, w2_ref, b2_ref, tri_ref, mask_ref, of_ref, ob_ref, st_ref):
    @pl.when(pl.program_id(1) == 0)
    def _():
        st_ref[...] = jnp.zeros(st_ref.shape, F32)

    rows = df_ref.shape[0]
    chains = []
    for r in range(rows):
        chains.append(_gla_block(df_ref.at[r], zf_ref.at[r], w2_ref[0], b2_ref[0], tri_ref[0], mask_ref[0],
                                 st_ref.at[r, 0], False))
        chains.append(_gla_block(db_ref.at[r], zb_ref.at[r], w2_ref[1], b2_ref[1], tri_ref[1], mask_ref[1],
                                 st_ref.at[r, 1], True))
    done = _lockstep(chains)
    for r in range(rows):
        for p in range(2):
            of_ref[r, :, p * LANES:(p + 1) * LANES] = done[2 * r][p]
            ob_ref[r, :, p * LANES:(p + 1) * LANES] = done[2 * r + 1][p]


def _gla_masks(TB):
    L = GLA_CHUNK
    r = np.arange(TB)[:, None]
    c = np.arange(TB)[None, :]
    same = (r // L) == (c // L)
    tri = np.stack([same & (c <= r), same & (c >= r)]).astype(np.float32)
    return jnp.asarray(tri, BF16), jnp.asarray(np.concatenate([tri, tri], axis=1), F32)


def _gla(d, zl, w2, b2):
    B, S, _ = d.shape
    TB = min(GLA_BLOCK, S)
    nb = S // TB
    R = GLA_ROWS if B % GLA_ROWS == 0 else 1
    fblk = lambda b, i: (b, i, 0)
    bblk = lambda b, i: (b, nb - 1 - i, 0)
    full = lambda a: pl.BlockSpec(a.shape, lambda b, i: (0,) * a.ndim)
    tri, mask = _gla_masks(TB)
    out = jax.ShapeDtypeStruct((B, S, 256), F32)
    return pl.pallas_call(
        _gla_kernel,
        grid=(B // R, nb),
        in_specs=[pl.BlockSpec((R, TB, 1024), fblk), pl.BlockSpec((R, TB, LANES), fblk),
                  pl.BlockSpec((R, TB, 1024), bblk), pl.BlockSpec((R, TB, LANES), bblk),
                  full(w2), full(b2), full(tri), full(mask)],
        out_specs=[pl.BlockSpec((R, TB, 256), fblk), pl.BlockSpec((R, TB, 256), bblk)],
        out_shape=[out, out],
        scratch_shapes=[pltpu.VMEM((R, 2, 2, LANES, LANES), F32)],
        compiler_params=_cparams("parallel", "arbitrary"),
        name="gla",
    )(d, zl, d, zl, w2, b2, tri, mask)


def _merge_kernel(x_ref, oa_ref, ob_ref, oc_ref, ogf_ref, ogb_ref, rd_ref, gmix_ref, wg_ref, wbr_ref, wout_ref,
                  gn_ref, gffn_ref, wrh_ref, wrl_ref, x1_ref, h2_ref, aff_ref):
    x = x_ref[...]
    h = _rms(x, gmix_ref[...]).astype(BF16)
    og = ogf_ref[...] + ogb_ref[...]
    od = og * lax.rsqrt(_group_meansq(og, HEAD_DIM) + NORM_EPS) * gn_ref[...]
    rd = rd_ref[...].astype(F32)
    od = (od * (rd * jax.nn.sigmoid(rd))).astype(BF16)
    branches = (oa_ref[...], ob_ref[...], oc_ref[...], od)
    merged = jnp.zeros(x.shape, F32)
    for i, o in enumerate(branches):
        gate = jax.nn.sigmoid(_dot(h, wg_ref[:, i * D_MODEL:(i + 1) * D_MODEL]))
        merged = merged + gate * _dot(o, wbr_ref[i])
    x1 = x + _dot(merged.astype(BF16), wout_ref[...])
    x1_ref[...] = x1
    h2 = _rms(x1, gffn_ref[...])
    h2_ref[...] = h2
    hh, hl = _split2(h2)
    logits = _dot(hh, wrh_ref[...]) + _dot(hh, wrl_ref[...]) + _dot(hl, wrh_ref[...])
    lt = logits.T[:N_EXPERTS]
    e = jnp.exp(lt - jnp.max(lt, axis=0, keepdims=True))
    aff_ref[...] = e / jnp.sum(e, axis=0, keepdims=True)


def _merge(x, oa, ob, oc, ogf, ogb, d, lw):
    N = x.shape[0]
    T = min(256, N)
    tok = lambda w: pl.BlockSpec((T, w), lambda i: (i, 0))
    full = lambda a: pl.BlockSpec(a.shape, lambda i: (0,) * a.ndim)
    consts = (lw['g_mix'], lw['wg'], lw['wbr'], lw['wout'], lw['gn'], lw['g_ffn'], lw['wr_hi'], lw['wr_lo'])
    return pl.pallas_call(
        _merge_kernel,
        grid=(N // T,),
        in_specs=[tok(D_MODEL), tok(256), tok(256), tok(256), tok(256), tok(256),
                  pl.BlockSpec((T, 256), lambda i: (i, 3))] + [full(a) for a in consts],
        out_specs=[tok(D_MODEL), tok(D_MODEL), pl.BlockSpec((N_EXPERTS, T), lambda i: (0, i))],
        out_shape=[jax.ShapeDtypeStruct((N, D_MODEL), F32), jax.ShapeDtypeStruct((N, D_MODEL), F32),
                   jax.ShapeDtypeStruct((N_EXPERTS, N), F32)],
        compiler_params=_cparams("parallel"),
        name="merge",
    )(x, oa, ob, oc, ogf, ogb, d, *consts)


def _select_kernel(aff_ref, rank_ref, incl_ref, *, cap):
    E, N = aff_ref.shape
    CH = min(N, 4096)
    SC = min(N, SEL_CHUNK)

    def count(pred):
        def body(c, acc):
            start = pl.multiple_of(c * CH, CH)
            bits = lax.bitcast_convert_type(aff_ref[:, pl.ds(start, CH)], jnp.int32)
            tok = start + lax.broadcasted_iota(jnp.int32, (E, CH), 1)
            return acc + jnp.sum(pred(bits, tok), axis=1, keepdims=True)
        return lax.fori_loop(0, N // CH, body, jnp.zeros((E, 1), F32))

    def value_step(_, lohi):
        lo, hi = lohi
        mid = lo + ((hi - lo) >> 1)
        ok = count(lambda b, t: jnp.where(b >= mid, 1.0, 0.0)) >= cap
        return jnp.where(ok, mid, lo), jnp.where(ok, hi, mid)

    thr, _ = lax.fori_loop(0, 32, value_step,
                           (jnp.zeros((E, 1), jnp.int32), jnp.full((E, 1), 0x7F800000, jnp.int32)))
    need = cap - count(lambda b, t: jnp.where(b > thr, 1.0, 0.0))

    def tie_step(_, lohi):
        lo, hi = lohi
        mid = lo + ((hi - lo) >> 1)
        ok = count(lambda b, t: jnp.where(b == thr, jnp.where(t <= mid, 1.0, 0.0), 0.0)) >= need
        return jnp.where(ok, lo, mid), jnp.where(ok, mid, hi)

    on_thr = count(lambda b, t: jnp.where(b == thr, 1.0, 0.0))
    all_in = jnp.full((E, 1), N - 1, jnp.int32)
    cut = lax.cond(
        jnp.max(on_thr - need) <= 0.0,
        lambda: all_in,
        lambda: lax.fori_loop(0, int(np.ceil(np.log2(N))) + 1, tie_step,
                              (jnp.full((E, 1), -1, jnp.int32), all_in))[1])

    r = lax.broadcasted_iota(jnp.int32, (SC, SC), 0)
    c = lax.broadcasted_iota(jnp.int32, (SC, SC), 1)
    tri = jnp.where(r <= c, 1.0, 0.0).astype(BF16)

    def emit(ci, carry):
        start = pl.multiple_of(ci * SC, SC)
        bits = lax.bitcast_convert_type(aff_ref[:, pl.ds(start, SC)], jnp.int32)
        tok = start + lax.broadcasted_iota(jnp.int32, (E, SC), 1)
        picked = jnp.where(bits > thr, 1.0, jnp.where(bits == thr, jnp.where(tok <= cut, 1.0, 0.0), 0.0))
        inc = carry + _dot(picked.astype(BF16), tri)
        incl_ref[:, pl.ds(start, SC)] = inc
        rank_ref[:, pl.ds(start, SC)] = jnp.where(picked > 0.0, inc - 1.0, -1.0).astype(jnp.int32)
        return inc[:, SC - 1:SC]

    lax.fori_loop(0, N // SC, emit, jnp.zeros((E, 1), F32))


def _select(aff_t, cap):
    E, N = aff_t.shape
    full = pl.BlockSpec((E, N), lambda i: (0, 0))
    return pl.pallas_call(
        functools.partial(_select_kernel, cap=cap),
        grid=(1,),
        in_specs=[full],
        out_specs=[full, full],
        out_shape=[jax.ShapeDtypeStruct((E, N), jnp.int32), jax.ShapeDtypeStruct((E, N), F32)],
        compiler_params=_cparams("arbitrary"),
        name="select",
    )(aff_t)


def _sc_params():
    cp = pltpu.CompilerParams()
    if "needs_layout_passes" in pltpu.CompilerParams.__dataclass_fields__:
        cp = dataclasses.replace(cp, needs_layout_passes=False)
    return cp


def _compact(rank, aff_t, cap):
    E, N = rank.shape
    CH = min(N, 4096)
    mesh = plsc.VectorSubcoreMesh(core_axis_name="c", subcore_axis_name="s")

    @pl.kernel(out_type=(jax.ShapeDtypeStruct((E * cap,), jnp.int32), jax.ShapeDtypeStruct((E * cap,), F32)),
               mesh=mesh,
               scratch_types=[pltpu.VMEM((CH,), jnp.int32), pltpu.VMEM((CH,), F32),
                              pltpu.VMEM((cap,), jnp.int32), pltpu.VMEM((cap,), F32)],
               compiler_params=_sc_params())
    def compact(rank_hbm, aff_hbm, idx_hbm, gate_hbm, rbuf, abuf, ibuf, gbuf):
        wid = lax.axis_index("s") * mesh.num_cores + lax.axis_index("c")

        @pl.when(wid < E)
        def _():
            @pl.loop(0, N // CH)
            def _(c):
                base = wid * N + c * CH
                pltpu.sync_copy(rank_hbm.at[pl.ds(base, CH)], rbuf)
                pltpu.sync_copy(aff_hbm.at[pl.ds(base, CH)], abuf)

                @pl.loop(0, CH, step=SC_LANES)
                def _(i):
                    r = rbuf[pl.ds(i, SC_LANES)]
                    picked = r >= 0
                    slot = jnp.where(picked, r, 0)
                    tok = c * CH + i + lax.iota(jnp.int32, SC_LANES)
                    plsc.store_scatter(ibuf, [slot], tok, mask=picked)
                    plsc.store_scatter(gbuf, [slot], abuf[pl.ds(i, SC_LANES)], mask=picked)

            pltpu.sync_copy(ibuf, idx_hbm.at[pl.ds(wid * cap, cap)])
            pltpu.sync_copy(gbuf, gate_hbm.at[pl.ds(wid * cap, cap)])

    return compact(rank.reshape(E * N), aff_t.reshape(E * N))


def _gather_rows(x, idx):
    R = idx.shape[0]
    win = GATHER_WINDOW
    mesh = plsc.VectorSubcoreMesh(core_axis_name="c", subcore_axis_name="s")

    @pl.kernel(out_type=jax.ShapeDtypeStruct((R, LANES), x.dtype), mesh=mesh)
    def gather(x_hbm, i_hbm, o_hbm):
        def body(i_vmem, o_vmem):
            pltpu.sync_copy(x_hbm.at[i_vmem.at[0]], o_vmem)

        pltpu.emit_pipeline(
            body,
            grid=(R // win,),
            in_specs=[pl.BlockSpec((1, win), lambda i: (0, i))],
            out_specs=[pl.BlockSpec((win, LANES), lambda i: (i, 0))],
            core_axis_name=("c", "s"),
            dimension_semantics=(pltpu.PARALLEL,),
            trace_scopes=False,
        )(i_hbm, o_hbm)

    return gather(x, idx.reshape(1, R))


def _ffn_kernel(x_ref, wg_ref, wu_ref, wd_ref, gate_ref, y_ref, w_ref):
    @pl.when(pl.program_id(1) == 0)
    def _():
        w_ref[0] = wg_ref[...].astype(BF16)
        w_ref[1] = wu_ref[...].astype(BF16)
        w_ref[2] = wd_ref[...].astype(BF16)

    x = jnp.concatenate([x_ref[k] for k in range(D_MODEL // LANES)], axis=1).astype(BF16)
    g = _dot(x, w_ref[0])
    hid = (g * jax.nn.sigmoid(g)) * _dot(x, w_ref[1])
    y_ref[...] = (_dot(hid.astype(BF16), w_ref[2]) * gate_ref[...]).astype(BF16)


def _ffn(xe, gate, lw, cap, first):
    E = N_EXPERTS // FFN_GROUPS
    M = min(512, cap)
    per = cap // M
    layer = lw['layer']
    wspec = pl.BlockSpec((None, None, D_MODEL, D_MODEL), lambda e, s: (layer, first + e, 0, 0))
    return pl.pallas_call(
        _ffn_kernel,
        grid=(E, per),
        in_specs=[pl.BlockSpec((D_MODEL // LANES, M, LANES), lambda e, s: (0, e * per + s, 0)), wspec, wspec, wspec,
                  pl.BlockSpec((M, 1), lambda e, s: (e * per + s, 0))],
        out_specs=pl.BlockSpec((M, D_MODEL), lambda e, s: (e * per + s, 0)),
        out_shape=jax.ShapeDtypeStruct((E * cap, D_MODEL), BF16),
        scratch_shapes=[pltpu.VMEM((3, D_MODEL, D_MODEL), BF16)],
        compiler_params=_cparams("parallel", "arbitrary"),
        name="ffn",
    )(xe, lw['we_g'], lw['we_u'], lw['we_d'], gate)


def _combine_kernel(starts_ref, rounds_ref, x_ref, p_ref, rank_ref, *rest, cap, final):
    ye_hbm = rest[:FFN_GROUPS]
    g_ref, wgate_ref, wproj_ref, gfin_ref, spread_ref, o_ref, stage_ref, sem_ref = rest[FFN_GROUPS:]
    j = pl.program_id(0)
    nj = pl.num_programs(0)
    T = x_ref.shape[0]
    E, W = N_EXPERTS, COMB_WIN
    EG = E // FFN_GROUPS
    slot = j % 2

    def window_start(tile, rnd, e):
        return pl.multiple_of(jnp.minimum(starts_ref[tile * E + e] + rnd * W, EG * cap - W), 16)

    def window_copy(tile, rnd, e, sl):
        return pltpu.make_async_copy(ye_hbm[e // EG].at[pl.ds(window_start(tile, rnd, e), W), :],
                                     stage_ref.at[sl, pl.ds(e * W, W), :], sem_ref.at[sl])

    def start_all(tile, rnd, sl):
        for e in range(E):
            window_copy(tile, rnd, e, sl).start()

    def wait_all(tile, rnd, sl):
        for e in range(E):
            window_copy(tile, rnd, e, sl).wait()

    @pl.when(j == 0)
    def _():
        start_all(0, 0, 0)

    @pl.when(j + 1 < nj)
    def _():
        start_all(j + 1, 0, 1 - slot)

    rk = rank_ref[...]
    erow = lax.broadcasted_iota(jnp.int32, (E, 1), 0)
    tgt = rk + (erow % EG) * cap

    def per_expert(fn):
        v = jnp.zeros((E, 1), jnp.int32)
        for e in range(E):
            v = jnp.where(erow == e, fn(e), v)
        return v

    def placed(rnd):
        staged_from = per_expert(lambda e: window_start(j, rnd, e))
        fresh = per_expert(lambda e: starts_ref[j * E + e] + rnd * W)
        loc = jnp.where(rk >= 0, jnp.where(tgt >= fresh, tgt - staged_from, -1), -1)
        loc = jnp.where(loc < W, loc, -1).astype(F32)
        lhs = jnp.concatenate([loc, jnp.ones((1, T), F32), jnp.zeros((LANES - E - 1, T), F32)], axis=0).T
        spread = _dot(lhs.astype(BF16), spread_ref[...])
        return _dot(jnp.where(spread == 0.0, 1.0, 0.0).astype(BF16), stage_ref[slot])

    wait_all(j, 0, slot)
    moe = placed(0)

    def extra(rnd, acc):
        start_all(j, rnd, slot)
        wait_all(j, rnd, slot)
        return acc + placed(rnd)

    moe = lax.fori_loop(1, rounds_ref[j], extra, moe)

    x = x_ref[...] + moe
    h = _rms(x, g_ref[...]).astype(BF16)
    gate = jax.nn.sigmoid(_dot(h, wgate_ref[...]))
    y = x + gate * _dot(p_ref[...].astype(BF16), wproj_ref[...])
    if final:
        y = _rms(y, gfin_ref[...])
    o_ref[...] = y


def _combine(x1, p, rank, incl, yes, lw, gfin, cap, final):
    N = x1.shape[0]
    E, W = N_EXPERTS, COMB_WIN
    T = min(256, N)
    nt = N // T
    ends = incl[:, T - 1::T].astype(jnp.int32)
    begins = jnp.concatenate([jnp.zeros((E, 1), jnp.int32), ends[:, :-1]], axis=1)
    base = (jnp.arange(E, dtype=jnp.int32) % (E // FFN_GROUPS))[:, None] * cap
    aligned = (base + begins) // 16 * 16
    rounds = jnp.maximum(1, jnp.max((base + ends - aligned + W - 1) // W, axis=0)).astype(jnp.int32)
    starts = aligned.T.reshape(-1)

    tok = lambda w: pl.BlockSpec((T, w), lambda i, *_: (i, 0))
    full = lambda a: pl.BlockSpec(a.shape, lambda i, *_: (0,) * a.ndim)
    col = np.arange(E * W)[None, :]
    row = np.arange(LANES)[:, None]
    spread = np.where(row == col // W, 1.0, 0.0) + np.where(row == E, -(col % W), 0.0)
    consts = (lw['g_ple'], lw['w_pg'], lw['w_pp'], gfin, jnp.asarray(spread, BF16))
    grid_spec = pltpu.PrefetchScalarGridSpec(
        num_scalar_prefetch=2,
        grid=(nt,),
        in_specs=[tok(D_MODEL), tok(PLE_DIM), pl.BlockSpec((E, T), lambda i, *_: (0, i))]
                 + [pl.BlockSpec(memory_space=pl.ANY)] * FFN_GROUPS + [full(a) for a in consts],
        out_specs=tok(D_MODEL),
        scratch_shapes=[pltpu.VMEM((2, E * W, D_MODEL), BF16), pltpu.SemaphoreType.DMA((2,))],
    )
    return pl.pallas_call(
        functools.partial(_combine_kernel, cap=cap, final=final),
        grid_spec=grid_spec,
        out_shape=jax.ShapeDtypeStruct((N, D_MODEL), F32),
        compiler_params=_cparams("arbitrary"),
        name="combine",
    )(starts, rounds, x1, p, rank, *yes, *consts)


def _rope_tables(S):
    lane = np.arange(LANES)
    d = lane % HEAD_DIM
    t = jnp.arange(S)
    inv_a = ROPE_THETA ** (-jnp.arange(0, HEAD_DIM // 2, 2, dtype=F32) / (HEAD_DIM // 2))
    pos_a = jnp.where((d // 32 == 0)[None, :], (t // GRID_W)[:, None], (t % GRID_W)[:, None]).astype(F32)
    ang_a = pos_a * inv_a[d % 16][None, :]
    sign_a = jnp.where(d % 32 < 16, -1.0, 1.0)[None, :]
    inv_c = ROPE_THETA ** (-jnp.arange(0, HEAD_DIM, 2, dtype=F32) / HEAD_DIM)
    ang_c = t.astype(F32)[:, None] * inv_c[d % 32][None, :]
    sign_c = jnp.where(d < 32, -1.0, 1.0)[None, :]
    return (jnp.cos(ang_a), jnp.sin(ang_a) * sign_a, jnp.cos(ang_c), jnp.sin(ang_c) * sign_c)


def _layer_weights(i, w):
    w_in = w['w_in'][i]
    cols = lambda r: w_in[:, r[0]:r[1]]
    bf = lambda a: a.astype(BF16)
    row = lambda a: a.reshape(1, -1).astype(F32)
    wz = jnp.pad(cols(_ZL), ((0, 0), (0, LANES - (_ZL[1] - _ZL[0]))))
    w2 = w['gla_w2'][i]
    w2p = jnp.zeros((2, LANES, 256), F32).at[0, 0:16].set(w2[0]).at[1, 16:32].set(w2[1])
    wr = jnp.pad(w['w_router'][i], ((0, 0), (0, LANES - N_EXPERTS)))
    wr_hi = wr.astype(BF16)
    wbr = w['w_branch'][i]
    wbr = jnp.stack([wbr[0][_QPERM], wbr[1], wbr[2][_QPERM], wbr[3]])
    return dict(
        g_mix=row(w['norm_mix'][i]),
        wa=bf(jnp.concatenate([cols(_QA)[:, _QPERM], cols(_KA), cols(_VA)], axis=1)),
        wb=bf(cols(_UB)),
        wc=bf(jnp.concatenate([cols(_QC)[:, _QPERM], cols(_KC), cols(_VC)], axis=1)),
        wd=bf(cols(_DD)),
        wz=bf(wz),
        gq=row(jnp.tile(w['qk_norm'][i, 0], 4)),
        gk=row(jnp.tile(w['qk_norm'][i, 1], 2)),
        sink=w['sink_logit'][i][_HEAD_PERM].astype(F32) * LOG2E,
        conv_w=w['conv_dw'][i].astype(F32),
        conv_b=row(w['conv_dw_b'][i]), ln_g=row(w['conv_ln_g'][i]), ln_b=row(w['conv_ln_b'][i]),
        w2=bf(w2p), b2=w['gla_b2'][i].reshape(2, 1, 256).astype(F32),
        gn=row(jnp.tile(w['gla_norm'][i], 4)),
        wg=bf(w_in[:, _GATES:]), wbr=bf(wbr), wout=bf(w['w_out'][i]),
        g_ffn=row(w['norm_ffn'][i]),
        wr_hi=wr_hi, wr_lo=(wr - wr_hi.astype(F32)).astype(BF16),
        layer=i, we_g=w['w_gate_e'], we_u=w['w_up_e'], we_d=w['w_down_e'],
        g_ple=row(w['norm_ple'][i]), w_pg=bf(w['w_ple_gate'][i]), w_pp=bf(w['w_ple_proj'][i]),
    )


def _trunk(x3, p4, layers, gfin):
    B, S, _ = x3.shape
    N = B * S
    tabs = _rope_tables(S)
    x = x3.reshape(N, D_MODEL)
    cap = max(1, EC_CAPACITY * N // N_EXPERTS)
    for i, lw in enumerate(layers):
        qa, kat, va, zb, qc, kc, vc, d, zl = _proj(x, lw, tabs, S)
        b3 = lambda a: a.reshape(B, S, a.shape[-1])
        oa = _gattn(b3(qa), kat, b3(va))
        ob = _conv(b3(zb), lw)
        oc = _wattn(b3(qc), b3(kc), b3(vc), lw['sink'])
        ogf, ogb = _gla(b3(d), b3(zl), lw['w2'], lw['b2'])
        flat = lambda a: a.reshape(N, a.shape[-1])
        x1, h2, aff_t = _merge(x, flat(oa), flat(ob), flat(oc), flat(ogf), flat(ogb), d, lw)
        rank, incl = _select(aff_t, cap)
        idx, gate = _compact(rank, aff_t, cap)
        sub = D_MODEL // LANES
        h2_rows = h2.reshape(N // SUBLANES, SUBLANES, sub, LANES).transpose(0, 2, 1, 3).reshape(N * sub, LANES)
        piece = jnp.arange(sub, dtype=jnp.int32)[:, None]
        rows = ((idx // SUBLANES)[None, :] * sub + piece) * SUBLANES + (idx % SUBLANES)[None, :]
        group = N_EXPERTS // FFN_GROUPS * cap
        yes = []
        for g in range(FFN_GROUPS):
            sl = slice(g * group, (g + 1) * group)
            xe = _gather_rows(h2_rows, rows[:, sl].reshape(-1)).reshape(sub, -1, LANES)
            yes.append(_ffn(xe, gate[sl].reshape(-1, 1), lw, cap, g * (N_EXPERTS // FFN_GROUPS)))
        x = _combine(x1, p4[i].reshape(N, PLE_DIM), rank, incl, yes, lw, gfin, cap, i == len(layers) - 1)
    return x.reshape(B, S, D_MODEL)


def kernel(x_prompt, x_sample, p_prompt, p_sample, norm_mix, w_in, qk_norm, sink_logit, conv_dw, conv_dw_b,
           conv_ln_g, conv_ln_b, gla_w2, gla_b2, gla_norm, w_branch, w_out, norm_ffn, w_router, w_gate_e,
           w_up_e, w_down_e, norm_ple, w_ple_gate, w_ple_proj, norm_final):
    w = dict(norm_mix=norm_mix, w_in=w_in, qk_norm=qk_norm, sink_logit=sink_logit, conv_dw=conv_dw,
             conv_dw_b=conv_dw_b, conv_ln_g=conv_ln_g, conv_ln_b=conv_ln_b, gla_w2=gla_w2, gla_b2=gla_b2,
             gla_norm=gla_norm, w_branch=w_branch, w_out=w_out, norm_ffn=norm_ffn, w_router=w_router,
             w_gate_e=w_gate_e, w_up_e=w_up_e, w_down_e=w_down_e, norm_ple=norm_ple,
             w_ple_gate=w_ple_gate, w_ple_proj=w_ple_proj)
    layers = [_layer_weights(i, w) for i in range(norm_mix.shape[0])]
    gfin = norm_final.reshape(1, -1).astype(F32)
    return (_trunk(x_prompt, p_prompt, layers, gfin), _trunk(x_sample, p_sample, layers, gfin))
```

```python
import dataclasses
import functools

import jax
import jax.numpy as jnp
import numpy as np
from jax import lax
from jax.experimental import pallas as pl
from jax.experimental.pallas import tpu as pltpu
from jax.experimental.pallas import tpu_sc as plsc

F32 = jnp.float32
BF16 = jnp.bfloat16

D_MODEL = 1024
DEPTH = 4
GRID_W = 64
HEAD_DIM = 64
ROPE_THETA = 10000.0
NORM_EPS = 1e-6
CONV_CH = 256
CONV_WIDTH = 31
CONV_HALO = 16
WINDOW = 128
GLA_TAU = 16.0
GLA_CHUNK = 64
GLA_BLOCK = 256
GLA_ROWS = 4
N_EXPERTS = 16
EC_CAPACITY = 2
PLE_DIM = 256
LANES = 128
SUBLANES = 8
VMEM_LIMIT = 56 * 1024 * 1024
LOG2E = 1.4426950408889634
SEL_CHUNK = 512
COMB_WIN = 64
FFN_GROUPS = 2
MERGE_STRIPS = 2
SC_LANES = 16
GATHER_WINDOW = 128

_QA, _KA, _VA = (0, 256), (256, 384), (384, 512)
_UB = (512, 1024)
_QC, _KC, _VC = (1024, 1280), (1280, 1408), (1408, 1536)
_DD = (1536, 2560)
_ZL = (2560, 2592)
_GATES = 2592
_QPERM = np.concatenate([np.arange(0, 64), np.arange(128, 192), np.arange(64, 128), np.arange(192, 256)])
_HEAD_PERM = np.array([0, 2, 1, 3])


def _cparams(*sem):
    return pltpu.CompilerParams(dimension_semantics=sem, vmem_limit_bytes=VMEM_LIMIT)


def _dot(a, b):
    return jnp.dot(a, b, preferred_element_type=F32)


def _dot_nt(a, b):
    return lax.dot_general(a, b, (((1,), (1,)), ((), ())), preferred_element_type=F32)


def _lockstep(chains):
    results = [None] * len(chains)
    active = list(range(len(chains)))
    while active:
        for i in list(active):
            try:
                next(chains[i])
            except StopIteration as done:
                results[i] = done.value
                active.remove(i)
    return results


def _rms(x, g):
    return x * lax.rsqrt(jnp.mean(x * x, axis=-1, keepdims=True) + NORM_EPS) * g


def _split2(x):
    hi = x.astype(BF16)
    lo = (x - hi.astype(F32)).astype(BF16)
    return hi, lo


def _group_ones(width, group):
    r = lax.broadcasted_iota(jnp.int32, (width, width), 0) // group
    c = lax.broadcasted_iota(jnp.int32, (width, width), 1) // group
    return jnp.where(r == c, 1.0, 0.0).astype(BF16)


def _group_meansq(x, group):
    hi, lo = _split2(x * x)
    ones = _group_ones(x.shape[1], group)
    return (_dot(hi, ones) + _dot(lo, ones)) * (1.0 / group)


def _rope(x, cos, sin_signed, half):
    width = x.shape[1]
    lane = lax.broadcasted_iota(jnp.int32, x.shape, 1)
    from_lo = pltpu.roll(x, half, 1)
    from_hi = pltpu.roll(x, width - half, 1)
    partner = jnp.where((lane & half) != 0, from_lo, from_hi)
    return x * cos + partner * sin_signed


def _proj_kernel(x_ref, g_ref, wa_ref, wb_ref, wc_ref, wd_ref, wz_ref, gq_ref, gk_ref,
                 ca_ref, sa_ref, cc_ref, sc_ref,
                 qa_ref, kat_ref, va_ref, zb_ref, qc_ref, kc_ref, vc_ref, d_ref, zl_ref):
    h = _rms(x_ref[...], g_ref[...]).astype(BF16)
    scale = HEAD_DIM ** -0.5

    ua = _dot(h, wa_ref[...])
    ub = _dot(h, wb_ref[...])
    uc = _dot(h, wc_ref[...])
    ud = _dot(h, wd_ref[...])
    uz = _dot(h, wz_ref[...])

    ca, sa = ca_ref[...], sa_ref[...]
    q = ua[:, :256]
    q = q * lax.rsqrt(_group_meansq(q, HEAD_DIM) + NORM_EPS) * gq_ref[...]
    q = _rope(q, jnp.concatenate([ca, ca], axis=1), jnp.concatenate([sa, sa], axis=1), 16)
    qa_ref[...] = (q * (scale * LOG2E)).astype(BF16)
    k = ua[:, 256:384]
    k = k * lax.rsqrt(_group_meansq(k, HEAD_DIM) + NORM_EPS) * gk_ref[...]
    kat_ref[...] = _rope(k, ca, sa, 16).T.astype(BF16)
    va_ref[...] = ua[:, 384:].astype(BF16)

    zb_ref[...] = (ub[:, :CONV_CH] * jax.nn.sigmoid(ub[:, CONV_CH:])).astype(BF16)

    cc, sc = cc_ref[...], sc_ref[...]
    qc = _rope(uc[:, :256], jnp.concatenate([cc, cc], axis=1), jnp.concatenate([sc, sc], axis=1), 32)
    qc_ref[...] = (qc * (scale * LOG2E)).astype(BF16)
    kc_ref[...] = _rope(uc[:, 256:384], cc, sc, 32).astype(BF16)
    vc_ref[...] = uc[:, 384:].astype(BF16)

    d_ref[:, :256] = (ud[:, :256] * scale).astype(BF16)
    d_ref[:, 256:] = ud[:, 256:].astype(BF16)
    zl_ref[...] = uz.astype(BF16)


def _proj(x, lw, tabs, S):
    N = x.shape[0]
    T = min(512, S)
    per_row = S // T
    tok = lambda w: pl.BlockSpec((T, w), lambda i: (i, 0))
    full = lambda a: pl.BlockSpec(a.shape, lambda i: (0,) * a.ndim)
    tab = pl.BlockSpec((T, LANES), lambda i: (i % per_row, 0))
    widths = (256, 128, 128, 256, 256, 128, 128, 1024, 128)
    consts = (lw['g_mix'], lw['wa'], lw['wb'], lw['wc'], lw['wd'], lw['wz'], lw['gq'], lw['gk'])
    return pl.pallas_call(
        _proj_kernel,
        grid=(N // T,),
        in_specs=[tok(D_MODEL)] + [full(a) for a in consts] + [tab] * 4,
        out_specs=[tok(256), pl.BlockSpec((LANES, T), lambda i: (0, i))] + [tok(w) for w in widths[2:]],
        out_shape=[jax.ShapeDtypeStruct((N, 256), BF16), jax.ShapeDtypeStruct((LANES, N), BF16)]
                  + [jax.ShapeDtypeStruct((N, w), BF16) for w in widths[2:]],
        compiler_params=_cparams("parallel"),
        name="proj",
    )(x, *consts, *tabs)


def _stack_heads(qb):
    lane = lax.broadcasted_iota(jnp.int32, qb.shape, 1)
    zero = jnp.zeros_like(qb)
    return jnp.concatenate([jnp.where(lane < HEAD_DIM, qb, zero), jnp.where(lane < HEAD_DIM, zero, qb)], axis=0)


def _unstack_heads(o, T):
    lane = lax.broadcasted_iota(jnp.int32, (T, LANES), 1)
    return jnp.where(lane < HEAD_DIM, o[:T], o[T:])


def _gattn_kernel(q_ref, kt_ref, v_ref, o_ref, *, tk):
    T = q_ref.shape[0]
    S = kt_ref.shape[1]
    lane_v = lax.broadcasted_iota(jnp.int32, (tk, LANES), 1)
    one = jnp.ones((tk, LANES), BF16)
    qs = [_stack_heads(q_ref[:, j * LANES:(j + 1) * LANES]) for j in range(2)]
    m = [jnp.full((2 * T, 1), -jnp.inf, F32) for _ in range(2)]
    acc = [jnp.zeros((2 * T, LANES), F32) for _ in range(2)]
    for c in range(S // tk):
        kt = kt_ref[:, c * tk:(c + 1) * tk]
        v = v_ref[c * tk:(c + 1) * tk, :]
        va = jnp.where(lane_v < HEAD_DIM, v, one)
        vb = jnp.where(lane_v < HEAD_DIM, one, v)
        for j in range(2):
            s = _dot(qs[j], kt)
            m_new = jnp.maximum(m[j], jnp.max(s, axis=1, keepdims=True))
            alpha = jnp.exp2(m[j] - m_new)
            p = jnp.exp2(s - m_new).astype(BF16)
            pv = jnp.concatenate([_dot(p[:T], va), _dot(p[T:], vb)], axis=0)
            acc[j] = alpha * acc[j] + pv
            m[j] = m_new
    lane_o = lax.broadcasted_iota(jnp.int32, (T, LANES), 1)
    for j in range(2):
        a, b = acc[j][:T], acc[j][T:]
        o = jnp.where(lane_o < HEAD_DIM, a / pltpu.roll(a, HEAD_DIM, 1), b / pltpu.roll(b, HEAD_DIM, 1))
        o_ref[:, j * LANES:(j + 1) * LANES] = o.astype(BF16)


def _gattn(q, kt, v):
    B, S, _ = q.shape
    T = min(512, S)
    tk = min(2048, S)
    return pl.pallas_call(
        functools.partial(_gattn_kernel, tk=tk),
        grid=(B, S // T),
        in_specs=[pl.BlockSpec((None, T, 256), lambda b, i: (b, i, 0)),
                  pl.BlockSpec((LANES, S), lambda b, i: (0, b)),
                  pl.BlockSpec((None, S, LANES), lambda b, i: (b, 0, 0))],
        out_specs=pl.BlockSpec((None, T, 256), lambda b, i: (b, i, 0)),
        out_shape=jax.ShapeDtypeStruct((B, S, 256), BF16),
        compiler_params=_cparams("parallel", "parallel"),
        name="gattn",
    )(q, kt, v)


def _wattn_kernel(sink_ref, q_ref, k_ref, v_ref, bias_ref, o_ref, *, kw):
    T = q_ref.shape[0]
    S = k_ref.shape[0]
    i = pl.program_id(1)
    start = pl.multiple_of(jnp.clip(i * T - WINDOW, 0, S - kw), WINDOW)
    kwin = k_ref[pl.ds(start, kw), :]
    vwin = v_ref[pl.ds(start, kw), :]
    band = bias_ref[(i * T - start) // WINDOW]
    bias = jnp.concatenate([band, band], axis=0)
    lane_v = lax.broadcasted_iota(jnp.int32, (kw, LANES), 1)
    one = jnp.ones((kw, LANES), BF16)
    va = jnp.where(lane_v < HEAD_DIM, vwin, one)
    vb = jnp.where(lane_v < HEAD_DIM, one, vwin)
    first = lax.broadcasted_iota(jnp.int32, (2 * T, 1), 0) < T
    lane_o = lax.broadcasted_iota(jnp.int32, (T, LANES), 1)
    def head_block(j):
        qst = _stack_heads(q_ref[:, j * LANES:(j + 1) * LANES])
        s = _dot_nt(qst, kwin) + bias
        yield
        sk = jnp.where(first, sink_ref[2 * j], sink_ref[2 * j + 1])
        m = jnp.maximum(jnp.max(s, axis=1, keepdims=True), sk)
        p = jnp.exp2(s - m).astype(BF16)
        sunk = jnp.exp2(sk - m)
        yield
        a = _dot(p[:T], va)
        b = _dot(p[T:], vb)
        yield
        o = jnp.where(lane_o < HEAD_DIM, a / (pltpu.roll(a, HEAD_DIM, 1) + sunk[:T]),
                      b / (pltpu.roll(b, HEAD_DIM, 1) + sunk[T:]))
        o_ref[:, j * LANES:(j + 1) * LANES] = o.astype(BF16)

    _lockstep([head_block(j) for j in range(2)])


def _band_bias(T, kw):
    r = np.arange(T)[None, :, None]
    c = np.arange(kw)[None, None, :]
    off = np.arange(3)[:, None, None] * WINDOW
    return jnp.asarray(np.where(np.abs(c - off - r) <= WINDOW, 0.0, -np.inf), F32)


def _wattn(q, k, v, sink):
    B, S, _ = q.shape
    T = min(256, S - 2 * WINDOW) if S > 2 * WINDOW else S
    kw = min(T + 2 * WINDOW, S)
    bias = _band_bias(T, kw)
    return pl.pallas_call(
        functools.partial(_wattn_kernel, kw=kw),
        grid=(B, S // T),
        in_specs=[pl.BlockSpec(memory_space=pltpu.SMEM),
                  pl.BlockSpec((None, T, 256), lambda b, i: (b, i, 0)),
                  pl.BlockSpec((None, S, LANES), lambda b, i: (b, 0, 0)),
                  pl.BlockSpec((None, S, LANES), lambda b, i: (b, 0, 0)),
                  pl.BlockSpec(bias.shape, lambda b, i: (0, 0, 0))],
        out_specs=pl.BlockSpec((None, T, 256), lambda b, i: (b, i, 0)),
        out_shape=jax.ShapeDtypeStruct((B, S, 256), BF16),
        compiler_params=_cparams("parallel", "parallel"),
        name="wattn",
    )(sink, q, k, v, bias)


def _conv_kernel(zp_ref, zc_ref, zn_ref, w_ref, b_ref, g_ref, beta_ref, o_ref, buf_ref, sh_ref):
    T = zc_ref.shape[0]
    i = pl.program_id(1)
    H = CONV_HALO
    keep_prev = jnp.where(i > 0, 1.0, 0.0)
    keep_next = jnp.where(i < pl.num_programs(1) - 1, 1.0, 0.0)
    buf_ref[0:H, :] = zp_ref[...].astype(F32) * keep_prev
    buf_ref[H:H + T, :] = zc_ref[...].astype(F32)
    buf_ref[H + T:, :] = zn_ref[...].astype(F32) * keep_next
    span = sh_ref.shape[1]
    for r in range(1, SUBLANES):
        sh_ref[r] = buf_ref[r:r + span, :]
    acc = jnp.zeros((T, CONV_CH), F32)
    off = H - CONV_WIDTH // 2
    for tap in range(CONV_WIDTH):
        r = (off + tap) % SUBLANES
        a = off + tap - r
        rows = buf_ref[a:a + T, :] if r == 0 else sh_ref[r, a:a + T, :]
        acc = acc + rows * w_ref[tap:tap + 1, :]
    z = acc + b_ref[...]
    mu = jnp.mean(z, axis=-1, keepdims=True)
    zc = z - mu
    var = jnp.mean(zc * zc, axis=-1, keepdims=True)
    y = zc * lax.rsqrt(var + NORM_EPS) * g_ref[...] + beta_ref[...]
    o_ref[...] = (y * jax.nn.sigmoid(y)).astype(BF16)


def _conv(z, lw):
    B, S, _ = z.shape
    T = min(512, S)
    H = CONV_HALO
    per = T // H
    last = S // H - 1
    full = lambda a: pl.BlockSpec(a.shape, lambda b, i: (0,) * a.ndim)
    consts = (lw['conv_w'], lw['conv_b'], lw['ln_g'], lw['ln_b'])
    return pl.pallas_call(
        _conv_kernel,
        grid=(B, S // T),
        in_specs=[pl.BlockSpec((None, H, CONV_CH), lambda b, i: (b, jnp.maximum(i * per - 1, 0), 0)),
                  pl.BlockSpec((None, T, CONV_CH), lambda b, i: (b, i, 0)),
                  pl.BlockSpec((None, H, CONV_CH), lambda b, i: (b, jnp.minimum((i + 1) * per, last), 0))]
                 + [full(a) for a in consts],
        out_specs=pl.BlockSpec((None, T, CONV_CH), lambda b, i: (b, i, 0)),
        out_shape=jax.ShapeDtypeStruct((B, S, CONV_CH), BF16),
        scratch_shapes=[pltpu.VMEM((T + 2 * H, CONV_CH), F32),
                        pltpu.VMEM((SUBLANES, T + 2 * H - SUBLANES, CONV_CH), F32)],
        compiler_params=_cparams("parallel", "parallel"),
        name="conv",
    )(z, z, z, *consts)


def _gla_block(d_ref, zl_ref, w2, b2, tri, att_mask, st_ref, reverse):
    TB = d_ref.shape[0]
    L = GLA_CHUNK
    nch = TB // L

    pre = _dot(zl_ref[...], w2) + b2
    la = (jnp.minimum(pre, 0.0) - jnp.log(1.0 + jnp.exp(-jnp.abs(pre)))) * (1.0 / GLA_TAU)
    yield

    hi = la.astype(BF16)
    r1 = la - hi.astype(F32)
    mid = r1.astype(BF16)
    lo = (r1 - mid.astype(F32)).astype(BF16)
    cums = _dot(tri, jnp.concatenate([hi, mid, lo], axis=1))
    b = cums[:, :256] + cums[:, 256:512] + cums[:, 512:]
    mid_row = L // 2 if reverse else L // 2 - 1
    last_row = 0 if reverse else L - 1
    per_chunk = lambda row: jnp.concatenate(
        [jnp.broadcast_to(b[ci * L + row:ci * L + row + 1], (L, 256)) for ci in range(nch)], axis=0)
    bmid, blast = per_chunk(mid_row), per_chunk(last_row)
    yield

    q = d_ref[:, 0:256].astype(F32)
    k = d_ref[:, 256:512].astype(F32)
    v = d_ref[:, 512:768]
    qt = (q * jnp.exp(b - bmid)).astype(BF16)
    kt = (k * jnp.exp(bmid - b)).astype(BF16)
    qe = (q * jnp.exp(b)).astype(BF16)
    kl = (k * jnp.exp(blast - b)).astype(BF16)
    dec = jnp.exp(blast)
    yield

    rr = lax.broadcasted_iota(jnp.int32, (LANES, LANES), 0) // HEAD_DIM
    cc = lax.broadcasted_iota(jnp.int32, (LANES, LANES), 1) // HEAD_DIM
    head_diag = rr == cc
    rowid = lax.broadcasted_iota(jnp.int32, (TB, LANES), 0) // L
    order = range(nch - 1, -1, -1) if reverse else range(nch)

    out = []
    for p in range(2):
        ls = slice(p * LANES, (p + 1) * LANES)
        vb = v[:, ls]
        att = _dot_nt(_stack_heads(qt[:, ls]), kt[:, ls]) * att_mask
        yield
        o_intra = _unstack_heads(_dot(att.astype(BF16), vb), TB)
        vt = vb.astype(F32).T.astype(BF16)
        klb = kl[:, ls]
        yield
        kv_t = [jnp.where(head_diag, _dot(vt, jnp.where(rowid == ci, klb, jnp.zeros_like(klb))), 0.0)
                for ci in range(nch)]
        yield
        st = st_ref[p]
        o_inter = [None] * nch
        for ci in order:
            o_inter[ci] = _dot_nt(qe[ci * L:(ci + 1) * L, ls], st.astype(BF16))
            st = st * dec[ci * L:ci * L + 1, ls] + kv_t[ci]
        st_ref[p] = st
        out.append(o_intra + jnp.concatenate(o_inter, axis=0))
        yield
    return out


def _gla_kernel(df_ref, zf_ref, db_ref, zb_ref, w2_ref, b2_ref, tri_ref, mask_ref, of_ref, ob_ref, st_ref):
    @pl.when(pl.program_id(1) == 0)
    def _():
        st_ref[...] = jnp.zeros(st_ref.shape, F32)

    rows = df_ref.shape[0]
    chains = []
    for r in range(rows):
        chains.append(_gla_block(df_ref.at[r], zf_ref.at[r], w2_ref[0], b2_ref[0], tri_ref[0], mask_ref[0],
                                 st_ref.at[r, 0], False))
        chains.append(_gla_block(db_ref.at[r], zb_ref.at[r], w2_ref[1], b2_ref[1], tri_ref[1], mask_ref[1],
                                 st_ref.at[r, 1], True))
    done = _lockstep(chains)
    for r in range(rows):
        for p in range(2):
            of_ref[r, :, p * LANES:(p + 1) * LANES] = done[2 * r][p]
            ob_ref[r, :, p * LANES:(p + 1) * LANES] = done[2 * r + 1][p]


def _gla_masks(TB):
    L = GLA_CHUNK
    r = np.arange(TB)[:, None]
    c = np.arange(TB)[None, :]
    same = (r // L) == (c // L)
    tri = np.stack([same & (c <= r), same & (c >= r)]).astype(np.float32)
    return jnp.asarray(tri, BF16), jnp.asarray(np.concatenate([tri, tri], axis=1), F32)


def _gla(d, zl, w2, b2):
    B, S, _ = d.shape
    TB = min(GLA_BLOCK, S)
    nb = S // TB
    R = GLA_ROWS if B % GLA_ROWS == 0 else 1
    fblk = lambda b, i: (b, i, 0)
    bblk = lambda b, i: (b, nb - 1 - i, 0)
    full = lambda a: pl.BlockSpec(a.shape, lambda b, i: (0,) * a.ndim)
    tri, mask = _gla_masks(TB)
    out = jax.ShapeDtypeStruct((B, S, 256), F32)
    return pl.pallas_call(
        _gla_kernel,
        grid=(B // R, nb),
        in_specs=[pl.BlockSpec((R, TB, 1024), fblk), pl.BlockSpec((R, TB, LANES), fblk),
                  pl.BlockSpec((R, TB, 1024), bblk), pl.BlockSpec((R, TB, LANES), bblk),
                  full(w2), full(b2), full(tri), full(mask)],
        out_specs=[pl.BlockSpec((R, TB, 256), fblk), pl.BlockSpec((R, TB, 256), bblk)],
        out_shape=[out, out],
        scratch_shapes=[pltpu.VMEM((R, 2, 2, LANES, LANES), F32)],
        compiler_params=_cparams("parallel", "arbitrary"),
        name="gla",
    )(d, zl, d, zl, w2, b2, tri, mask)


def _merge_kernel(x_ref, oa_ref, ob_ref, oc_ref, ogf_ref, ogb_ref, rd_ref, gmix_ref, wg_ref, wbr_ref, wout_ref,
                  gn_ref, gffn_ref, wrh_ref, wrl_ref, x1_ref, h2_ref, aff_ref):
    x = x_ref[...]
    h = _rms(x, gmix_ref[...]).astype(BF16)
    og = ogf_ref[...] + ogb_ref[...]
    od = og * lax.rsqrt(_group_meansq(og, HEAD_DIM) + NORM_EPS) * gn_ref[...]
    rd = rd_ref[...].astype(F32)
    od = (od * (rd * jax.nn.sigmoid(rd))).astype(BF16)
    branches = (oa_ref[...], ob_ref[...], oc_ref[...], od)
    half = D_MODEL // MERGE_STRIPS
    y = jnp.zeros(x.shape, F32)
    for n in range(MERGE_STRIPS):
        merged = jnp.zeros((x.shape[0], half), F32)
        for i, o in enumerate(branches):
            c0 = i * D_MODEL + n * half
            gate = jax.nn.sigmoid(_dot(h, wg_ref[:, c0:c0 + half]))
            merged = merged + gate * _dot(o, wbr_ref[i, :, n * half:(n + 1) * half])
        y = y + _dot(merged.astype(BF16), wout_ref[n * half:(n + 1) * half, :])
    x1 = x + y
    x1_ref[...] = x1
    h2 = _rms(x1, gffn_ref[...])
    h2_ref[...] = h2
    hh, hl = _split2(h2)
    logits = _dot(hh, wrh_ref[...]) + _dot(hh, wrl_ref[...]) + _dot(hl, wrh_ref[...])
    lt = logits.T[:N_EXPERTS]
    e = jnp.exp(lt - jnp.max(lt, axis=0, keepdims=True))
    aff_ref[...] = e / jnp.sum(e, axis=0, keepdims=True)


def _merge(x, oa, ob, oc, ogf, ogb, d, lw):
    N = x.shape[0]
    T = min(256, N)
    tok = lambda w: pl.BlockSpec((T, w), lambda i: (i, 0))
    full = lambda a: pl.BlockSpec(a.shape, lambda i: (0,) * a.ndim)
    consts = (lw['g_mix'], lw['wg'], lw['wbr'], lw['wout'], lw['gn'], lw['g_ffn'], lw['wr_hi'], lw['wr_lo'])
    return pl.pallas_call(
        _merge_kernel,
        grid=(N // T,),
        in_specs=[tok(D_MODEL), tok(256), tok(256), tok(256), tok(256), tok(256),
                  pl.BlockSpec((T, 256), lambda i: (i, 3))] + [full(a) for a in consts],
        out_specs=[tok(D_MODEL), tok(D_MODEL), pl.BlockSpec((N_EXPERTS, T), lambda i: (0, i))],
        out_shape=[jax.ShapeDtypeStruct((N, D_MODEL), F32), jax.ShapeDtypeStruct((N, D_MODEL), F32),
                   jax.ShapeDtypeStruct((N_EXPERTS, N), F32)],
        compiler_params=_cparams("parallel"),
        name="merge",
    )(x, oa, ob, oc, ogf, ogb, d, *consts)


def _select_kernel(aff_ref, rank_ref, incl_ref, *, cap):
    E, N = aff_ref.shape
    CH = min(N, 4096)
    SC = min(N, SEL_CHUNK)

    def count(pred):
        def body(c, acc):
            start = pl.multiple_of(c * CH, CH)
            bits = lax.bitcast_convert_type(aff_ref[:, pl.ds(start, CH)], jnp.int32)
            tok = start + lax.broadcasted_iota(jnp.int32, (E, CH), 1)
            return acc + jnp.sum(pred(bits, tok), axis=1, keepdims=True)
        return lax.fori_loop(0, N // CH, body, jnp.zeros((E, 1), F32))

    def value_step(_, lohi):
        lo, hi = lohi
        mid = lo + ((hi - lo) >> 1)
        ok = count(lambda b, t: jnp.where(b >= mid, 1.0, 0.0)) >= cap
        return jnp.where(ok, mid, lo), jnp.where(ok, hi, mid)

    thr, _ = lax.fori_loop(0, 32, value_step,
                           (jnp.zeros((E, 1), jnp.int32), jnp.full((E, 1), 0x7F800000, jnp.int32)))
    need = cap - count(lambda b, t: jnp.where(b > thr, 1.0, 0.0))

    def tie_step(_, lohi):
        lo, hi = lohi
        mid = lo + ((hi - lo) >> 1)
        ok = count(lambda b, t: jnp.where(b == thr, jnp.where(t <= mid, 1.0, 0.0), 0.0)) >= need
        return jnp.where(ok, lo, mid), jnp.where(ok, mid, hi)

    on_thr = count(lambda b, t: jnp.where(b == thr, 1.0, 0.0))
    all_in = jnp.full((E, 1), N - 1, jnp.int32)
    cut = lax.cond(
        jnp.max(on_thr - need) <= 0.0,
        lambda: all_in,
        lambda: lax.fori_loop(0, int(np.ceil(np.log2(N))) + 1, tie_step,
                              (jnp.full((E, 1), -1, jnp.int32), all_in))[1])

    r = lax.broadcasted_iota(jnp.int32, (SC, SC), 0)
    c = lax.broadcasted_iota(jnp.int32, (SC, SC), 1)
    tri = jnp.where(r <= c, 1.0, 0.0).astype(BF16)

    def emit(ci, carry):
        start = pl.multiple_of(ci * SC, SC)
        bits = lax.bitcast_convert_type(aff_ref[:, pl.ds(start, SC)], jnp.int32)
        tok = start + lax.broadcasted_iota(jnp.int32, (E, SC), 1)
        picked = jnp.where(bits > thr, 1.0, jnp.where(bits == thr, jnp.where(tok <= cut, 1.0, 0.0), 0.0))
        inc = carry + _dot(picked.astype(BF16), tri)
        incl_ref[:, pl.ds(start, SC)] = inc
        rank_ref[:, pl.ds(start, SC)] = jnp.where(picked > 0.0, inc - 1.0, -1.0).astype(jnp.int32)
        return inc[:, SC - 1:SC]

    lax.fori_loop(0, N // SC, emit, jnp.zeros((E, 1), F32))


def _select(aff_t, cap):
    E, N = aff_t.shape
    full = pl.BlockSpec((E, N), lambda i: (0, 0))
    return pl.pallas_call(
        functools.partial(_select_kernel, cap=cap),
        grid=(1,),
        in_specs=[full],
        out_specs=[full, full],
        out_shape=[jax.ShapeDtypeStruct((E, N), jnp.int32), jax.ShapeDtypeStruct((E, N), F32)],
        compiler_params=_cparams("arbitrary"),
        name="select",
    )(aff_t)


def _sc_params():
    cp = pltpu.CompilerParams()
    if "needs_layout_passes" in pltpu.CompilerParams.__dataclass_fields__:
        cp = dataclasses.replace(cp, needs_layout_passes=False)
    return cp


def _compact(rank, aff_t, cap):
    E, N = rank.shape
    CH = min(N, 4096)
    mesh = plsc.VectorSubcoreMesh(core_axis_name="c", subcore_axis_name="s")

    @pl.kernel(out_type=(jax.ShapeDtypeStruct((E * cap,), jnp.int32), jax.ShapeDtypeStruct((E * cap,), F32)),
               mesh=mesh,
               scratch_types=[pltpu.VMEM((CH,), jnp.int32), pltpu.VMEM((CH,), F32),
                              pltpu.VMEM((cap,), jnp.int32), pltpu.VMEM((cap,), F32)],
               compiler_params=_sc_params())
    def compact(rank_hbm, aff_hbm, idx_hbm, gate_hbm, rbuf, abuf, ibuf, gbuf):
        wid = lax.axis_index("s") * mesh.num_cores + lax.axis_index("c")

        @pl.when(wid < E)
        def _():
            @pl.loop(0, N // CH)
            def _(c):
                base = wid * N + c * CH
                pltpu.sync_copy(rank_hbm.at[pl.ds(base, CH)], rbuf)
                pltpu.sync_copy(aff_hbm.at[pl.ds(base, CH)], abuf)

                @pl.loop(0, CH, step=SC_LANES)
                def _(i):
                    r = rbuf[pl.ds(i, SC_LANES)]
                    picked = r >= 0
                    slot = jnp.where(picked, r, 0)
                    tok = c * CH + i + lax.iota(jnp.int32, SC_LANES)
                    plsc.store_scatter(ibuf, [slot], tok, mask=picked)
                    plsc.store_scatter(gbuf, [slot], abuf[pl.ds(i, SC_LANES)], mask=picked)

            pltpu.sync_copy(ibuf, idx_hbm.at[pl.ds(wid * cap, cap)])
            pltpu.sync_copy(gbuf, gate_hbm.at[pl.ds(wid * cap, cap)])

    return compact(rank.reshape(E * N), aff_t.reshape(E * N))


def _gather_rows(x, idx):
    R = idx.shape[0]
    win = GATHER_WINDOW
    mesh = plsc.VectorSubcoreMesh(core_axis_name="c", subcore_axis_name="s")

    @pl.kernel(out_type=jax.ShapeDtypeStruct((R, LANES), x.dtype), mesh=mesh)
    def gather(x_hbm, i_hbm, o_hbm):
        def body(i_vmem, o_vmem):
            pltpu.sync_copy(x_hbm.at[i_vmem.at[0]], o_vmem)

        pltpu.emit_pipeline(
            body,
            grid=(R // win,),
            in_specs=[pl.BlockSpec((1, win), lambda i: (0, i))],
            out_specs=[pl.BlockSpec((win, LANES), lambda i: (i, 0))],
            core_axis_name=("c", "s"),
            dimension_semantics=(pltpu.PARALLEL,),
            trace_scopes=False,
        )(i_hbm, o_hbm)

    return gather(x, idx.reshape(1, R))


def _ffn_kernel(x_ref, wg_ref, wu_ref, wd_ref, gate_ref, y_ref, w_ref):
    @pl.when(pl.program_id(1) == 0)
    def _():
        w_ref[0] = wg_ref[...].astype(BF16)
        w_ref[1] = wu_ref[...].astype(BF16)
        w_ref[2] = wd_ref[...].astype(BF16)

    x = jnp.concatenate([x_ref[k] for k in range(D_MODEL // LANES)], axis=1).astype(BF16)
    g = _dot(x, w_ref[0])
    hid = (g * jax.nn.sigmoid(g)) * _dot(x, w_ref[1])
    y_ref[...] = (_dot(hid.astype(BF16), w_ref[2]) * gate_ref[...]).astype(BF16)


def _ffn(xe, gate, lw, cap, first):
    E = N_EXPERTS // FFN_GROUPS
    M = min(512, cap)
    per = cap // M
    layer = lw['layer']
    wspec = pl.BlockSpec((None, None, D_MODEL, D_MODEL), lambda e, s: (layer, first + e, 0, 0))
    return pl.pallas_call(
        _ffn_kernel,
        grid=(E, per),
        in_specs=[pl.BlockSpec((D_MODEL // LANES, M, LANES), lambda e, s: (0, e * per + s, 0)), wspec, wspec, wspec,
                  pl.BlockSpec((M, 1), lambda e, s: (e * per + s, 0))],
        out_specs=pl.BlockSpec((M, D_MODEL), lambda e, s: (e * per + s, 0)),
        out_shape=jax.ShapeDtypeStruct((E * cap, D_MODEL), BF16),
        scratch_shapes=[pltpu.VMEM((3, D_MODEL, D_MODEL), BF16)],
        compiler_params=_cparams("parallel", "arbitrary"),
        name="ffn",
    )(xe, lw['we_g'], lw['we_u'], lw['we_d'], gate)


def _combine_kernel(starts_ref, rounds_ref, x_ref, p_ref, rank_ref, *rest, cap, final):
    ye_hbm = rest[:FFN_GROUPS]
    g_ref, wgate_ref, wproj_ref, gfin_ref, spread_ref, o_ref, stage_ref, sem_ref = rest[FFN_GROUPS:]
    j = pl.program_id(0)
    nj = pl.num_programs(0)
    T = x_ref.shape[0]
    E, W = N_EXPERTS, COMB_WIN
    EG = E // FFN_GROUPS
    slot = j % 2

    def window_start(tile, rnd, e):
        return pl.multiple_of(jnp.minimum(starts_ref[tile * E + e] + rnd * W, EG * cap - W), 16)

    def window_copy(tile, rnd, e, sl):
        return pltpu.make_async_copy(ye_hbm[e // EG].at[pl.ds(window_start(tile, rnd, e), W), :],
                                     stage_ref.at[sl, pl.ds(e * W, W), :], sem_ref.at[sl])

    def start_all(tile, rnd, sl):
        for e in range(E):
            window_copy(tile, rnd, e, sl).start()

    def wait_all(tile, rnd, sl):
        for e in range(E):
            window_copy(tile, rnd, e, sl).wait()

    @pl.when(j == 0)
    def _():
        start_all(0, 0, 0)

    @pl.when(j + 1 < nj)
    def _():
        start_all(j + 1, 0, 1 - slot)

    rk = rank_ref[...]
    erow = lax.broadcasted_iota(jnp.int32, (E, 1), 0)
    tgt = rk + (erow % EG) * cap

    def per_expert(fn):
        v = jnp.zeros((E, 1), jnp.int32)
        for e in range(E):
            v = jnp.where(erow == e, fn(e), v)
        return v

    def placed(rnd):
        staged_from = per_expert(lambda e: window_start(j, rnd, e))
        fresh = per_expert(lambda e: starts_ref[j * E + e] + rnd * W)
        loc = jnp.where(rk >= 0, jnp.where(tgt >= fresh, tgt - staged_from, -1), -1)
        loc = jnp.where(loc < W, loc, -1).astype(F32)
        lhs = jnp.concatenate([loc, jnp.ones((1, T), F32), jnp.zeros((LANES - E - 1, T), F32)], axis=0).T
        spread = _dot(lhs.astype(BF16), spread_ref[...])
        return _dot(jnp.where(spread == 0.0, 1.0, 0.0).astype(BF16), stage_ref[slot])

    wait_all(j, 0, slot)
    moe = placed(0)

    def extra(rnd, acc):
        start_all(j, rnd, slot)
        wait_all(j, rnd, slot)
        return acc + placed(rnd)

    moe = lax.fori_loop(1, rounds_ref[j], extra, moe)

    x = x_ref[...] + moe
    h = _rms(x, g_ref[...]).astype(BF16)
    gate = jax.nn.sigmoid(_dot(h, wgate_ref[...]))
    y = x + gate * _dot(p_ref[...].astype(BF16), wproj_ref[...])
    if final:
        y = _rms(y, gfin_ref[...])
    o_ref[...] = y


def _combine(x1, p, rank, incl, yes, lw, gfin, cap, final):
    N = x1.shape[0]
    E, W = N_EXPERTS, COMB_WIN
    T = min(256, N)
    nt = N // T
    ends = incl[:, T - 1::T].astype(jnp.int32)
    begins = jnp.concatenate([jnp.zeros((E, 1), jnp.int32), ends[:, :-1]], axis=1)
    base = (jnp.arange(E, dtype=jnp.int32) % (E // FFN_GROUPS))[:, None] * cap
    aligned = (base + begins) // 16 * 16
    rounds = jnp.maximum(1, jnp.max((base + ends - aligned + W - 1) // W, axis=0)).astype(jnp.int32)
    starts = aligned.T.reshape(-1)

    tok = lambda w: pl.BlockSpec((T, w), lambda i, *_: (i, 0))
    full = lambda a: pl.BlockSpec(a.shape, lambda i, *_: (0,) * a.ndim)
    col = np.arange(E * W)[None, :]
    row = np.arange(LANES)[:, None]
    spread = np.where(row == col // W, 1.0, 0.0) + np.where(row == E, -(col % W), 0.0)
    consts = (lw['g_ple'], lw['w_pg'], lw['w_pp'], gfin, jnp.asarray(spread, BF16))
    grid_spec = pltpu.PrefetchScalarGridSpec(
        num_scalar_prefetch=2,
        grid=(nt,),
        in_specs=[tok(D_MODEL), tok(PLE_DIM), pl.BlockSpec((E, T), lambda i, *_: (0, i))]
                 + [pl.BlockSpec(memory_space=pl.ANY)] * FFN_GROUPS + [full(a) for a in consts],
        out_specs=tok(D_MODEL),
        scratch_shapes=[pltpu.VMEM((2, E * W, D_MODEL), BF16), pltpu.SemaphoreType.DMA((2,))],
    )
    return pl.pallas_call(
        functools.partial(_combine_kernel, cap=cap, final=final),
        grid_spec=grid_spec,
        out_shape=jax.ShapeDtypeStruct((N, D_MODEL), F32),
        compiler_params=_cparams("arbitrary"),
        name="combine",
    )(starts, rounds, x1, p, rank, *yes, *consts)


def _rope_tables(S):
    lane = np.arange(LANES)
    d = lane % HEAD_DIM
    t = jnp.arange(S)
    inv_a = ROPE_THETA ** (-jnp.arange(0, HEAD_DIM // 2, 2, dtype=F32) / (HEAD_DIM // 2))
    pos_a = jnp.where((d // 32 == 0)[None, :], (t // GRID_W)[:, None], (t % GRID_W)[:, None]).astype(F32)
    ang_a = pos_a * inv_a[d % 16][None, :]
    sign_a = jnp.where(d % 32 < 16, -1.0, 1.0)[None, :]
    inv_c = ROPE_THETA ** (-jnp.arange(0, HEAD_DIM, 2, dtype=F32) / HEAD_DIM)
    ang_c = t.astype(F32)[:, None] * inv_c[d % 32][None, :]
    sign_c = jnp.where(d < 32, -1.0, 1.0)[None, :]
    return (jnp.cos(ang_a), jnp.sin(ang_a) * sign_a, jnp.cos(ang_c), jnp.sin(ang_c) * sign_c)


def _layer_weights(i, w):
    w_in = w['w_in'][i]
    cols = lambda r: w_in[:, r[0]:r[1]]
    bf = lambda a: a.astype(BF16)
    row = lambda a: a.reshape(1, -1).astype(F32)
    wz = jnp.pad(cols(_ZL), ((0, 0), (0, LANES - (_ZL[1] - _ZL[0]))))
    w2 = w['gla_w2'][i]
    w2p = jnp.zeros((2, LANES, 256), F32).at[0, 0:16].set(w2[0]).at[1, 16:32].set(w2[1])
    wr = jnp.pad(w['w_router'][i], ((0, 0), (0, LANES - N_EXPERTS)))
    wr_hi = wr.astype(BF16)
    wbr = w['w_branch'][i]
    wbr = jnp.stack([wbr[0][_QPERM], wbr[1], wbr[2][_QPERM], wbr[3]])
    return dict(
        g_mix=row(w['norm_mix'][i]),
        wa=bf(jnp.concatenate([cols(_QA)[:, _QPERM], cols(_KA), cols(_VA)], axis=1)),
        wb=bf(cols(_UB)),
        wc=bf(jnp.concatenate([cols(_QC)[:, _QPERM], cols(_KC), cols(_VC)], axis=1)),
        wd=bf(cols(_DD)),
        wz=bf(wz),
        gq=row(jnp.tile(w['qk_norm'][i, 0], 4)),
        gk=row(jnp.tile(w['qk_norm'][i, 1], 2)),
        sink=w['sink_logit'][i][_HEAD_PERM].astype(F32) * LOG2E,
        conv_w=w['conv_dw'][i].astype(F32),
        conv_b=row(w['conv_dw_b'][i]), ln_g=row(w['conv_ln_g'][i]), ln_b=row(w['conv_ln_b'][i]),
        w2=bf(w2p), b2=w['gla_b2'][i].reshape(2, 1, 256).astype(F32),
        gn=row(jnp.tile(w['gla_norm'][i], 4)),
        wg=bf(w_in[:, _GATES:]), wbr=bf(wbr), wout=bf(w['w_out'][i]),
        g_ffn=row(w['norm_ffn'][i]),
        wr_hi=wr_hi, wr_lo=(wr - wr_hi.astype(F32)).astype(BF16),
        layer=i, we_g=w['w_gate_e'], we_u=w['w_up_e'], we_d=w['w_down_e'],
        g_ple=row(w['norm_ple'][i]), w_pg=bf(w['w_ple_gate'][i]), w_pp=bf(w['w_ple_proj'][i]),
    )


def _trunk(x3, p4, layers, gfin):
    B, S, _ = x3.shape
    N = B * S
    tabs = _rope_tables(S)
    x = x3.reshape(N, D_MODEL)
    cap = max(1, EC_CAPACITY * N // N_EXPERTS)
    for i, lw in enumerate(layers):
        qa, kat, va, zb, qc, kc, vc, d, zl = _proj(x, lw, tabs, S)
        b3 = lambda a: a.reshape(B, S, a.shape[-1])
        oa = _gattn(b3(qa), kat, b3(va))
        ob = _conv(b3(zb), lw)
        oc = _wattn(b3(qc), b3(kc), b3(vc), lw['sink'])
        ogf, ogb = _gla(b3(d), b3(zl), lw['w2'], lw['b2'])
        flat = lambda a: a.reshape(N, a.shape[-1])
        x1, h2, aff_t = _merge(x, flat(oa), flat(ob), flat(oc), flat(ogf), flat(ogb), d, lw)
        rank, incl = _select(aff_t, cap)
        idx, gate = _compact(rank, aff_t, cap)
        sub = D_MODEL // LANES
        h2_rows = h2.reshape(N // SUBLANES, SUBLANES, sub, LANES).transpose(0, 2, 1, 3).reshape(N * sub, LANES)
        piece = jnp.arange(sub, dtype=jnp.int32)[:, None]
        rows = ((idx // SUBLANES)[None, :] * sub + piece) * SUBLANES + (idx % SUBLANES)[None, :]
        group = N_EXPERTS // FFN_GROUPS * cap
        yes = []
        for g in range(FFN_GROUPS):
            sl = slice(g * group, (g + 1) * group)
            xe = _gather_rows(h2_rows, rows[:, sl].reshape(-1)).reshape(sub, -1, LANES)
            yes.append(_ffn(xe, gate[sl].reshape(-1, 1), lw, cap, g * (N_EXPERTS // FFN_GROUPS)))
        x = _combine(x1, p4[i].reshape(N, PLE_DIM), rank, incl, yes, lw, gfin, cap, i == len(layers) - 1)
    return x.reshape(B, S, D_MODEL)


def kernel(x_prompt, x_sample, p_prompt, p_sample, norm_mix, w_in, qk_norm, sink_logit, conv_dw, conv_dw_b,
           conv_ln_g, conv_ln_b, gla_w2, gla_b2, gla_norm, w_branch, w_out, norm_ffn, w_router, w_gate_e,
           w_up_e, w_down_e, norm_ple, w_ple_gate, w_ple_proj, norm_final):
    w = dict(norm_mix=norm_mix, w_in=w_in, qk_norm=qk_norm, sink_logit=sink_logit, conv_dw=conv_dw,
             conv_dw_b=conv_dw_b, conv_ln_g=conv_ln_g, conv_ln_b=conv_ln_b, gla_w2=gla_w2, gla_b2=gla_b2,
             gla_norm=gla_norm, w_branch=w_branch, w_out=w_out, norm_ffn=norm_ffn, w_router=w_router,
             w_gate_e=w_gate_e, w_up_e=w_up_e, w_down_e=w_down_e, norm_ple=norm_ple,
             w_ple_gate=w_ple_gate, w_ple_proj=w_ple_proj)
    layers = [_layer_weights(i, w) for i in range(norm_mix.shape[0])]
    gfin = norm_final.reshape(1, -1).astype(F32)
    return (_trunk(x_prompt, p_prompt, layers, gfin), _trunk(x_sample, p_sample, layers, gfin))
```
